```python
import jax, jax.numpy as jnp
from jax import lax
import numpy as np

D_MODEL = 1024
BATCH = 8
SEQ = 2048
DEPTH = 2

CHUNK = 64
N_META = 16
ROPE_THETA = 500000.0
ROPE_FRACTION = 4
NORM_EPS = 1e-6

HGRN_HEADS = 4
HGRN_HEAD_DIM = 128
HGRN_WIDTH = HGRN_HEADS * HGRN_HEAD_DIM

ATT_HEADS = 8
ATT_KV_HEADS = 2
ATT_HEAD_DIM = 64
ATT_GROUP = ATT_HEADS // ATT_KV_HEADS
ATT_WIDTH = ATT_HEADS * ATT_HEAD_DIM
ATT_KV_WIDTH = ATT_KV_HEADS * ATT_HEAD_DIM
IDX_HEADS = 8
IDX_HEAD_DIM = 32
TOPK_MAX = 256
Q_BLOCK = 128

N_BRANCHES = 2

N_GROUPS = 4
EXPERTS_PER_GROUP = 8
N_EXPERTS = N_GROUPS * EXPERTS_PER_GROUP
EXPERT_TOP_K = 2
EXPERT_FF = D_MODEL // 4

IN_SPLITS = (HGRN_WIDTH, HGRN_WIDTH, HGRN_WIDTH, HGRN_WIDTH,
             ATT_WIDTH, ATT_KV_WIDTH, ATT_KV_WIDTH,
             IDX_HEADS * IDX_HEAD_DIM, IDX_HEAD_DIM, IDX_HEADS,
             D_MODEL, D_MODEL)
IN_WIDTH = sum(IN_SPLITS)

kernel_name = "hybrid_hgrn2_dsa_hiermoe_streaming"


def rmsnorm(x, g):
    xf = x.astype(jnp.float32)
    y = xf * lax.rsqrt(jnp.mean(xf * xf, axis=-1, keepdims=True) + NORM_EPS)
    return (y * g.astype(jnp.float32)).astype(x.dtype)


def chunk_ids(pos):
    return jnp.where(pos < N_META, 0, (pos - N_META) // CHUNK + 1)


def rope_tables(n_pos, head_dim):
    rot = head_dim // ROPE_FRACTION
    inv = ROPE_THETA ** (-jnp.arange(0, rot, 2, dtype=jnp.float32) / rot)
    ang = jnp.arange(n_pos, dtype=jnp.float32)[:, None] * inv[None, :]
    return jnp.cos(ang), jnp.sin(ang)


def apply_partial_rope(x, cos, sin):
    half = cos.shape[-1]
    rot = 2 * half
    xf = x.astype(jnp.float32)
    x1, x2, rest = xf[..., :half], xf[..., half:rot], xf[..., rot:]
    c, s = cos[:, None, :], sin[:, None, :]
    out = jnp.concatenate([x1 * c - x2 * s, x2 * c + x1 * s, rest], axis=-1)
    return out.astype(x.dtype)


def hgrn_lower_bounds(lb_logits):
    p = jax.nn.softmax(lb_logits.astype(jnp.float32), axis=0)
    c = jnp.cumsum(p, axis=0)
    return c - c[0:1]


def hgrn2_mixer(q, f_pre, inp, g_out, lb, norm_g):
    B, L, _ = q.shape
    f32 = jnp.float32
    z = f_pre.astype(f32)
    f = lb + (1.0 - lb) * jax.nn.sigmoid(z)
    log_f = jnp.log(f)
    k = (1.0 - lb) * jax.nn.sigmoid(-z)
    front = (-N_META) % CHUNK
    back = (-(front + L)) % CHUNK
    T = front + L + back
    n_chunks = T // CHUNK

    def to_chunks(t):
        t = jnp.pad(t.astype(f32), ((0, 0), (front, back), (0, 0)))
        return t.reshape(B, n_chunks, CHUNK, HGRN_HEADS, HGRN_HEAD_DIM).transpose(1, 0, 3, 2, 4)

    qc, kc, vc, gc = to_chunks(q), to_chunks(k), to_chunks(inp), to_chunks(log_f)
    tri = jnp.tril(jnp.ones((CHUNK, CHUNK), dtype=bool))

    def step(S, xs):
        qb, kb, vb, gb = xs
        b = jnp.cumsum(gb, axis=2)
        inter = jnp.einsum('bhtd,bhde->bhte', qb * jnp.exp(b), S)
        diff = jnp.where(tri[None, None, :, :, None],
                         b[:, :, :, None, :] - b[:, :, None, :, :], -jnp.inf)
        A = jnp.einsum('bhtsd,bhsd->bhts', qb[:, :, :, None, :] * jnp.exp(diff), kb)
        intra = jnp.einsum('bhts,bhse->bhte', A, vb)
        b_last = b[:, :, -1:, :]
        S_new = (jnp.exp(b_last[:, :, 0, :])[..., None] * S
                 + jnp.einsum('bhsd,bhse->bhde', kb * jnp.exp(b_last - b), vb))
        return S_new, inter + intra

    S0 = jnp.zeros((B, HGRN_HEADS, HGRN_HEAD_DIM, HGRN_HEAD_DIM), f32)
    _, o = lax.scan(step, S0, (qc, kc, vc, gc))
    o = o.transpose(1, 0, 3, 2, 4).reshape(B, T, HGRN_HEADS, HGRN_HEAD_DIM)[:, front:front + L]
    o = rmsnorm(o, norm_g.reshape(HGRN_HEADS, HGRN_HEAD_DIM)).reshape(B, L, HGRN_WIDTH)
    return (o * jax.nn.silu(g_out.astype(f32))).astype(q.dtype)


def to_q_blocks(t, n_blk):
    pad = n_blk * Q_BLOCK - t.shape[1]
    t = jnp.pad(t, [(0, 0), (0, pad)] + [(0, 0)] * (t.ndim - 2))
    return jnp.moveaxis(t.reshape(t.shape[0], n_blk, Q_BLOCK, *t.shape[2:]), 1, 0)


def dsa_mixer(q, k, v, q_idx, k_idx, w_idx, k_idx_g, cos_a, sin_a, cos_i, sin_i, topk):
    B, L, _ = q.shape
    f32 = jnp.float32
    q = apply_partial_rope(q.reshape(B, L, ATT_HEADS, ATT_HEAD_DIM), cos_a, sin_a)
    k = apply_partial_rope(k.reshape(B, L, ATT_KV_HEADS, ATT_HEAD_DIM), cos_a, sin_a)
    v = v.reshape(B, L, ATT_KV_HEADS, ATT_HEAD_DIM)
    q_idx = apply_partial_rope(q_idx.reshape(B, L, IDX_HEADS, IDX_HEAD_DIM), cos_i, sin_i)
    k_idx = apply_partial_rope(rmsnorm(k_idx, k_idx_g)[:, :, None, :], cos_i, sin_i)[:, :, 0, :]
    w_idx = w_idx.astype(f32) * (IDX_HEADS ** -0.5 * IDX_HEAD_DIM ** -0.5)
    scale = ATT_HEAD_DIM ** -0.5
    n_blk = -(-L // Q_BLOCK)
    cid_k = chunk_ids(jnp.arange(L))
    cid_q = chunk_ids(jnp.arange(n_blk * Q_BLOCK)).reshape(n_blk, Q_BLOCK)

    def one_block(args):
        qb, qib, wb, cqb = args
        rel = jax.nn.relu(jnp.einsum('bqhd,bkd->bqhk', qib, k_idx).astype(f32))
        score = jnp.einsum('bqhk,bqh->bqk', rel, wb)
        admissible = cid_k[None, :] <= cqb[:, None]
        score = jnp.where(admissible[None], score, -jnp.inf)
        _, sel = lax.top_k(score, topk)
        valid = cid_k[sel] <= cqb[None, :, None]
        k_sel = jax.vmap(lambda kb, ib: kb[ib])(k, sel)
        v_sel = jax.vmap(lambda vb, ib: vb[ib])(v, sel)
        qg = qb.reshape(B, Q_BLOCK, ATT_KV_HEADS, ATT_GROUP, ATT_HEAD_DIM)
        s = jnp.einsum('bqgrd,bqkgd->bqgrk', qg, k_sel).astype(f32) * scale
        s = jnp.where(valid[:, :, None, None, :], s, -jnp.inf)
        p = jax.nn.softmax(s, axis=-1).astype(v.dtype)
        o = jnp.einsum('bqgrk,bqkgd->bqgrd', p, v_sel)
        return o.reshape(B, Q_BLOCK, ATT_WIDTH)

    out = lax.map(one_block, (to_q_blocks(q, n_blk), to_q_blocks(q_idx, n_blk),
                              to_q_blocks(w_idx, n_blk), cid_q))
    return jnp.moveaxis(out, 0, 1).reshape(B, n_blk * Q_BLOCK, ATT_WIDTH)[:, :L]


def hier_moe(h, wg_r, bg_r, we_r, be_r, w_gate, w_up, w_down):
    B, L, D = h.shape
    N = B * L
    f32 = jnp.float32
    hf = h.reshape(N, D)
    pg = jax.nn.softmax((hf @ wg_r).astype(f32) + bg_r.astype(f32), axis=-1)
    pg_top, g_sel = lax.top_k(pg, 1)
    le = ((hf @ we_r).astype(f32) + be_r.astype(f32)).reshape(N, N_GROUPS, EXPERTS_PER_GROUP)
    le_sel = jnp.take_along_axis(le, g_sel[:, :, None], axis=1)[:, 0]
    pe = jax.nn.softmax(le_sel, axis=-1)
    pe_top, e_sel = lax.top_k(pe, EXPERT_TOP_K)
    w_e = pe_top / jnp.sum(pe_top, axis=-1, keepdims=True) * pg_top
    g_onehot = jax.nn.one_hot(g_sel[:, 0], N_GROUPS, dtype=f32)
    e_w = jnp.sum(jax.nn.one_hot(e_sel, EXPERTS_PER_GROUP, dtype=f32) * w_e[..., None], axis=1)
    combine = (g_onehot[:, :, None] * e_w[:, None, :]).transpose(1, 0, 2)
    wg = w_gate.reshape(N_GROUPS, EXPERTS_PER_GROUP, D, EXPERT_FF)
    wu = w_up.reshape(N_GROUPS, EXPERTS_PER_GROUP, D, EXPERT_FF)
    wd = w_down.reshape(N_GROUPS, EXPERTS_PER_GROUP, EXPERT_FF, D)

    def group_out(args):
        g_w, u_w, d_w, c = args
        a = jnp.einsum('nd,edf->nef', hf, g_w)
        u = jnp.einsum('nd,edf->nef', hf, u_w)
        act = jax.nn.silu(a) * u * c[:, :, None].astype(h.dtype)
        return jnp.einsum('nef,efd->nd', act, d_w)

    y = jnp.sum(lax.map(group_out, (wg, wu, wd, combine)), axis=0)
    return y.reshape(B, L, D).astype(h.dtype)


def setup_inputs(seed: int = 0) -> dict:
    key = jax.random.key(seed)
    ks = jax.random.split(key, 20)
    f32 = jnp.float32
    nrm = lambda k, shape, s: jax.random.normal(k, shape, f32) * s
    D = D_MODEL
    return {
        "x": nrm(ks[0], (BATCH, SEQ, D), 1.0),
        "meta_tokens": nrm(ks[1], (N_META, D), 1.0),
        "mix_norm_g": 1.0 + nrm(ks[2], (DEPTH, D), 0.02),
        "w_in": nrm(ks[3], (DEPTH, D, IN_WIDTH), D ** -0.5),
        "hgrn_lb_logits": nrm(ks[4], (DEPTH, HGRN_WIDTH), 0.5),
        "hgrn_norm_g": 1.0 + nrm(ks[5], (DEPTH, HGRN_WIDTH), 0.02),
        "idx_k_norm_g": 1.0 + nrm(ks[6], (DEPTH, IDX_HEAD_DIM), 0.02),
        "w_branch_hgrn": nrm(ks[7], (DEPTH, HGRN_WIDTH, D), HGRN_WIDTH ** -0.5),
        "w_branch_dsa": nrm(ks[8], (DEPTH, ATT_WIDTH, D), ATT_WIDTH ** -0.5),
        "w_out": nrm(ks[9], (DEPTH, D, D), D ** -0.5),
        "ffn_norm_g": 1.0 + nrm(ks[10], (DEPTH, D), 0.02),
        "router_group_w": nrm(ks[11], (DEPTH, D, N_GROUPS), D ** -0.5),
        "router_group_b": nrm(ks[12], (DEPTH, N_GROUPS), 0.01),
        "router_expert_w": nrm(ks[13], (DEPTH, D, N_EXPERTS), D ** -0.5),
        "router_expert_b": nrm(ks[14], (DEPTH, N_EXPERTS), 0.01),
        "w_expert_gate": nrm(ks[15], (DEPTH, N_EXPERTS, D, EXPERT_FF), D ** -0.5),
        "w_expert_up": nrm(ks[16], (DEPTH, N_EXPERTS, D, EXPERT_FF), D ** -0.5),
        "w_expert_down": nrm(ks[17], (DEPTH, N_EXPERTS, EXPERT_FF, D), EXPERT_FF ** -0.5),
        "final_norm_g": 1.0 + nrm(ks[18], (D,), 0.02),
    }


def reference(x, meta_tokens, mix_norm_g, w_in, hgrn_lb_logits, hgrn_norm_g, idx_k_norm_g,
              w_branch_hgrn, w_branch_dsa, w_out, ffn_norm_g, router_group_w, router_group_b,
              router_expert_w, router_expert_b, w_expert_gate, w_expert_up, w_expert_down,
              final_norm_g):
    B = x.shape[0]
    meta = jnp.broadcast_to(meta_tokens[None].astype(x.dtype), (B, N_META, D_MODEL))
    h = jnp.concatenate([meta, x], axis=1)
    L = h.shape[1]
    cos_a, sin_a = rope_tables(L, ATT_HEAD_DIM)
    cos_i, sin_i = rope_tables(L, IDX_HEAD_DIM)
    topk = min(TOPK_MAX, L // 4)
    lower_bounds = hgrn_lower_bounds(hgrn_lb_logits)
    offsets = np.cumsum(IN_SPLITS)[:-1].tolist()
    for layer in range(DEPTH):
        xn = rmsnorm(h, mix_norm_g[layer])
        proj = xn @ w_in[layer]
        (hq, hf, hi, hg, aq, ak, av, iq, ik, iw, gate_a, gate_b) = jnp.split(proj, offsets, axis=-1)
        y_a = hgrn2_mixer(hq, hf, hi, hg, lower_bounds[layer], hgrn_norm_g[layer])
        y_b = dsa_mixer(aq, ak, av, iq, ik, iw, idx_k_norm_g[layer],
                        cos_a, sin_a, cos_i, sin_i, topk)
        merged = (jax.nn.sigmoid(gate_a) * (y_a @ w_branch_hgrn[layer])
                  + jax.nn.sigmoid(gate_b) * (y_b @ w_branch_dsa[layer]))
        h = h + merged @ w_out[layer]
        h = h + hier_moe(rmsnorm(h, ffn_norm_g[layer]), router_group_w[layer], router_group_b[layer],
                         router_expert_w[layer], router_expert_b[layer], w_expert_gate[layer],
                         w_expert_up[layer], w_expert_down[layer])
    return rmsnorm(h, final_norm_g)[:, N_META:]
```

```python
import functools

import jax
import jax.numpy as jnp
import numpy as np
from jax import lax
from jax.experimental import pallas as pl
from jax.experimental.pallas import tpu as pltpu

F32 = jnp.float32
BF16 = jnp.bfloat16

D_MODEL = 1024
CHUNK = 64
N_META = 16
FRONT = CHUNK - N_META
ROPE_THETA = 500000.0
NORM_EPS = 1e-6
LANES = 128

HGRN_HEADS = 4
HGRN_HEAD_DIM = 128
HGRN_WIDTH = HGRN_HEADS * HGRN_HEAD_DIM
SUB = 16

ATT_HEADS = 8
ATT_KV_HEADS = 2
ATT_HEAD_DIM = 64
ATT_GROUP = ATT_HEADS // ATT_KV_HEADS
ATT_WIDTH = ATT_HEADS * ATT_HEAD_DIM
ATT_KV_WIDTH = ATT_KV_HEADS * ATT_HEAD_DIM
ATT_ROPE_HALF = ATT_HEAD_DIM // 4 // 2
IDX_HEADS = 8
IDX_HEAD_DIM = 32
IDX_WIDTH = IDX_HEADS * IDX_HEAD_DIM
IDX_ROPE_HALF = IDX_HEAD_DIM // 4 // 2
TOPK_MAX = 256
Q_BLOCK = 128

N_GROUPS = 4
EXPERTS_PER_GROUP = 8
N_EXPERTS = N_GROUPS * EXPERTS_PER_GROUP
EXPERT_FF = D_MODEL // 4

COL_HG = 0
COL_ATT = 4 * HGRN_WIDTH
COL_IDX = COL_ATT + ATT_WIDTH + 2 * ATT_KV_WIDTH
IDX_RAW = IDX_WIDTH + IDX_HEAD_DIM + IDX_HEADS
IDX_PAD = IDX_WIDTH + LANES
COL_GATE = COL_IDX + IDX_PAD
IN_WIDTH_PAD = COL_GATE + 2 * D_MODEL

INT_MIN = np.int32(-2 ** 31)
NEG_INF = float("-inf")

VMEM_LIMIT = 56 * 1024 * 1024


def _dot(a, b):
    return jnp.dot(a, b, preferred_element_type=F32)


def _dot_nt(a, b):
    return lax.dot_general(a, b, (((1,), (1,)), ((), ())), preferred_element_type=F32)


def _rope(x, c, s1, s2, half):
    return x * c + pltpu.roll(x, half, 1) * s1 + pltpu.roll(x, LANES - half, 1) * s2


def _inproj_kernel(h_ref, g_ref, w_ref, tab_ref, ikg_ref, rep_ref,
                   hg_ref, gate_ref, q_ref, k_ref, v_ref, iq_ref, ik_ref, iw_ref):
    x = h_ref[...]
    xn = (x * lax.rsqrt(jnp.mean(x * x, axis=-1, keepdims=True) + NORM_EPS) * g_ref[...]).astype(BF16)
    hg_ref[...] = _dot(xn, w_ref[:, COL_HG:COL_ATT])
    gate_ref[...] = _dot(xn, w_ref[:, COL_GATE:IN_WIDTH_PAD])
    att = _dot(xn, w_ref[:, COL_ATT:COL_IDX])
    idx = _dot(xn, w_ref[:, COL_IDX:COL_GATE])
    ca, s1a, s2a = tab_ref[0], tab_ref[1], tab_ref[2]
    ci, s1i, s2i = tab_ref[3], tab_ref[4], tab_ref[5]
    scale = ATT_HEAD_DIM ** -0.5
    for m in range(ATT_WIDTH // LANES):
        sl = slice(m * LANES, (m + 1) * LANES)
        q_ref[:, sl] = (_rope(att[:, sl], ca, s1a, s2a, ATT_ROPE_HALF) * scale).astype(BF16)
    k_ref[...] = _rope(att[:, ATT_WIDTH:ATT_WIDTH + LANES], ca, s1a, s2a, ATT_ROPE_HALF).astype(BF16)
    v_ref[...] = att[:, ATT_WIDTH + LANES:].astype(BF16)
    for m in range(IDX_WIDTH // LANES):
        sl = slice(m * LANES, (m + 1) * LANES)
        iq_ref[:, sl] = _rope(idx[:, sl], ci, s1i, s2i, IDX_ROPE_HALF).astype(BF16)
    xk = idx[:, IDX_WIDTH:]
    lane = lax.broadcasted_iota(jnp.int32, xk.shape, 1)
    ms = jnp.sum(jnp.where(lane < IDX_HEAD_DIM, xk * xk, 0.0), axis=-1, keepdims=True) / IDX_HEAD_DIM
    ikn = xk * lax.rsqrt(ms + NORM_EPS) * ikg_ref[...]
    ikr = _rope(ikn, ci, s1i, s2i, IDX_ROPE_HALF).astype(BF16)
    ik_ref[...] = _dot(ikr, rep_ref[...]).astype(BF16)
    iw_ref[...] = xk * (IDX_HEADS ** -0.5 * IDX_HEAD_DIM ** -0.5)


def _inproj(h, g, w, tabs, ikg, rep, lp):
    n = h.shape[0]
    tm = lp // 8
    per = lp // tm
    row = lambda i: (i, 0)
    const2 = lambda i: (0, 0)
    out_shape = (
        jax.ShapeDtypeStruct((n, 4 * HGRN_WIDTH), F32),
        jax.ShapeDtypeStruct((n, 2 * D_MODEL), F32),
        jax.ShapeDtypeStruct((n, ATT_WIDTH), BF16),
        jax.ShapeDtypeStruct((n, LANES), BF16),
        jax.ShapeDtypeStruct((n, LANES), BF16),
        jax.ShapeDtypeStruct((n, IDX_WIDTH), BF16),
        jax.ShapeDtypeStruct((n, IDX_WIDTH), BF16),
        jax.ShapeDtypeStruct((n, LANES), F32),
    )
    return pl.pallas_call(
        _inproj_kernel,
        grid=(n // tm,),
        in_specs=[
            pl.BlockSpec((tm, D_MODEL), row),
            pl.BlockSpec((1, D_MODEL), const2),
            pl.BlockSpec((D_MODEL, IN_WIDTH_PAD), const2),
            pl.BlockSpec((6, tm, LANES), lambda i: (0, i % per, 0)),
            pl.BlockSpec((1, LANES), const2),
            pl.BlockSpec((LANES, IDX_WIDTH), const2),
        ],
        out_specs=tuple(pl.BlockSpec((tm, s.shape[1]), row) for s in out_shape),
        out_shape=out_shape,
        compiler_params=pltpu.CompilerParams(
            dimension_semantics=("arbitrary",), vmem_limit_bytes=VMEM_LIMIT),
        name="inproj",
    )(h, g, w, tabs, ikg, rep)


def _hgrn_kernel(q_ref, f_ref, i_ref, g_ref, lb_ref, ng_ref, o_ref, st_ref, kb_ref, bb_ref, vb_ref,
                 *, n_chunks):
    C = CHUNK
    st_ref[...] = jnp.zeros_like(st_ref)
    kb_ref[...] = jnp.zeros_like(kb_ref)
    bb_ref[...] = jnp.zeros_like(bb_ref)
    vb_ref[...] = jnp.zeros_like(vb_ref)
    lb = lb_ref[...]
    ng = ng_ref[...]
    row = lax.broadcasted_iota(jnp.int32, (C, 1), 0)
    tri = (lax.broadcasted_iota(jnp.int32, (C, C), 0) >= lax.broadcasted_iota(jnp.int32, (C, C), 1)).astype(F32)
    sub_pos = row & (SUB - 1)

    def body(c, carry):
        r0 = pl.multiple_of(c * C, C)
        rows = pl.ds(r0, C)
        z = f_ref[rows, :]
        q = q_ref[rows, :]
        v = i_ref[rows, :]
        go = g_ref[rows, :]
        f = lb + (1.0 - lb) * jax.nn.sigmoid(z)
        logf = jnp.log(f)
        k = (1.0 - lb) * jax.nn.sigmoid(-z)
        pad = jnp.logical_and(c == 0, row < FRONT)
        logf = jnp.where(pad, 0.0, logf)
        k = jnp.where(pad, 0.0, k)
        v = jnp.where(pad, 0.0, v)
        b = jnp.dot(tri, logf, preferred_element_type=F32, precision=lax.Precision.HIGHEST)

        st = st_ref[...]
        o = _dot_nt((q * jnp.exp(b)).astype(BF16), st.astype(BF16))

        kb_ref[SUB:, :] = k
        bb_ref[SUB:, :] = b
        vb_ref[SUB:, :] = v
        for j in range(SUB):
            ks = kb_ref[SUB - j:SUB - j + C, :]
            bs = bb_ref[SUB - j:SUB - j + C, :]
            vs = vb_ref[SUB - j:SUB - j + C, :]
            w = jnp.exp(jnp.minimum(b - bs, 0.0))
            a = jnp.sum(q * ks * w, axis=-1, keepdims=True)
            a = jnp.where(sub_pos >= j, a, 0.0)
            o = o + a * vs

        pieces = [jnp.zeros((SUB, HGRN_HEAD_DIM), F32)]
        for i in range(1, C // SUB):
            lo = i * SUB
            ref_b = b[lo - 1:lo, :]
            qi = (q[lo:lo + SUB, :] * jnp.exp(b[lo:lo + SUB, :] - ref_b)).astype(BF16)
            kj = (k[:lo, :] * jnp.exp(ref_b - b[:lo, :])).astype(BF16)
            a_off = _dot_nt(qi, kj)
            pieces.append(_dot(a_off.astype(BF16), v[:lo, :].astype(BF16)))
        o = o + jnp.concatenate(pieces, axis=0)

        b_last = b[C - 1:C, :]
        kd = (k * jnp.exp(b_last - b)).astype(BF16)
        st_ref[...] = st * jnp.exp(b_last) + _dot(v.T.astype(BF16), kd)

        on = o * lax.rsqrt(jnp.mean(o * o, axis=-1, keepdims=True) + NORM_EPS) * ng
        o_ref[rows, :] = (on * (go * jax.nn.sigmoid(go))).astype(o_ref.dtype)
        return carry

    lax.fori_loop(0, n_chunks, body, 0)
    tail = o_ref.shape[0] - n_chunks * C
    if tail:
        o_ref[n_chunks * C:, :] = jnp.zeros((tail, HGRN_HEAD_DIM), o_ref.dtype)


def _hgrn(hg, lb, ng, batch, lp, n_chunks):
    n = hg.shape[0]
    H = HGRN_HEADS

    def col(k):
        return pl.BlockSpec((lp, HGRN_HEAD_DIM), lambda b, h: (b, k * H + h))

    vec = pl.BlockSpec((1, HGRN_HEAD_DIM), lambda b, h: (0, h))
    return pl.pallas_call(
        functools.partial(_hgrn_kernel, n_chunks=n_chunks),
        grid=(batch, H),
        in_specs=[col(0), col(1), col(2), col(3), vec, vec],
        out_specs=pl.BlockSpec((lp, HGRN_HEAD_DIM), lambda b, h: (b, h)),
        out_shape=jax.ShapeDtypeStruct((n, HGRN_WIDTH), BF16),
        scratch_shapes=[
            pltpu.VMEM((HGRN_HEAD_DIM, HGRN_HEAD_DIM), F32),
            pltpu.VMEM((CHUNK + SUB, HGRN_HEAD_DIM), F32),
            pltpu.VMEM((CHUNK + SUB, HGRN_HEAD_DIM), F32),
            pltpu.VMEM((CHUNK + SUB, HGRN_HEAD_DIM), F32),
        ],
        compiler_params=pltpu.CompilerParams(
            dimension_semantics=("arbitrary", "arbitrary"), vmem_limit_bytes=VMEM_LIMIT),
        name="hgrn2",
    )(hg, hg, hg, hg, lb, ng)


def _dsa_kernel(q_ref, iq_ref, iw_ref, k_ref, v_ref, ik_ref, u2_ref, o_ref, key_ref, bias_ref,
                *, topk, n_valid):
    QB = Q_BLOCK
    lp = k_ref.shape[0]
    qi = pl.program_id(1)

    iq = iq_ref[...]
    iw = iw_ref[...]
    ik = ik_ref[...]
    head_of_lane = lax.broadcasted_iota(jnp.int32, iq.shape, 1) >> 5
    score = jnp.zeros((QB, lp), F32)
    for h in range(IDX_HEADS):
        qm = jnp.where(head_of_lane == h, iq, jnp.zeros_like(iq))
        rel = jnp.maximum(_dot_nt(qm, ik), 0.0)
        score = score + rel * iw[:, IDX_HEAD_DIM + h:IDX_HEAD_DIM + h + 1]
    score = jnp.where(score == 0.0, 0.0, score)

    kpos = lax.broadcasted_iota(jnp.int32, (QB, lp), 1)
    qrow = qi * QB + lax.broadcasted_iota(jnp.int32, (QB, 1), 0)
    adm = ((kpos >> 6) <= (qrow >> 6)) & (kpos >= FRONT) & (kpos < n_valid)

    bits = pltpu.bitcast(score, jnp.int32)
    key = jnp.where(bits < 0, bits ^ jnp.int32(0x7FFFFFFF), bits)
    key_ref[...] = jnp.where(adm, key, INT_MIN)

    def search(i, t):
        cand = t + lax.shift_left(jnp.int32(1), 31 - i)
        cnt = jnp.sum(jnp.where(key_ref[...] >= cand, 1.0, 0.0), axis=-1, keepdims=True)
        return jnp.where(cnt >= topk, cand, t)

    thr = lax.fori_loop(0, 32, search, jnp.full((QB, 1), INT_MIN, jnp.int32))

    key = key_ref[...]
    gt = key > thr
    need = topk - jnp.sum(jnp.where(gt, 1.0, 0.0), axis=-1, keepdims=True)
    carry = jnp.zeros((QB, LANES), F32)
    u2 = u2_ref[...]
    for jb in range(lp // LANES):
        sl = slice(jb * LANES, (jb + 1) * LANES)
        eq = key[:, sl] == thr
        r = _dot(jnp.where(eq, 1.0, 0.0).astype(BF16), u2)
        rank = r[:, :LANES] + carry
        carry = carry + r[:, LANES:]
        take = gt[:, sl] | (eq & (rank <= need))
        bias_ref[:, sl] = jnp.where(take & adm[:, sl], 0.0, NEG_INF)

    kk = k_ref[...]
    vv = v_ref[...]
    lane_half = lax.broadcasted_iota(jnp.int32, (QB, LANES), 1) >> 6
    for m in range(ATT_WIDTH // LANES):
        out = jnp.zeros((QB, LANES), F32)
        qg = q_ref[:, m * LANES:(m + 1) * LANES].astype(F32)
        for p in range(LANES // ATT_HEAD_DIM):
            h = m * (LANES // ATT_HEAD_DIM) + p
            g = h // ATT_GROUP
            qh = qg if p == g else pltpu.roll(qg, ATT_HEAD_DIM, 1)
            qh = jnp.where(lane_half == g, qh, 0.0).astype(BF16)
            s = _dot_nt(qh, kk) + bias_ref[...]
            e = jnp.exp(s - jnp.max(s, axis=-1, keepdims=True))
            pv = _dot(e.astype(BF16), vv) / jnp.sum(e, axis=-1, keepdims=True)
            if p != g:
                pv = pltpu.roll(pv, ATT_HEAD_DIM, 1)
            out = jnp.where(lane_half == p, pv, out)
        o_ref[:, m * LANES:(m + 1) * LANES] = out.astype(o_ref.dtype)


def _dsa(q, k, v, iq, ik, iw, u2, batch, lp, topk, n_valid):
    n = q.shape[0]
    nq = lp // Q_BLOCK
    qblk = lambda b, i: (b * nq + i, 0)
    bat = lambda b, i: (b, 0)
    return pl.pallas_call(
        functools.partial(_dsa_kernel, topk=topk, n_valid=n_valid),
        grid=(batch, nq),
        in_specs=[
            pl.BlockSpec((Q_BLOCK, ATT_WIDTH), qblk),
            pl.BlockSpec((Q_BLOCK, IDX_WIDTH), qblk),
            pl.BlockSpec((Q_BLOCK, LANES), qblk),
            pl.BlockSpec((lp, LANES), bat),
            pl.BlockSpec((lp, LANES), bat),
            pl.BlockSpec((lp, IDX_WIDTH), bat),
            pl.BlockSpec((LANES, 2 * LANES), lambda b, i: (0, 0)),
        ],
        out_specs=pl.BlockSpec((Q_BLOCK, ATT_WIDTH), qblk),
        out_shape=jax.ShapeDtypeStruct((n, ATT_WIDTH), BF16),
        scratch_shapes=[
            pltpu.VMEM((Q_BLOCK, lp), jnp.int32),
            pltpu.VMEM((Q_BLOCK, lp), F32),
        ],
        compiler_params=pltpu.CompilerParams(
            dimension_semantics=("arbitrary", "arbitrary"), vmem_limit_bytes=VMEM_LIMIT),
        name="dsa",
    )(q, iq, iw, k, v, ik, u2)


def _merge_kernel(ya_ref, yb_ref, gate_ref, h_ref, wa_ref, wb_ref, wo_ref, fg_ref, wr_ref, br_ref,
                  ho_ref, xn_ref, cmb_ref):
    gate = gate_ref[...]
    pa = _dot(ya_ref[...], wa_ref[...])
    pb = _dot(yb_ref[...], wb_ref[...])
    merged = jax.nn.sigmoid(gate[:, :D_MODEL]) * pa + jax.nn.sigmoid(gate[:, D_MODEL:]) * pb
    h = h_ref[...] + _dot(merged.astype(BF16), wo_ref[...])
    ho_ref[...] = h
    xn = h * lax.rsqrt(jnp.mean(h * h, axis=-1, keepdims=True) + NORM_EPS) * fg_ref[...]
    xn_ref[...] = xn.astype(BF16)

    lg = jnp.dot(xn, wr_ref[...], preferred_element_type=F32, precision=lax.Precision.HIGHEST) + br_ref[...]
    lane = lax.broadcasted_iota(jnp.int32, lg.shape, 1)
    lane_f = lane.astype(F32)
    big = float(LANES)
    is_g = (lane >= N_EXPERTS) & (lane < N_EXPERTS + N_GROUPS)
    gl = jnp.where(is_g, lg, NEG_INF)
    gmax = jnp.max(gl, axis=-1, keepdims=True)
    pg_top = 1.0 / jnp.sum(jnp.exp(gl - gmax), axis=-1, keepdims=True)
    g_lane = jnp.min(jnp.where(gl == gmax, lane_f, big), axis=-1, keepdims=True)
    e_lo = (g_lane - N_EXPERTS) * EXPERTS_PER_GROUP
    in_grp = (lane_f >= e_lo) & (lane_f < e_lo + EXPERTS_PER_GROUP)
    el = jnp.where(in_grp, lg, NEG_INF)
    ee = jnp.exp(el - jnp.max(el, axis=-1, keepdims=True))
    pe = ee / jnp.sum(ee, axis=-1, keepdims=True)
    pe = jnp.where(in_grp, pe, -1.0)
    p1 = jnp.max(pe, axis=-1, keepdims=True)
    i1 = jnp.min(jnp.where(pe == p1, lane_f, big), axis=-1, keepdims=True)
    pe2 = jnp.where(lane_f == i1, -1.0, pe)
    p2 = jnp.max(pe2, axis=-1, keepdims=True)
    i2 = jnp.min(jnp.where(pe2 == p2, lane_f, big), axis=-1, keepdims=True)
    tot = p1 + p2
    cmb_ref[...] = jnp.where(lane_f == i1, p1 / tot * pg_top,
                             jnp.where(lane_f == i2, p2 / tot * pg_top, 0.0))


def _merge(ya, yb, gates, h, wa, wb, wo, fg, wr, br):
    n = h.shape[0]
    tm = min(256, n)
    row = lambda i: (i, 0)
    const = lambda i: (0, 0)
    return pl.pallas_call(
        _merge_kernel,
        grid=(n // tm,),
        in_specs=[
            pl.BlockSpec((tm, HGRN_WIDTH), row),
            pl.BlockSpec((tm, ATT_WIDTH), row),
            pl.BlockSpec((tm, 2 * D_MODEL), row),
            pl.BlockSpec((tm, D_MODEL), row),
            pl.BlockSpec((HGRN_WIDTH, D_MODEL), const),
            pl.BlockSpec((ATT_WIDTH, D_MODEL), const),
            pl.BlockSpec((D_MODEL, D_MODEL), const),
            pl.BlockSpec((1, D_MODEL), const),
            pl.BlockSpec((D_MODEL, LANES), const),
            pl.BlockSpec((1, LANES), const),
        ],
        out_specs=(pl.BlockSpec((tm, D_MODEL), row), pl.BlockSpec((tm, D_MODEL), row),
                   pl.BlockSpec((tm, LANES), row)),
        out_shape=(jax.ShapeDtypeStruct((n, D_MODEL), F32), jax.ShapeDtypeStruct((n, D_MODEL), BF16),
                   jax.ShapeDtypeStruct((n, LANES), F32)),
        compiler_params=pltpu.CompilerParams(
            dimension_semantics=("arbitrary",), vmem_limit_bytes=VMEM_LIMIT),
        name="merge_router",
    )(ya, yb, gates, h, wa, wb, wo, fg, wr, br)


def _moe_kernel(x_ref, c_ref, h_ref, wg_ref, wu_ref, wd_ref, o_ref, acc_ref):
    e = pl.program_id(1)

    @pl.when(e == 0)
    def _():
        acc_ref[...] = h_ref[...]

    x = x_ref[...]
    a = _dot(x, wg_ref[0])
    u = _dot(x, wu_ref[0])
    c = c_ref[...]
    lane = lax.broadcasted_iota(jnp.int32, c.shape, 1)
    ce = jnp.sum(jnp.where(lane == e, c, 0.0), axis=-1, keepdims=True)
    act = (a * jax.nn.sigmoid(a)) * u * ce
    acc_ref[...] += _dot(act.astype(BF16), wd_ref[0])

    @pl.when(e == pl.num_programs(1) - 1)
    def _():
        o_ref[...] = acc_ref[...]


def _moe(xn, cmb, h, wg, wu, wd):
    n = h.shape[0]
    tm = min(1024, n)
    row = lambda i, e: (i, 0)
    return pl.pallas_call(
        _moe_kernel,
        grid=(n // tm, N_EXPERTS),
        in_specs=[
            pl.BlockSpec((tm, D_MODEL), row),
            pl.BlockSpec((tm, LANES), row),
            pl.BlockSpec((tm, D_MODEL), row),
            pl.BlockSpec((1, D_MODEL, EXPERT_FF), lambda i, e: (e, 0, 0)),
            pl.BlockSpec((1, D_MODEL, EXPERT_FF), lambda i, e: (e, 0, 0)),
            pl.BlockSpec((1, EXPERT_FF, D_MODEL), lambda i, e: (e, 0, 0)),
        ],
        out_specs=pl.BlockSpec((tm, D_MODEL), row),
        out_shape=jax.ShapeDtypeStruct((n, D_MODEL), F32),
        scratch_shapes=[pltpu.VMEM((tm, D_MODEL), F32)],
        compiler_params=pltpu.CompilerParams(
            dimension_semantics=("arbitrary", "arbitrary"), vmem_limit_bytes=VMEM_LIMIT),
        name="moe_experts",
    )(xn, cmb, h, wg, wu, wd)


def _final_kernel(h_ref, g_ref, o_ref, *, seq):
    step = 256 if seq % 256 == 0 else CHUNK
    for r in range(0, seq, step):
        x = h_ref[CHUNK + r:CHUNK + r + step, :]
        o_ref[0, r:r + step, :] = x * lax.rsqrt(jnp.mean(x * x, axis=-1, keepdims=True) + NORM_EPS) * g_ref[...]


def _final_norm(h, g, batch, lp, seq):
    return pl.pallas_call(
        functools.partial(_final_kernel, seq=seq),
        grid=(batch,),
        in_specs=[pl.BlockSpec((lp, D_MODEL), lambda b: (b, 0)), pl.BlockSpec((1, D_MODEL), lambda b: (0, 0))],
        out_specs=pl.BlockSpec((1, seq, D_MODEL), lambda b: (b, 0, 0)),
        out_shape=jax.ShapeDtypeStruct((batch, seq, D_MODEL), F32),
        compiler_params=pltpu.CompilerParams(
            dimension_semantics=("arbitrary",), vmem_limit_bytes=VMEM_LIMIT),
        name="final_norm",
    )(h, g)


def _rope_lane_tables(pos, head_dim):
    rot = head_dim // 4
    half = rot // 2
    inv = ROPE_THETA ** (-jnp.arange(0, rot, 2, dtype=F32) / rot)
    ang = pos[:, None] * inv[None, :]
    cos, sin = jnp.cos(ang), jnp.sin(ang)
    jj = np.arange(LANES) % head_dim
    first = jnp.asarray(jj < half)
    second = jnp.asarray((jj >= half) & (jj < rot))
    fidx = jnp.asarray(np.where(jj < half, jj, np.where(jj < rot, jj - half, 0)))
    cl, sl = cos[:, fidx], sin[:, fidx]
    c = jnp.where(first | second, cl, 1.0)
    s1 = jnp.where(second, sl, 0.0)
    s2 = jnp.where(first, -sl, 0.0)
    return c, s1, s2


def kernel(x, meta_tokens, mix_norm_g, w_in, hgrn_lb_logits, hgrn_norm_g, idx_k_norm_g, w_branch_hgrn,
           w_branch_dsa, w_out, ffn_norm_g, router_group_w, router_group_b, router_expert_w,
           router_expert_b, w_expert_gate, w_expert_up, w_expert_down, final_norm_g):
    batch, seq, _ = x.shape
    depth = w_in.shape[0]
    n_valid = CHUNK + seq
    assert seq % CHUNK == 0
    lp = -(-n_valid // Q_BLOCK) * Q_BLOCK
    n_chunks = n_valid // CHUNK
    topk = min(TOPK_MAX, (N_META + seq) // 4)

    meta = jnp.broadcast_to(meta_tokens[None].astype(x.dtype), (batch, N_META, D_MODEL))
    h = jnp.concatenate([jnp.zeros((batch, FRONT, D_MODEL), x.dtype), meta, x,
                         jnp.zeros((batch, lp - n_valid, D_MODEL), x.dtype)], axis=1)
    h = h.reshape(batch * lp, D_MODEL)

    pos = jnp.clip(jnp.arange(lp, dtype=jnp.int32) - FRONT, 0, N_META + seq - 1).astype(F32)
    tabs = jnp.stack(_rope_lane_tables(pos, ATT_HEAD_DIM) + _rope_lane_tables(pos, IDX_HEAD_DIM))

    p = jax.nn.softmax(hgrn_lb_logits.astype(F32), axis=0)
    cs = jnp.cumsum(p, axis=0)
    lower_bounds = cs - cs[0:1]

    rep = np.zeros((LANES, IDX_WIDTH), np.float32)
    for hh in range(IDX_HEADS):
        rep[np.arange(IDX_HEAD_DIM), hh * IDX_HEAD_DIM + np.arange(IDX_HEAD_DIM)] = 1.0
    rep = jnp.asarray(rep, BF16)
    tri = np.triu(np.ones((LANES, LANES), np.float32))
    u2 = jnp.asarray(np.concatenate([tri, np.ones((LANES, LANES), np.float32)], axis=1), BF16)

    pad_cols = jnp.zeros((D_MODEL, IDX_PAD - IDX_RAW), F32)
    for layer in range(depth):
        w = w_in[layer]
        w_p = jnp.concatenate([w[:, :COL_IDX + IDX_RAW], pad_cols, w[:, COL_IDX + IDX_RAW:]], axis=1).astype(BF16)
        ikg = jnp.pad(idx_k_norm_g[layer].astype(F32), (0, LANES - IDX_HEAD_DIM))[None]
        hg, gates, q, k, v, iq, ik, iw = _inproj(h, mix_norm_g[layer][None].astype(F32), w_p, tabs, ikg, rep, lp)
        ya = _hgrn(hg, lower_bounds[layer][None], hgrn_norm_g[layer][None].astype(F32), batch, lp, n_chunks)
        yb = _dsa(q, k, v, iq, ik, iw, u2, batch, lp, topk, n_valid)
        wr = jnp.concatenate([router_expert_w[layer], router_group_w[layer],
                              jnp.zeros((D_MODEL, LANES - N_EXPERTS - N_GROUPS), F32)], axis=1)
        br = jnp.concatenate([router_expert_b[layer], router_group_b[layer],
                              jnp.zeros((LANES - N_EXPERTS - N_GROUPS,), F32)])[None]
        h, xn, cmb = _merge(ya, yb, gates, h, w_branch_hgrn[layer].astype(BF16), w_branch_dsa[layer].astype(BF16),
                            w_out[layer].astype(BF16), ffn_norm_g[layer][None].astype(F32), wr, br)
        h = _moe(xn, cmb, h, w_expert_gate[layer].astype(BF16), w_expert_up[layer].astype(BF16),
                 w_expert_down[layer].astype(BF16))
    return _final_norm(h, final_norm_g[None].astype(F32), batch, lp, seq)
```

```python
import functools

import jax
import jax.numpy as jnp
import numpy as np
from jax import lax
from jax.experimental import pallas as pl
from jax.experimental.pallas import tpu as pltpu

F32 = jnp.float32
BF16 = jnp.bfloat16

D_MODEL = 1024
CHUNK = 64
N_META = 16
FRONT = CHUNK - N_META
ROPE_THETA = 500000.0
NORM_EPS = 1e-6
LANES = 128
SUBLANES = 8

HGRN_HEADS = 4
HGRN_HEAD_DIM = 128
HGRN_WIDTH = HGRN_HEADS * HGRN_HEAD_DIM
SUB = 16

ATT_HEADS = 8
ATT_KV_HEADS = 2
ATT_HEAD_DIM = 64
ATT_GROUP = ATT_HEADS // ATT_KV_HEADS
ATT_WIDTH = ATT_HEADS * ATT_HEAD_DIM
ATT_KV_WIDTH = ATT_KV_HEADS * ATT_HEAD_DIM
ATT_ROPE_HALF = ATT_HEAD_DIM // 4 // 2
IDX_HEADS = 8
IDX_HEAD_DIM = 32
IDX_WIDTH = IDX_HEADS * IDX_HEAD_DIM
IDX_ROPE_HALF = IDX_HEAD_DIM // 4 // 2
IDX_HEADS_PER_LANE_GROUP = LANES // IDX_HEAD_DIM
TOPK_MAX = 256
Q_BLOCK = 128
DSA_EXTENT_STEP = 4
FOLD = 64

N_GROUPS = 4
EXPERTS_PER_GROUP = 8
N_EXPERTS = N_GROUPS * EXPERTS_PER_GROUP
EXPERT_FF = D_MODEL // 4

COL_HG = 0
COL_ATT = 4 * HGRN_WIDTH
COL_IDX = COL_ATT + ATT_WIDTH + 2 * ATT_KV_WIDTH
IDX_RAW = IDX_WIDTH + IDX_HEAD_DIM + IDX_HEADS
IDX_PAD = IDX_WIDTH + LANES
COL_GATE = COL_IDX + IDX_PAD
IN_WIDTH_PAD = COL_GATE + 2 * D_MODEL

INT_MIN = np.int32(-2 ** 31)
NEG_FLT_MAX_KEY = np.int32(-2 ** 31 + 0x800000)
NEG_INF = float("-inf")
POS_INF = float("inf")

VMEM_LIMIT = 56 * 1024 * 1024


def _dot(a, b):
    return jnp.dot(a, b, preferred_element_type=F32)


def _dot_nt(a, b):
    return lax.dot_general(a, b, (((1,), (1,)), ((), ())), preferred_element_type=F32)


def _rope(x, c, s1, s2, half):
    return x * c + pltpu.roll(x, half, 1) * s1 + pltpu.roll(x, LANES - half, 1) * s2


def _inproj_kernel(h_ref, g_ref, w_ref, tab_ref, ikg_ref, rep_ref,
                   hg_ref, gate_ref, q_ref, k_ref, v_ref, iq_ref, ik_ref, iw_ref):
    x = h_ref[...]
    xn = (x * lax.rsqrt(jnp.mean(x * x, axis=-1, keepdims=True) + NORM_EPS) * g_ref[...]).astype(BF16)
    hg_ref[...] = _dot(xn, w_ref[:, COL_HG:COL_ATT])
    gate_ref[...] = _dot(xn, w_ref[:, COL_GATE:IN_WIDTH_PAD])
    att = _dot(xn, w_ref[:, COL_ATT:COL_IDX])
    idx = _dot(xn, w_ref[:, COL_IDX:COL_GATE])
    ca, s1a, s2a = tab_ref[0], tab_ref[1], tab_ref[2]
    ci, s1i, s2i = tab_ref[3], tab_ref[4], tab_ref[5]
    scale = ATT_HEAD_DIM ** -0.5
    for m in range(ATT_WIDTH // LANES):
        sl = slice(m * LANES, (m + 1) * LANES)
        q_ref[:, sl] = (_rope(att[:, sl], ca, s1a, s2a, ATT_ROPE_HALF) * scale).astype(BF16)
    k_ref[...] = _rope(att[:, ATT_WIDTH:ATT_WIDTH + LANES], ca, s1a, s2a, ATT_ROPE_HALF).astype(BF16)
    v_ref[...] = att[:, ATT_WIDTH + LANES:].astype(BF16)
    for m in range(IDX_WIDTH // LANES):
        sl = slice(m * LANES, (m + 1) * LANES)
        iq_ref[:, sl] = _rope(idx[:, sl], ci, s1i, s2i, IDX_ROPE_HALF).astype(BF16)
    xk = idx[:, IDX_WIDTH:]
    lane = lax.broadcasted_iota(jnp.int32, xk.shape, 1)
    ms = jnp.sum(jnp.where(lane < IDX_HEAD_DIM, xk * xk, 0.0), axis=-1, keepdims=True) / IDX_HEAD_DIM
    ikn = xk * lax.rsqrt(ms + NORM_EPS) * ikg_ref[...]
    ikr = _rope(ikn, ci, s1i, s2i, IDX_ROPE_HALF).astype(BF16)
    ik_ref[...] = _dot(ikr, rep_ref[...]).astype(BF16)
    iw_ref[...] = xk * (IDX_HEADS ** -0.5 * IDX_HEAD_DIM ** -0.5)


def _inproj(h, g, w, tabs, ikg, rep, lp):
    n = h.shape[0]
    tm = lp // 8
    per = lp // tm
    row = lambda i: (i, 0)
    const2 = lambda i: (0, 0)
    out_shape = (
        jax.ShapeDtypeStruct((n, 4 * HGRN_WIDTH), F32),
        jax.ShapeDtypeStruct((n, 2 * D_MODEL), F32),
        jax.ShapeDtypeStruct((n, ATT_WIDTH), BF16),
        jax.ShapeDtypeStruct((n, LANES), BF16),
        jax.ShapeDtypeStruct((n, LANES), BF16),
        jax.ShapeDtypeStruct((n, IDX_WIDTH), BF16),
        jax.ShapeDtypeStruct((n, LANES), BF16),
        jax.ShapeDtypeStruct((n, LANES), F32),
    )
    return pl.pallas_call(
        _inproj_kernel,
        grid=(n // tm,),
        in_specs=[
            pl.BlockSpec((tm, D_MODEL), row),
            pl.BlockSpec((1, D_MODEL), const2),
            pl.BlockSpec((D_MODEL, IN_WIDTH_PAD), const2),
            pl.BlockSpec((6, tm, LANES), lambda i: (0, i % per, 0)),
            pl.BlockSpec((1, LANES), const2),
            pl.BlockSpec((LANES, LANES), const2),
        ],
        out_specs=tuple(pl.BlockSpec((tm, s.shape[1]), row) for s in out_shape),
        out_shape=out_shape,
        compiler_params=pltpu.CompilerParams(
            dimension_semantics=("arbitrary",), vmem_limit_bytes=VMEM_LIMIT),
        name="inproj",
    )(h, g, w, tabs, ikg, rep)


def _hgrn_kernel(q_ref, f_ref, i_ref, g_ref, lb_ref, ng_ref, o_ref, st_ref, kb_ref, bb_ref, vb_ref,
                 *, n_chunks):
    C = CHUNK
    st_ref[...] = jnp.zeros_like(st_ref)
    kb_ref[...] = jnp.zeros_like(kb_ref)
    bb_ref[...] = jnp.zeros_like(bb_ref)
    vb_ref[...] = jnp.zeros_like(vb_ref)
    lb = lb_ref[...]
    ng = ng_ref[...]
    row = lax.broadcasted_iota(jnp.int32, (C, 1), 0)
    tri = (lax.broadcasted_iota(jnp.int32, (C, C), 0) >= lax.broadcasted_iota(jnp.int32, (C, C), 1)).astype(F32)
    sub_pos = row & (SUB - 1)

    def body(c, carry):
        r0 = pl.multiple_of(c * C, C)
        rows = pl.ds(r0, C)
        z = f_ref[rows, :]
        q = q_ref[rows, :]
        v = i_ref[rows, :]
        go = g_ref[rows, :]
        f = lb + (1.0 - lb) * jax.nn.sigmoid(z)
        logf = jnp.log(f)
        k = (1.0 - lb) * jax.nn.sigmoid(-z)
        pad = jnp.logical_and(c == 0, row < FRONT)
        logf = jnp.where(pad, 0.0, logf)
        k = jnp.where(pad, 0.0, k)
        v = jnp.where(pad, 0.0, v)
        b = jnp.dot(tri, logf, preferred_element_type=F32, precision=lax.Precision.HIGHEST)

        st = st_ref[...]
        o = _dot_nt((q * jnp.exp(b)).astype(BF16), st.astype(BF16))

        kb_ref[SUB:, :] = k
        bb_ref[SUB:, :] = b
        vb_ref[SUB:, :] = v
        for j in range(SUB):
            ks = kb_ref[SUB - j:SUB - j + C, :]
            bs = bb_ref[SUB - j:SUB - j + C, :]
            vs = vb_ref[SUB - j:SUB - j + C, :]
            w = jnp.exp(jnp.minimum(b - bs, 0.0))
            a = jnp.sum(q * ks * w, axis=-1, keepdims=True)
            a = jnp.where(sub_pos >= j, a, 0.0)
            o = o + a * vs

        pieces = [jnp.zeros((SUB, HGRN_HEAD_DIM), F32)]
        for i in range(1, C // SUB):
            lo = i * SUB
            ref_b = b[lo - 1:lo, :]
            qi = (q[lo:lo + SUB, :] * jnp.exp(b[lo:lo + SUB, :] - ref_b)).astype(BF16)
            kj = (k[:lo, :] * jnp.exp(ref_b - b[:lo, :])).astype(BF16)
            a_off = _dot_nt(qi, kj)
            pieces.append(_dot(a_off.astype(BF16), v[:lo, :].astype(BF16)))
        o = o + jnp.concatenate(pieces, axis=0)

        b_last = b[C - 1:C, :]
        kd = (k * jnp.exp(b_last - b)).astype(BF16)
        st_ref[...] = st * jnp.exp(b_last) + _dot(v.T.astype(BF16), kd)

        on = o * lax.rsqrt(jnp.mean(o * o, axis=-1, keepdims=True) + NORM_EPS) * ng
        o_ref[rows, :] = (on * (go * jax.nn.sigmoid(go))).astype(o_ref.dtype)
        return carry

    lax.fori_loop(0, n_chunks, body, 0)
    tail = o_ref.shape[0] - n_chunks * C
    if tail:
        o_ref[n_chunks * C:, :] = jnp.zeros((tail, HGRN_HEAD_DIM), o_ref.dtype)


def _hgrn(hg, lb, ng, batch, lp, n_chunks):
    n = hg.shape[0]
    H = HGRN_HEADS

    def col(k):
        return pl.BlockSpec((lp, HGRN_HEAD_DIM), lambda b, h: (b, k * H + h))

    vec = pl.BlockSpec((1, HGRN_HEAD_DIM), lambda b, h: (0, h))
    return pl.pallas_call(
        functools.partial(_hgrn_kernel, n_chunks=n_chunks),
        grid=(batch, H),
        in_specs=[col(0), col(1), col(2), col(3), vec, vec],
        out_specs=pl.BlockSpec((lp, HGRN_HEAD_DIM), lambda b, h: (b, h)),
        out_shape=jax.ShapeDtypeStruct((n, HGRN_WIDTH), BF16),
        scratch_shapes=[
            pltpu.VMEM((HGRN_HEAD_DIM, HGRN_HEAD_DIM), F32),
            pltpu.VMEM((CHUNK + SUB, HGRN_HEAD_DIM), F32),
            pltpu.VMEM((CHUNK + SUB, HGRN_HEAD_DIM), F32),
            pltpu.VMEM((CHUNK + SUB, HGRN_HEAD_DIM), F32),
        ],
        compiler_params=pltpu.CompilerParams(
            dimension_semantics=("arbitrary", "arbitrary"), vmem_limit_bytes=VMEM_LIMIT),
        name="hgrn2",
    )(hg, hg, hg, hg, lb, ng)


def _sortable_to_f32(key):
    bits = jnp.where(key < 0, key ^ jnp.int32(0x7FFFFFFF), key)
    return pltpu.bitcast(bits, F32)


def _fold_rows(x, op):
    rows = x.shape[0]
    if rows > FOLD and rows % FOLD == 0:
        x = op(x.reshape(rows // FOLD, FOLD, x.shape[1]), axis=0)
    return op(x, axis=0, keepdims=True)


def _dsa_kernel(q_ref, iq_ref, iw_ref, k_ref, v_ref, ik_ref, ltri_ref, o_ref,
                sc_ref, vt_ref, iq8_ref, qg_ref, ot_ref, *, topk, n_valid, extents):
    QB = Q_BLOCK
    lp = k_ref.shape[0]
    t = pl.program_id(1)
    kf = float(topk)

    @pl.when(t == 0)
    def _():
        for kb in range(lp // QB):
            sl = slice(kb * QB, (kb + 1) * QB)
            vt_ref[:, sl] = v_ref[sl, :].astype(F32).T.astype(BF16)

    lane = lax.broadcasted_iota(jnp.int32, (QB, LANES), 1)
    iq = iq_ref[...]
    for h in range(IDX_HEADS):
        grp, slot = divmod(h, IDX_HEADS_PER_LANE_GROUP)
        part = iq[:, grp * LANES:(grp + 1) * LANES]
        iq8_ref[h * QB:(h + 1) * QB, :] = jnp.where((lane >> 5) == slot, part, jnp.zeros_like(part))
    wt = iw_ref[...].T
    lane_half = lane >> 6
    for h in range(ATT_HEADS):
        g, r = divmod(h, ATT_GROUP)
        m, p = divmod(h, LANES // ATT_HEAD_DIM)
        qh = q_ref[:, m * LANES:(m + 1) * LANES].astype(F32)
        if p != g:
            qh = pltpu.roll(qh, ATT_HEAD_DIM, 1)
        qg_ref[g, r * QB:(r + 1) * QB, :] = jnp.where(lane_half == g, qh, 0.0).astype(BF16)

    qpos = t * QB + lax.broadcasted_iota(jnp.int32, (1, QB), 1)
    qchunk = qpos >> 6
    n_adm = jnp.minimum((qchunk + 1) * CHUNK, n_valid) - FRONT
    take_all = n_adm <= topk
    ltri = ltri_ref[...]

    def tile_body(n_blocks):
        ke = n_blocks * QB
        keys = slice(0, ke)
        kpos = lax.broadcasted_iota(jnp.int32, (ke, 1), 0)
        adm = ((kpos >> 6) <= qchunk) & (kpos >= FRONT) & (kpos < n_valid)

        ik = ik_ref[keys, :]
        acc = jnp.zeros((ke, QB), F32)
        for h in range(IDX_HEADS):
            rel = jnp.maximum(_dot_nt(ik, iq8_ref[h * QB:(h + 1) * QB, :]), 0.0)
            acc = acc + rel * wt[IDX_HEAD_DIM + h:IDX_HEAD_DIM + h + 1, :]
        acc = jnp.where(acc == 0.0, 0.0, acc)
        sc_ref[keys, :] = jnp.where(adm, acc, NEG_INF)

        def count_ge(cand):
            return _fold_rows(jnp.where(sc_ref[keys, :] >= cand, 1.0, 0.0), jnp.sum)

        def counts(cand):
            x = sc_ref[keys, :]
            return (_fold_rows(jnp.where(x >= cand, 1.0, 0.0), jnp.sum),
                    _fold_rows(jnp.where(x > cand, 1.0, 0.0), jnp.sum))

        def search(i, tkey):
            cand = tkey + lax.shift_left(jnp.int32(1), 31 - i)
            return jnp.where(count_ge(_sortable_to_f32(cand)) >= kf, cand, tkey)

        tkey = lax.fori_loop(0, 32, search, jnp.full((1, QB), INT_MIN, jnp.int32))
        thr = _sortable_to_f32(jnp.where(take_all, NEG_FLT_MAX_KEY, tkey))
        c_ge, c_gt = counts(thr)

        def unsettled(c_ge, c_gt):
            live = jnp.logical_not(take_all)
            return (c_gt >= kf) & live, (c_ge < kf) & live

        def cond(carry):
            up, dn = unsettled(carry[1], carry[2])
            return jnp.max(jnp.where(up | dn, 1.0, 0.0)) > 0.0

        def fix(carry):
            thr, c_ge, c_gt = carry
            up, dn = unsettled(c_ge, c_gt)
            x = sc_ref[keys, :]
            above = _fold_rows(jnp.where(x > thr, x, POS_INF), jnp.min)
            below = _fold_rows(jnp.where(x < thr, x, NEG_INF), jnp.max)
            thr = jnp.where(up, above, jnp.where(dn, below, thr))
            return (thr,) + counts(thr)

        thr, c_ge, c_gt = lax.while_loop(cond, fix, (thr, c_ge, c_gt))

        need = kf - c_gt
        seen = jnp.zeros((1, QB), F32)
        for kb in range(n_blocks):
            blk = slice(kb * QB, (kb + 1) * QB)
            x = sc_ref[blk, :]
            eq = x == thr
            pre = _dot(ltri, jnp.where(eq, 1.0, 0.0).astype(BF16))
            take = (x > thr) | (eq & (pre + seen <= need))
            sc_ref[blk, :] = jnp.where(take, 0.0, NEG_INF)
            seen = seen + pre[QB - 1:QB, :]

        bias = sc_ref[keys, :]
        bias = jnp.concatenate([bias] * ATT_GROUP, axis=1)
        kk = k_ref[keys, :]
        for g in range(ATT_KV_HEADS):
            s = _dot_nt(kk, qg_ref[g]) + bias
            e = jnp.exp(s - _fold_rows(s, jnp.max))
            vt = vt_ref[g * ATT_HEAD_DIM:(g + 1) * ATT_HEAD_DIM, keys]
            pv = _dot(vt, e.astype(BF16)) / _fold_rows(e, jnp.sum)
            for r in range(ATT_GROUP):
                h = g * ATT_GROUP + r
                ot_ref[h * ATT_HEAD_DIM:(h + 1) * ATT_HEAD_DIM, :] = pv[:, r * QB:(r + 1) * QB]
        o_ref[...] = ot_ref[...].T.astype(o_ref.dtype)

    lo = 0
    for n_blocks in extents:
        pl.when((t >= lo) & (t < n_blocks))(functools.partial(tile_body, n_blocks))
        lo = n_blocks


def _dsa(q, k, v, iq, ik, iw, ltri, batch, lp, topk, n_valid):
    n = q.shape[0]
    nq = lp // Q_BLOCK
    extents = tuple(sorted({nq - DSA_EXTENT_STEP * i for i in range(-(-nq // DSA_EXTENT_STEP))}))
    qblk = lambda b, i: (b * nq + i, 0)
    bat = lambda b, i: (b, 0)
    return pl.pallas_call(
        functools.partial(_dsa_kernel, topk=topk, n_valid=n_valid, extents=extents),
        grid=(batch, nq),
        in_specs=[
            pl.BlockSpec((Q_BLOCK, ATT_WIDTH), qblk),
            pl.BlockSpec((Q_BLOCK, IDX_WIDTH), qblk),
            pl.BlockSpec((Q_BLOCK, LANES), qblk),
            pl.BlockSpec((lp, LANES), bat),
            pl.BlockSpec((lp, LANES), bat),
            pl.BlockSpec((lp, LANES), bat),
            pl.BlockSpec((Q_BLOCK, Q_BLOCK), lambda b, i: (0, 0)),
        ],
        out_specs=pl.BlockSpec((Q_BLOCK, ATT_WIDTH), qblk),
        out_shape=jax.ShapeDtypeStruct((n, ATT_WIDTH), BF16),
        scratch_shapes=[
            pltpu.VMEM((lp, Q_BLOCK), F32),
            pltpu.VMEM((LANES, lp), BF16),
            pltpu.VMEM((IDX_HEADS * Q_BLOCK, LANES), BF16),
            pltpu.VMEM((ATT_KV_HEADS, ATT_GROUP * Q_BLOCK, LANES), BF16),
            pltpu.VMEM((ATT_WIDTH, Q_BLOCK), F32),
        ],
        compiler_params=pltpu.CompilerParams(
            dimension_semantics=("arbitrary", "arbitrary"), vmem_limit_bytes=VMEM_LIMIT),
        name="dsa",
    )(q, iq, iw, k, v, ik, ltri)


def _merge_kernel(ya_ref, yb_ref, gate_ref, h_ref, wa_ref, wb_ref, wo_ref, fg_ref, wr_ref, br_ref,
                  ho_ref, xn_ref, cmb_ref):
    gate = gate_ref[...]
    pa = _dot(ya_ref[...], wa_ref[...])
    pb = _dot(yb_ref[...], wb_ref[...])
    merged = jax.nn.sigmoid(gate[:, :D_MODEL]) * pa + jax.nn.sigmoid(gate[:, D_MODEL:]) * pb
    h = h_ref[...] + _dot(merged.astype(BF16), wo_ref[...])
    ho_ref[...] = h
    xn = h * lax.rsqrt(jnp.mean(h * h, axis=-1, keepdims=True) + NORM_EPS) * fg_ref[...]
    xn_ref[...] = xn.astype(BF16)

    lg = jnp.dot(xn, wr_ref[...], preferred_element_type=F32, precision=lax.Precision.HIGHEST) + br_ref[...]
    lane = lax.broadcasted_iota(jnp.int32, lg.shape, 1)
    lane_f = lane.astype(F32)
    big = float(LANES)
    is_g = (lane >= N_EXPERTS) & (lane < N_EXPERTS + N_GROUPS)
    gl = jnp.where(is_g, lg, NEG_INF)
    gmax = jnp.max(gl, axis=-1, keepdims=True)
    pg_top = 1.0 / jnp.sum(jnp.exp(gl - gmax), axis=-1, keepdims=True)
    g_lane = jnp.min(jnp.where(gl == gmax, lane_f, big), axis=-1, keepdims=True)
    e_lo = (g_lane - N_EXPERTS) * EXPERTS_PER_GROUP
    in_grp = (lane_f >= e_lo) & (lane_f < e_lo + EXPERTS_PER_GROUP)
    el = jnp.where(in_grp, lg, NEG_INF)
    ee = jnp.exp(el - jnp.max(el, axis=-1, keepdims=True))
    pe = ee / jnp.sum(ee, axis=-1, keepdims=True)
    pe = jnp.where(in_grp, pe, -1.0)
    p1 = jnp.max(pe, axis=-1, keepdims=True)
    i1 = jnp.min(jnp.where(pe == p1, lane_f, big), axis=-1, keepdims=True)
    pe2 = jnp.where(lane_f == i1, -1.0, pe)
    p2 = jnp.max(pe2, axis=-1, keepdims=True)
    i2 = jnp.min(jnp.where(pe2 == p2, lane_f, big), axis=-1, keepdims=True)
    tot = p1 + p2
    cmb_ref[...] = jnp.where(lane_f == i1, p1 / tot * pg_top,
                             jnp.where(lane_f == i2, p2 / tot * pg_top, 0.0))


def _merge(ya, yb, gates, h, wa, wb, wo, fg, wr, br):
    n = h.shape[0]
    tm = min(256, n)
    row = lambda i: (i, 0)
    const = lambda i: (0, 0)
    return pl.pallas_call(
        _merge_kernel,
        grid=(n // tm,),
        in_specs=[
            pl.BlockSpec((tm, HGRN_WIDTH), row),
            pl.BlockSpec((tm, ATT_WIDTH), row),
            pl.BlockSpec((tm, 2 * D_MODEL), row),
            pl.BlockSpec((tm, D_MODEL), row),
            pl.BlockSpec((HGRN_WIDTH, D_MODEL), const),
            pl.BlockSpec((ATT_WIDTH, D_MODEL), const),
            pl.BlockSpec((D_MODEL, D_MODEL), const),
            pl.BlockSpec((1, D_MODEL), const),
            pl.BlockSpec((D_MODEL, LANES), const),
            pl.BlockSpec((1, LANES), const),
        ],
        out_specs=(pl.BlockSpec((tm, D_MODEL), row), pl.BlockSpec((tm, D_MODEL), row),
                   pl.BlockSpec((tm, LANES), row)),
        out_shape=(jax.ShapeDtypeStruct((n, D_MODEL), F32), jax.ShapeDtypeStruct((n, D_MODEL), BF16),
                   jax.ShapeDtypeStruct((n, LANES), F32)),
        compiler_params=pltpu.CompilerParams(
            dimension_semantics=("arbitrary",), vmem_limit_bytes=VMEM_LIMIT),
        name="merge_router",
    )(ya, yb, gates, h, wa, wb, wo, fg, wr, br)


def _moe_kernel(x_ref, c_ref, h_ref, wg_ref, wu_ref, wd_ref, o_ref, acc_ref):
    e = pl.program_id(1)

    @pl.when(e == 0)
    def _():
        acc_ref[...] = h_ref[...]

    x = x_ref[...]
    a = _dot(x, wg_ref[0])
    u = _dot(x, wu_ref[0])
    c = c_ref[...]
    lane = lax.broadcasted_iota(jnp.int32, c.shape, 1)
    ce = jnp.sum(jnp.where(lane == e, c, 0.0), axis=-1, keepdims=True)
    act = (a * jax.nn.sigmoid(a)) * u * ce
    acc_ref[...] += _dot(act.astype(BF16), wd_ref[0])

    @pl.when(e == pl.num_programs(1) - 1)
    def _():
        o_ref[...] = acc_ref[...]


def _moe(xn, cmb, h, wg, wu, wd):
    n = h.shape[0]
    tm = min(1024, n)
    row = lambda i, e: (i, 0)
    return pl.pallas_call(
        _moe_kernel,
        grid=(n // tm, N_EXPERTS),
        in_specs=[
            pl.BlockSpec((tm, D_MODEL), row),
            pl.BlockSpec((tm, LANES), row),
            pl.BlockSpec((tm, D_MODEL), row),
            pl.BlockSpec((1, D_MODEL, EXPERT_FF), lambda i, e: (e, 0, 0)),
            pl.BlockSpec((1, D_MODEL, EXPERT_FF), lambda i, e: (e, 0, 0)),
            pl.BlockSpec((1, EXPERT_FF, D_MODEL), lambda i, e: (e, 0, 0)),
        ],
        out_specs=pl.BlockSpec((tm, D_MODEL), row),
        out_shape=jax.ShapeDtypeStruct((n, D_MODEL), F32),
        scratch_shapes=[pltpu.VMEM((tm, D_MODEL), F32)],
        compiler_params=pltpu.CompilerParams(
            dimension_semantics=("arbitrary", "arbitrary"), vmem_limit_bytes=VMEM_LIMIT),
        name="moe_experts",
    )(xn, cmb, h, wg, wu, wd)


def _final_kernel(h_ref, g_ref, o_ref, *, seq):
    step = 256 if seq % 256 == 0 else CHUNK
    for r in range(0, seq, step):
        x = h_ref[CHUNK + r:CHUNK + r + step, :]
        o_ref[0, r:r + step, :] = x * lax.rsqrt(jnp.mean(x * x, axis=-1, keepdims=True) + NORM_EPS) * g_ref[...]


def _final_norm(h, g, batch, lp, seq):
    return pl.pallas_call(
        functools.partial(_final_kernel, seq=seq),
        grid=(batch,),
        in_specs=[pl.BlockSpec((lp, D_MODEL), lambda b: (b, 0)), pl.BlockSpec((1, D_MODEL), lambda b: (0, 0))],
        out_specs=pl.BlockSpec((1, seq, D_MODEL), lambda b: (b, 0, 0)),
        out_shape=jax.ShapeDtypeStruct((batch, seq, D_MODEL), F32),
        compiler_params=pltpu.CompilerParams(
            dimension_semantics=("arbitrary",), vmem_limit_bytes=VMEM_LIMIT),
        name="final_norm",
    )(h, g)


def _rope_lane_tables(pos, head_dim):
    rot = head_dim // 4
    half = rot // 2
    inv = ROPE_THETA ** (-jnp.arange(0, rot, 2, dtype=F32) / rot)
    ang = pos[:, None] * inv[None, :]
    cos, sin = jnp.cos(ang), jnp.sin(ang)
    jj = np.arange(LANES) % head_dim
    first = jnp.asarray(jj < half)
    second = jnp.asarray((jj >= half) & (jj < rot))
    fidx = jnp.asarray(np.where(jj < half, jj, np.where(jj < rot, jj - half, 0)))
    cl, sl = cos[:, fidx], sin[:, fidx]
    c = jnp.where(first | second, cl, 1.0)
    s1 = jnp.where(second, sl, 0.0)
    s2 = jnp.where(first, -sl, 0.0)
    return c, s1, s2


def kernel(x, meta_tokens, mix_norm_g, w_in, hgrn_lb_logits, hgrn_norm_g, idx_k_norm_g, w_branch_hgrn,
           w_branch_dsa, w_out, ffn_norm_g, router_group_w, router_group_b, router_expert_w,
           router_expert_b, w_expert_gate, w_expert_up, w_expert_down, final_norm_g):
    batch, seq, _ = x.shape
    depth = w_in.shape[0]
    n_valid = CHUNK + seq
    assert seq % CHUNK == 0
    lp = -(-n_valid // Q_BLOCK) * Q_BLOCK
    n_chunks = n_valid // CHUNK
    topk = min(TOPK_MAX, (N_META + seq) // 4)

    meta = jnp.broadcast_to(meta_tokens[None].astype(x.dtype), (batch, N_META, D_MODEL))
    h = jnp.concatenate([jnp.zeros((batch, FRONT, D_MODEL), x.dtype), meta, x,
                         jnp.zeros((batch, lp - n_valid, D_MODEL), x.dtype)], axis=1)
    h = h.reshape(batch * lp, D_MODEL)

    pos = jnp.clip(jnp.arange(lp, dtype=jnp.int32) - FRONT, 0, N_META + seq - 1).astype(F32)
    tabs = jnp.stack(_rope_lane_tables(pos, ATT_HEAD_DIM) + _rope_lane_tables(pos, IDX_HEAD_DIM))

    p = jax.nn.softmax(hgrn_lb_logits.astype(F32), axis=0)
    cs = jnp.cumsum(p, axis=0)
    lower_bounds = cs - cs[0:1]

    rep = np.zeros((LANES, LANES), np.float32)
    for slot in range(IDX_HEADS_PER_LANE_GROUP):
        rep[np.arange(IDX_HEAD_DIM), slot * IDX_HEAD_DIM + np.arange(IDX_HEAD_DIM)] = 1.0
    rep = jnp.asarray(rep, BF16)
    ltri = jnp.asarray(np.tril(np.ones((Q_BLOCK, Q_BLOCK), np.float32)), BF16)

    pad_cols = jnp.zeros((D_MODEL, IDX_PAD - IDX_RAW), F32)
    for layer in range(depth):
        w = w_in[layer]
        w_p = jnp.concatenate([w[:, :COL_IDX + IDX_RAW], pad_cols, w[:, COL_IDX + IDX_RAW:]], axis=1).astype(BF16)
        ikg = jnp.pad(idx_k_norm_g[layer].astype(F32), (0, LANES - IDX_HEAD_DIM))[None]
        hg, gates, q, k, v, iq, ik, iw = _inproj(h, mix_norm_g[layer][None].astype(F32), w_p, tabs, ikg, rep, lp)
        ya = _hgrn(hg, lower_bounds[layer][None], hgrn_norm_g[layer][None].astype(F32), batch, lp, n_chunks)
        yb = _dsa(q, k, v, iq, ik, iw, ltri, batch, lp, topk, n_valid)
        wr = jnp.concatenate([router_expert_w[layer], router_group_w[layer],
                              jnp.zeros((D_MODEL, LANES - N_EXPERTS - N_GROUPS), F32)], axis=1)
        br = jnp.concatenate([router_expert_b[layer], router_group_b[layer],
                              jnp.zeros((LANES - N_EXPERTS - N_GROUPS,), F32)])[None]
        h, xn, cmb = _merge(ya, yb, gates, h, w_branch_hgrn[layer].astype(BF16), w_branch_dsa[layer].astype(BF16),
                            w_out[layer].astype(BF16), ffn_norm_g[layer][None].astype(F32), wr, br)
        h = _moe(xn, cmb, h, w_expert_gate[layer].astype(BF16), w_expert_up[layer].astype(BF16),
                 w_expert_down[layer].astype(BF16))
    return _final_norm(h, final_norm_g[None].astype(F32), batch, lp, seq)
```

```python
import functools

import jax
import jax.numpy as jnp
import numpy as np
from jax import lax
from jax.experimental import pallas as pl
from jax.experimental.pallas import tpu as pltpu

F32 = jnp.float32
BF16 = jnp.bfloat16

D_MODEL = 1024
CHUNK = 64
N_META = 16
FRONT = CHUNK - N_META
ROPE_THETA = 500000.0
NORM_EPS = 1e-6
LANES = 128
SUBLANES = 8

HGRN_HEADS = 4
HGRN_HEAD_DIM = 128
HGRN_WIDTH = HGRN_HEADS * HGRN_HEAD_DIM
SUB = 8
HGRN_MAX_GROUP = 11

ATT_HEADS = 8
ATT_KV_HEADS = 2
ATT_HEAD_DIM = 64
ATT_GROUP = ATT_HEADS // ATT_KV_HEADS
ATT_WIDTH = ATT_HEADS * ATT_HEAD_DIM
ATT_KV_WIDTH = ATT_KV_HEADS * ATT_HEAD_DIM
ATT_ROPE_HALF = ATT_HEAD_DIM // 4 // 2
IDX_HEADS = 8
IDX_HEAD_DIM = 32
IDX_WIDTH = IDX_HEADS * IDX_HEAD_DIM
IDX_ROPE_HALF = IDX_HEAD_DIM // 4 // 2
IDX_HEADS_PER_LANE_GROUP = LANES // IDX_HEAD_DIM
TOPK_MAX = 256
Q_BLOCK = 128
DSA_EXTENT_STEP = 4
FOLD = 64

N_GROUPS = 4
EXPERTS_PER_GROUP = 8
N_EXPERTS = N_GROUPS * EXPERTS_PER_GROUP
EXPERT_FF = D_MODEL // 4

COL_HG = 0
COL_ATT = 4 * HGRN_WIDTH
COL_IDX = COL_ATT + ATT_WIDTH + 2 * ATT_KV_WIDTH
IDX_RAW = IDX_WIDTH + IDX_HEAD_DIM + IDX_HEADS
IDX_PAD = IDX_WIDTH + LANES
COL_GATE = COL_IDX + IDX_PAD
IN_WIDTH_PAD = COL_GATE + 2 * D_MODEL

INT_MIN = np.int32(-2 ** 31)
NEG_FLT_MAX_KEY = np.int32(-2 ** 31 + 0x800000)
NEG_INF = float("-inf")
POS_INF = float("inf")

VMEM_LIMIT = 56 * 1024 * 1024


def _dot(a, b):
    return jnp.dot(a, b, preferred_element_type=F32)


def _dot_nt(a, b):
    return lax.dot_general(a, b, (((1,), (1,)), ((), ())), preferred_element_type=F32)


def _rope(x, c, s1, s2, half):
    return x * c + pltpu.roll(x, half, 1) * s1 + pltpu.roll(x, LANES - half, 1) * s2


def _inproj_kernel(h_ref, g_ref, w_ref, tab_ref, ikg_ref, rep_ref,
                   hg_ref, gate_ref, q_ref, k_ref, v_ref, iq_ref, ik_ref, iw_ref):
    x = h_ref[...]
    xn = (x * lax.rsqrt(jnp.mean(x * x, axis=-1, keepdims=True) + NORM_EPS) * g_ref[...]).astype(BF16)
    hg_ref[...] = _dot(xn, w_ref[:, COL_HG:COL_ATT])
    gate_ref[...] = _dot(xn, w_ref[:, COL_GATE:IN_WIDTH_PAD])
    att = _dot(xn, w_ref[:, COL_ATT:COL_IDX])
    idx = _dot(xn, w_ref[:, COL_IDX:COL_GATE])
    ca, s1a, s2a = tab_ref[0], tab_ref[1], tab_ref[2]
    ci, s1i, s2i = tab_ref[3], tab_ref[4], tab_ref[5]
    scale = ATT_HEAD_DIM ** -0.5
    for m in range(ATT_WIDTH // LANES):
        sl = slice(m * LANES, (m + 1) * LANES)
        q_ref[:, sl] = (_rope(att[:, sl], ca, s1a, s2a, ATT_ROPE_HALF) * scale).astype(BF16)
    k_ref[...] = _rope(att[:, ATT_WIDTH:ATT_WIDTH + LANES], ca, s1a, s2a, ATT_ROPE_HALF).astype(BF16)
    v_ref[...] = att[:, ATT_WIDTH + LANES:].astype(BF16)
    for m in range(IDX_WIDTH // LANES):
        sl = slice(m * LANES, (m + 1) * LANES)
        iq_ref[:, sl] = _rope(idx[:, sl], ci, s1i, s2i, IDX_ROPE_HALF).astype(BF16)
    xk = idx[:, IDX_WIDTH:]
    lane = lax.broadcasted_iota(jnp.int32, xk.shape, 1)
    ms = jnp.sum(jnp.where(lane < IDX_HEAD_DIM, xk * xk, 0.0), axis=-1, keepdims=True) / IDX_HEAD_DIM
    ikn = xk * lax.rsqrt(ms + NORM_EPS) * ikg_ref[...]
    ikr = _rope(ikn, ci, s1i, s2i, IDX_ROPE_HALF).astype(BF16)
    ik_ref[...] = _dot(ikr, rep_ref[...]).astype(BF16)
    iw_ref[...] = xk * (IDX_HEADS ** -0.5 * IDX_HEAD_DIM ** -0.5)


def _inproj(h, g, w, tabs, ikg, rep, lp):
    n = h.shape[0]
    tm = lp // 8
    per = lp // tm
    row = lambda i: (i, 0)
    const2 = lambda i: (0, 0)
    out_shape = (
        jax.ShapeDtypeStruct((n, 4 * HGRN_WIDTH), F32),
        jax.ShapeDtypeStruct((n, 2 * D_MODEL), F32),
        jax.ShapeDtypeStruct((n, ATT_WIDTH), BF16),
        jax.ShapeDtypeStruct((n, LANES), BF16),
        jax.ShapeDtypeStruct((n, LANES), BF16),
        jax.ShapeDtypeStruct((n, IDX_WIDTH), BF16),
        jax.ShapeDtypeStruct((n, LANES), BF16),
        jax.ShapeDtypeStruct((n, LANES), F32),
    )
    return pl.pallas_call(
        _inproj_kernel,
        grid=(n // tm,),
        in_specs=[
            pl.BlockSpec((tm, D_MODEL), row),
            pl.BlockSpec((1, D_MODEL), const2),
            pl.BlockSpec((D_MODEL, IN_WIDTH_PAD), const2),
            pl.BlockSpec((6, tm, LANES), lambda i: (0, i % per, 0)),
            pl.BlockSpec((1, LANES), const2),
            pl.BlockSpec((LANES, LANES), const2),
        ],
        out_specs=tuple(pl.BlockSpec((tm, s.shape[1]), row) for s in out_shape),
        out_shape=out_shape,
        compiler_params=pltpu.CompilerParams(
            dimension_semantics=("arbitrary",), vmem_limit_bytes=VMEM_LIMIT),
        name="inproj",
    )(h, g, w, tabs, ikg, rep)


def _hgrn_kernel(q_ref, f_ref, i_ref, g_ref, lb_ref, ng_ref, o_ref, st_ref, kb_ref, bb_ref, vb_ref,
                 *, n_chunks, group):
    C = CHUNK
    D = HGRN_HEAD_DIM
    st_ref[...] = jnp.zeros_like(st_ref)
    kb_ref[...] = jnp.zeros_like(kb_ref)
    bb_ref[...] = jnp.zeros_like(bb_ref)
    vb_ref[...] = jnp.zeros_like(vb_ref)
    lb = lb_ref[...]
    ng = ng_ref[...]
    R = group * C
    tri_row = lax.broadcasted_iota(jnp.int32, (C, C), 0)
    tri_col = lax.broadcasted_iota(jnp.int32, (C, C), 1)
    tri = (tri_row >= tri_col).astype(F32)
    local_row = lax.broadcasted_iota(jnp.int32, (R, 1), 0)
    sub_pos = local_row & (SUB - 1)

    def chunk(x, u):
        return x[u * C:(u + 1) * C, :]

    def body(it, carry):
        rows = pl.ds(pl.multiple_of(it * R, C), R)
        z = f_ref[rows, :]
        q = q_ref[rows, :]
        v = i_ref[rows, :]
        go = g_ref[rows, :]
        pad = (it * R + local_row) < FRONT
        f = lb + (1.0 - lb) * jax.nn.sigmoid(z)
        logf = jnp.where(pad, 0.0, jnp.log(f))
        k = jnp.where(pad, 0.0, (1.0 - lb) * jax.nn.sigmoid(-z))
        v = jnp.where(pad, 0.0, v)
        b = jnp.concatenate(
            [jnp.dot(tri, chunk(logf, u), preferred_element_type=F32, precision=lax.Precision.HIGHEST)
             for u in range(group)], axis=0)
        b3 = b.reshape(group, C, D)
        b_last = b3[:, C - 1:C, :]
        qd = (q * jnp.exp(b)).astype(BF16)
        kd = (k.reshape(group, C, D) * jnp.exp(b_last - b3)).reshape(R, D).astype(BF16)
        e_last = jnp.exp(b_last)
        upd =[_dot(chunk(v, u).T.astype(BF16), chunk(kd, u)) for u in range(group)]

        st = st_ref[...]
        before = []
        for u in range(group):
            before.append(st.astype(BF16))
            st = st * e_last[u] + upd[u]
        st_ref[...] = st
        o = jnp.concatenate([_dot_nt(chunk(qd, u), before[u]) for u in range(group)], axis=0)

        kb_ref[SUB:, :] = k
        bb_ref[SUB:, :] = b
        vb_ref[SUB:, :] = v
        for j in range(SUB):
            ks = kb_ref[SUB - j:SUB - j + R, :]
            bs = bb_ref[SUB - j:SUB - j + R, :]
            vs = vb_ref[SUB - j:SUB - j + R, :]
            w = jnp.exp(jnp.minimum(b - bs, 0.0))
            a = jnp.sum(q * ks * w, axis=-1, keepdims=True)
            o = o + jnp.where(sub_pos >= j, a, 0.0) * vs

        att = [jnp.zeros((C, C), F32) for _ in range(group)]
        span = C
        while span > SUB:
            half = span // 2
            upper = (local_row & (span - 1)) >= half
            bs3 = b.reshape(R // span, span, D)
            mid = bs3[:, half - 1:half, :]
            qs = jnp.where(upper, (q.reshape(bs3.shape) * jnp.exp(jnp.minimum(bs3 - mid, 0.0))).reshape(R, D), 0.0)
            ks = jnp.where(upper, 0.0, (k.reshape(bs3.shape) * jnp.exp(jnp.minimum(mid - bs3, 0.0))).reshape(R, D))
            qs, ks = qs.astype(BF16), ks.astype(BF16)
            same_block = (tri_row & -span) == (tri_col & -span)
            att = [att[u] + jnp.where(same_block, _dot_nt(chunk(qs, u), chunk(ks, u)), 0.0) for u in range(group)]
            span = half
        o = o + jnp.concatenate([_dot(att[u].astype(BF16), chunk(v, u).astype(BF16)) for u in range(group)], axis=0)

        on = o * lax.rsqrt(jnp.mean(o * o, axis=-1, keepdims=True) + NORM_EPS) * ng
        o_ref[rows, :] = (on * (go * jax.nn.sigmoid(go))).astype(o_ref.dtype)
        return carry

    lax.fori_loop(0, n_chunks // group, body, 0)
    tail = o_ref.shape[0] - n_chunks * C
    if tail:
        o_ref[n_chunks * C:, :] = jnp.zeros((tail, D), o_ref.dtype)


def _hgrn(hg, lb, ng, batch, lp, n_chunks):
    n = hg.shape[0]
    H, D = HGRN_HEADS, HGRN_HEAD_DIM
    group = max(g for g in range(1, HGRN_MAX_GROUP + 1) if n_chunks % g == 0)

    def col(k):
        return pl.BlockSpec((lp, D), lambda b, h: (b, k * H + h))

    vec = pl.BlockSpec((1, D), lambda b, h: (0, h))
    return pl.pallas_call(
        functools.partial(_hgrn_kernel, n_chunks=n_chunks, group=group),
        grid=(batch, H),
        in_specs=[col(0), col(1), col(2), col(3), vec, vec],
        out_specs=pl.BlockSpec((lp, D), lambda b, h: (b, h)),
        out_shape=jax.ShapeDtypeStruct((n, HGRN_WIDTH), BF16),
        scratch_shapes=[
            pltpu.VMEM((D, D), F32),
            pltpu.VMEM((group * CHUNK + SUB, D), F32),
            pltpu.VMEM((group * CHUNK + SUB, D), F32),
            pltpu.VMEM((group * CHUNK + SUB, D), F32),
        ],
        compiler_params=pltpu.CompilerParams(
            dimension_semantics=("arbitrary", "arbitrary"), vmem_limit_bytes=VMEM_LIMIT),
        name="hgrn2",
    )(hg, hg, hg, hg, lb, ng)


def _sortable_to_f32(key):
    bits = jnp.where(key < 0, key ^ jnp.int32(0x7FFFFFFF), key)
    return pltpu.bitcast(bits, F32)


def _fold_rows(x, op):
    rows = x.shape[0]
    if rows > FOLD and rows % FOLD == 0:
        x = op(x.reshape(rows // FOLD, FOLD, x.shape[1]), axis=0)
    return op(x, axis=0, keepdims=True)


def _dsa_kernel(q_ref, iq_ref, iw_ref, k_ref, v_ref, ik_ref, ltri_ref, o_ref,
                sc_ref, vt_ref, iq8_ref, qg_ref, ot_ref, *, topk, n_valid, extents):
    QB = Q_BLOCK
    lp = k_ref.shape[0]
    t = pl.program_id(1)
    kf = float(topk)

    @pl.when(t == 0)
    def _():
        for kb in range(lp // QB):
            sl = slice(kb * QB, (kb + 1) * QB)
            vt_ref[:, sl] = v_ref[sl, :].astype(F32).T.astype(BF16)

    lane = lax.broadcasted_iota(jnp.int32, (QB, LANES), 1)
    iq = iq_ref[...]
    for h in range(IDX_HEADS):
        grp, slot = divmod(h, IDX_HEADS_PER_LANE_GROUP)
        part = iq[:, grp * LANES:(grp + 1) * LANES]
        iq8_ref[h * QB:(h + 1) * QB, :] = jnp.where((lane >> 5) == slot, part, jnp.zeros_like(part))
    wt = iw_ref[...].T
    lane_half = lane >> 6
    for h in range(ATT_HEADS):
        g, r = divmod(h, ATT_GROUP)
        m, p = divmod(h, LANES // ATT_HEAD_DIM)
        qh = q_ref[:, m * LANES:(m + 1) * LANES].astype(F32)
        if p != g:
            qh = pltpu.roll(qh, ATT_HEAD_DIM, 1)
        qg_ref[g, r * QB:(r + 1) * QB, :] = jnp.where(lane_half == g, qh, 0.0).astype(BF16)

    qpos = t * QB + lax.broadcasted_iota(jnp.int32, (1, QB), 1)
    qchunk = qpos >> 6
    n_adm = jnp.minimum((qchunk + 1) * CHUNK, n_valid) - FRONT
    take_all = n_adm <= topk
    ltri = ltri_ref[...]

    def tile_body(n_blocks):
        ke = n_blocks * QB
        keys = slice(0, ke)
        kpos = lax.broadcasted_iota(jnp.int32, (ke, 1), 0)
        adm = ((kpos >> 6) <= qchunk) & (kpos >= FRONT) & (kpos < n_valid)

        ik = ik_ref[keys, :]
        acc = jnp.zeros((ke, QB), F32)
        for h in range(IDX_HEADS):
            rel = jnp.maximum(_dot_nt(ik, iq8_ref[h * QB:(h + 1) * QB, :]), 0.0)
            acc = acc + rel * wt[IDX_HEAD_DIM + h:IDX_HEAD_DIM + h + 1, :]
        acc = jnp.where(acc == 0.0, 0.0, acc)
        sc_ref[keys, :] = jnp.where(adm, acc, NEG_INF)

        def count_ge(cand):
            return _fold_rows(jnp.where(sc_ref[keys, :] >= cand, 1.0, 0.0), jnp.sum)

        def counts(cand):
            x = sc_ref[keys, :]
            return (_fold_rows(jnp.where(x >= cand, 1.0, 0.0), jnp.sum),
                    _fold_rows(jnp.where(x > cand, 1.0, 0.0), jnp.sum))

        def search(i, tkey):
            cand = tkey + lax.shift_left(jnp.int32(1), 31 - i)
            return jnp.where(count_ge(_sortable_to_f32(cand)) >= kf, cand, tkey)

        tkey = lax.fori_loop(0, 32, search, jnp.full((1, QB), INT_MIN, jnp.int32))
        thr = _sortable_to_f32(jnp.where(take_all, NEG_FLT_MAX_KEY, tkey))
        c_ge, c_gt = counts(thr)

        def unsettled(c_ge, c_gt):
            live = jnp.logical_not(take_all)
            return (c_gt >= kf) & live, (c_ge < kf) & live

        def cond(carry):
            up, dn = unsettled(carry[1], carry[2])
            return jnp.max(jnp.where(up | dn, 1.0, 0.0)) > 0.0

        def fix(carry):
            thr, c_ge, c_gt = carry
            up, dn = unsettled(c_ge, c_gt)
            x = sc_ref[keys, :]
            above = _fold_rows(jnp.where(x > thr, x, POS_INF), jnp.min)
            below = _fold_rows(jnp.where(x < thr, x, NEG_INF), jnp.max)
            thr = jnp.where(up, above, jnp.where(dn, below, thr))
            return (thr,) + counts(thr)

        thr, c_ge, c_gt = lax.while_loop(cond, fix, (thr, c_ge, c_gt))

        need = kf - c_gt
        seen = jnp.zeros((1, QB), F32)
        for kb in range(n_blocks):
            blk = slice(kb * QB, (kb + 1) * QB)
            x = sc_ref[blk, :]
            eq = x == thr
            pre = _dot(ltri, jnp.where(eq, 1.0, 0.0).astype(BF16))
            take = (x > thr) | (eq & (pre + seen <= need))
            sc_ref[blk, :] = jnp.where(take, 0.0, NEG_INF)
            seen = seen + pre[QB - 1:QB, :]

        bias = sc_ref[keys, :]
        bias = jnp.concatenate([bias] * ATT_GROUP, axis=1)
        kk = k_ref[keys, :]
        for g in range(ATT_KV_HEADS):
            s = _dot_nt(kk, qg_ref[g]) + bias
            e = jnp.exp(s - _fold_rows(s, jnp.max))
            vt = vt_ref[g * ATT_HEAD_DIM:(g + 1) * ATT_HEAD_DIM, keys]
            pv = _dot(vt, e.astype(BF16)) / _fold_rows(e, jnp.sum)
            for r in range(ATT_GROUP):
                h = g * ATT_GROUP + r
                ot_ref[h * ATT_HEAD_DIM:(h + 1) * ATT_HEAD_DIM, :] = pv[:, r * QB:(r + 1) * QB]
        o_ref[...] = ot_ref[...].T.astype(o_ref.dtype)

    lo = 0
    for n_blocks in extents:
        pl.when((t >= lo) & (t < n_blocks))(functools.partial(tile_body, n_blocks))
        lo = n_blocks


def _dsa(q, k, v, iq, ik, iw, ltri, batch, lp, topk, n_valid):
    n = q.shape[0]
    nq = lp // Q_BLOCK
    extents = tuple(sorted({nq - DSA_EXTENT_STEP * i for i in range(-(-nq // DSA_EXTENT_STEP))}))
    qblk = lambda b, i: (b * nq + i, 0)
    bat = lambda b, i: (b, 0)
    return pl.pallas_call(
        functools.partial(_dsa_kernel, topk=topk, n_valid=n_valid, extents=extents),
        grid=(batch, nq),
        in_specs=[
            pl.BlockSpec((Q_BLOCK, ATT_WIDTH), qblk),
            pl.BlockSpec((Q_BLOCK, IDX_WIDTH), qblk),
            pl.BlockSpec((Q_BLOCK, LANES), qblk),
            pl.BlockSpec((lp, LANES), bat),
            pl.BlockSpec((lp, LANES), bat),
            pl.BlockSpec((lp, LANES), bat),
            pl.BlockSpec((Q_BLOCK, Q_BLOCK), lambda b, i: (0, 0)),
        ],
        out_specs=pl.BlockSpec((Q_BLOCK, ATT_WIDTH), qblk),
        out_shape=jax.ShapeDtypeStruct((n, ATT_WIDTH), BF16),
        scratch_shapes=[
            pltpu.VMEM((lp, Q_BLOCK), F32),
            pltpu.VMEM((LANES, lp), BF16),
            pltpu.VMEM((IDX_HEADS * Q_BLOCK, LANES), BF16),
            pltpu.VMEM((ATT_KV_HEADS, ATT_GROUP * Q_BLOCK, LANES), BF16),
            pltpu.VMEM((ATT_WIDTH, Q_BLOCK), F32),
        ],
        compiler_params=pltpu.CompilerParams(
            dimension_semantics=("arbitrary", "arbitrary"), vmem_limit_bytes=VMEM_LIMIT),
        name="dsa",
    )(q, iq, iw, k, v, ik, ltri)


def _merge_kernel(ya_ref, yb_ref, gate_ref, h_ref, wa_ref, wb_ref, wo_ref, fg_ref, wr_ref, br_ref,
                  ho_ref, xn_ref, cmb_ref):
    gate = gate_ref[...]
    pa = _dot(ya_ref[...], wa_ref[...])
    pb = _dot(yb_ref[...], wb_ref[...])
    merged = jax.nn.sigmoid(gate[:, :D_MODEL]) * pa + jax.nn.sigmoid(gate[:, D_MODEL:]) * pb
    h = h_ref[...] + _dot(merged.astype(BF16), wo_ref[...])
    ho_ref[...] = h
    xn = h * lax.rsqrt(jnp.mean(h * h, axis=-1, keepdims=True) + NORM_EPS) * fg_ref[...]
    xh = xn.astype(BF16)
    xn_ref[...] = xh

    xl = (xn - xh.astype(F32)).astype(BF16)
    r = _dot(xh, wr_ref[...])
    lg = r[:, :LANES] + r[:, LANES:] + _dot(xl, wr_ref[:, :LANES]) + br_ref[...]
    lane = lax.broadcasted_iota(jnp.int32, lg.shape, 1)
    lane_f = lane.astype(F32)
    big = float(LANES)
    is_g = (lane >= N_EXPERTS) & (lane < N_EXPERTS + N_GROUPS)
    gl = jnp.where(is_g, lg, NEG_INF)
    gmax = jnp.max(gl, axis=-1, keepdims=True)
    pg_top = 1.0 / jnp.sum(jnp.exp(gl - gmax), axis=-1, keepdims=True)
    g_lane = jnp.min(jnp.where(gl == gmax, lane_f, big), axis=-1, keepdims=True)
    e_lo = (g_lane - N_EXPERTS) * EXPERTS_PER_GROUP
    in_grp = (lane_f >= e_lo) & (lane_f < e_lo + EXPERTS_PER_GROUP)
    el = jnp.where(in_grp, lg, NEG_INF)
    ee = jnp.exp(el - jnp.max(el, axis=-1, keepdims=True))
    pe = ee / jnp.sum(ee, axis=-1, keepdims=True)
    pe = jnp.where(in_grp, pe, -1.0)
    p1 = jnp.max(pe, axis=-1, keepdims=True)
    i1 = jnp.min(jnp.where(pe == p1, lane_f, big), axis=-1, keepdims=True)
    pe2 = jnp.where(lane_f == i1, -1.0, pe)
    p2 = jnp.max(pe2, axis=-1, keepdims=True)
    i2 = jnp.min(jnp.where(pe2 == p2, lane_f, big), axis=-1, keepdims=True)
    tot = p1 + p2
    cmb_ref[...] = jnp.where(lane_f == i1, p1 / tot * pg_top,
                             jnp.where(lane_f == i2, p2 / tot * pg_top, 0.0))


def _merge(ya, yb, gates, h, wa, wb, wo, fg, wr, br):
    n = h.shape[0]
    tm = min(512, n)
    row = lambda i: (i, 0)
    const = lambda i: (0, 0)
    return pl.pallas_call(
        _merge_kernel,
        grid=(n // tm,),
        in_specs=[
            pl.BlockSpec((tm, HGRN_WIDTH), row),
            pl.BlockSpec((tm, ATT_WIDTH), row),
            pl.BlockSpec((tm, 2 * D_MODEL), row),
            pl.BlockSpec((tm, D_MODEL), row),
            pl.BlockSpec((HGRN_WIDTH, D_MODEL), const),
            pl.BlockSpec((ATT_WIDTH, D_MODEL), const),
            pl.BlockSpec((D_MODEL, D_MODEL), const),
            pl.BlockSpec((1, D_MODEL), const),
            pl.BlockSpec((D_MODEL, 2 * LANES), const),
            pl.BlockSpec((1, LANES), const),
        ],
        out_specs=(pl.BlockSpec((tm, D_MODEL), row), pl.BlockSpec((tm, D_MODEL), row),
                   pl.BlockSpec((tm, LANES), row)),
        out_shape=(jax.ShapeDtypeStruct((n, D_MODEL), F32), jax.ShapeDtypeStruct((n, D_MODEL), BF16),
                   jax.ShapeDtypeStruct((n, LANES), F32)),
        compiler_params=pltpu.CompilerParams(
            dimension_semantics=("arbitrary",), vmem_limit_bytes=VMEM_LIMIT),
        name="merge_router",
    )(ya, yb, gates, h, wa, wb, wo, fg, wr, br)


def _moe_kernel(x_ref, c_ref, h_ref, wg_ref, wu_ref, wd_ref, o_ref, acc_ref):
    e = pl.program_id(1)

    @pl.when(e == 0)
    def _():
        acc_ref[...] = h_ref[...]

    x = x_ref[...]
    a = _dot(x, wg_ref[0])
    u = _dot(x, wu_ref[0])
    c = c_ref[...]
    lane = lax.broadcasted_iota(jnp.int32, c.shape, 1)
    ce = jnp.sum(jnp.where(lane == e, c, 0.0), axis=-1, keepdims=True)
    act = (a * jax.nn.sigmoid(a)) * u * ce
    acc_ref[...] += _dot(act.astype(BF16), wd_ref[0])

    @pl.when(e == pl.num_programs(1) - 1)
    def _():
        o_ref[...] = acc_ref[...]


def _moe(xn, cmb, h, wg, wu, wd):
    n = h.shape[0]
    tm = min(1024, n)
    row = lambda i, e: (i, 0)
    return pl.pallas_call(
        _moe_kernel,
        grid=(n // tm, N_EXPERTS),
        in_specs=[
            pl.BlockSpec((tm, D_MODEL), row),
            pl.BlockSpec((tm, LANES), row),
            pl.BlockSpec((tm, D_MODEL), row),
            pl.BlockSpec((1, D_MODEL, EXPERT_FF), lambda i, e: (e, 0, 0)),
            pl.BlockSpec((1, D_MODEL, EXPERT_FF), lambda i, e: (e, 0, 0)),
            pl.BlockSpec((1, EXPERT_FF, D_MODEL), lambda i, e: (e, 0, 0)),
        ],
        out_specs=pl.BlockSpec((tm, D_MODEL), row),
        out_shape=jax.ShapeDtypeStruct((n, D_MODEL), F32),
        scratch_shapes=[pltpu.VMEM((tm, D_MODEL), F32)],
        compiler_params=pltpu.CompilerParams(
            dimension_semantics=("arbitrary", "arbitrary"), vmem_limit_bytes=VMEM_LIMIT),
        name="moe_experts",
    )(xn, cmb, h, wg, wu, wd)


def _final_kernel(h_ref, g_ref, o_ref, *, seq):
    step = 256 if seq % 256 == 0 else CHUNK
    for r in range(0, seq, step):
        x = h_ref[CHUNK + r:CHUNK + r + step, :]
        o_ref[0, r:r + step, :] = x * lax.rsqrt(jnp.mean(x * x, axis=-1, keepdims=True) + NORM_EPS) * g_ref[...]


def _final_norm(h, g, batch, lp, seq):
    return pl.pallas_call(
        functools.partial(_final_kernel, seq=seq),
        grid=(batch,),
        in_specs=[pl.BlockSpec((lp, D_MODEL), lambda b: (b, 0)), pl.BlockSpec((1, D_MODEL), lambda b: (0, 0))],
        out_specs=pl.BlockSpec((1, seq, D_MODEL), lambda b: (b, 0, 0)),
        out_shape=jax.ShapeDtypeStruct((batch, seq, D_MODEL), F32),
        compiler_params=pltpu.CompilerParams(
            dimension_semantics=("arbitrary",), vmem_limit_bytes=VMEM_LIMIT),
        name="final_norm",
    )(h, g)


def _rope_lane_tables(pos, head_dim):
    rot = head_dim // 4
    half = rot // 2
    inv = ROPE_THETA ** (-jnp.arange(0, rot, 2, dtype=F32) / rot)
    ang = pos[:, None] * inv[None, :]
    cos, sin = jnp.cos(ang), jnp.sin(ang)
    jj = np.arange(LANES) % head_dim
    first = jnp.asarray(jj < half)
    second = jnp.asarray((jj >= half) & (jj < rot))
    fidx = jnp.asarray(np.where(jj < half, jj, np.where(jj < rot, jj - half, 0)))
    cl, sl = cos[:, fidx], sin[:, fidx]
    c = jnp.where(first | second, cl, 1.0)
    s1 = jnp.where(second, sl, 0.0)
    s2 = jnp.where(first, -sl, 0.0)
    return c, s1, s2


def kernel(x, meta_tokens, mix_norm_g, w_in, hgrn_lb_logits, hgrn_norm_g, idx_k_norm_g, w_branch_hgrn,
           w_branch_dsa, w_out, ffn_norm_g, router_group_w, router_group_b, router_expert_w,
           router_expert_b, w_expert_gate, w_expert_up, w_expert_down, final_norm_g):
    batch, seq, _ = x.shape
    depth = w_in.shape[0]
    n_valid = CHUNK + seq
    assert seq % CHUNK == 0
    lp = -(-n_valid // Q_BLOCK) * Q_BLOCK
    n_chunks = n_valid // CHUNK
    topk = min(TOPK_MAX, (N_META + seq) // 4)

    meta = jnp.broadcast_to(meta_tokens[None].astype(x.dtype), (batch, N_META, D_MODEL))
    h = jnp.concatenate([jnp.zeros((batch, FRONT, D_MODEL), x.dtype), meta, x,
                         jnp.zeros((batch, lp - n_valid, D_MODEL), x.dtype)], axis=1)
    h = h.reshape(batch * lp, D_MODEL)

    pos = jnp.clip(jnp.arange(lp, dtype=jnp.int32) - FRONT, 0, N_META + seq - 1).astype(F32)
    tabs = jnp.stack(_rope_lane_tables(pos, ATT_HEAD_DIM) + _rope_lane_tables(pos, IDX_HEAD_DIM))

    p = jax.nn.softmax(hgrn_lb_logits.astype(F32), axis=0)
    cs = jnp.cumsum(p, axis=0)
    lower_bounds = cs - cs[0:1]

    rep = np.zeros((LANES, LANES), np.float32)
    for slot in range(IDX_HEADS_PER_LANE_GROUP):
        rep[np.arange(IDX_HEAD_DIM), slot * IDX_HEAD_DIM + np.arange(IDX_HEAD_DIM)] = 1.0
    rep = jnp.asarray(rep, BF16)
    ltri = jnp.asarray(np.tril(np.ones((Q_BLOCK, Q_BLOCK), np.float32)), BF16)

    pad_cols = jnp.zeros((D_MODEL, IDX_PAD - IDX_RAW), F32)
    for layer in range(depth):
        w = w_in[layer]
        w_p = jnp.concatenate([w[:, :COL_IDX + IDX_RAW], pad_cols, w[:, COL_IDX + IDX_RAW:]], axis=1).astype(BF16)
        ikg = jnp.pad(idx_k_norm_g[layer].astype(F32), (0, LANES - IDX_HEAD_DIM))[None]
        hg, gates, q, k, v, iq, ik, iw = _inproj(h, mix_norm_g[layer][None].astype(F32), w_p, tabs, ikg, rep, lp)
        ya = _hgrn(hg, lower_bounds[layer][None], hgrn_norm_g[layer][None].astype(F32), batch, lp, n_chunks)
        yb = _dsa(q, k, v, iq, ik, iw, ltri, batch, lp, topk, n_valid)
        wr = jnp.concatenate([router_expert_w[layer], router_group_w[layer],
                              jnp.zeros((D_MODEL, LANES - N_EXPERTS - N_GROUPS), F32)], axis=1)
        wr_hi = wr.astype(BF16)
        wr = jnp.concatenate([wr_hi, (wr - wr_hi.astype(F32)).astype(BF16)], axis=1)
        br = jnp.concatenate([router_expert_b[layer], router_group_b[layer],
                              jnp.zeros((LANES - N_EXPERTS - N_GROUPS,), F32)])[None]
        h, xn, cmb = _merge(ya, yb, gates, h, w_branch_hgrn[layer].astype(BF16), w_branch_dsa[layer].astype(BF16),
                            w_out[layer].astype(BF16), ffn_norm_g[layer][None].astype(F32), wr, br)
        h = _moe(xn, cmb, h, w_expert_gate[layer].astype(BF16), w_expert_up[layer].astype(BF16),
                 w_expert_down[layer].astype(BF16))
    return _final_norm(h, final_norm_g[None].astype(F32), batch, lp, seq)
```

```python
import functools

import jax
import jax.numpy as jnp
import numpy as np
from jax import lax
from jax.experimental import pallas as pl
from jax.experimental.pallas import tpu as pltpu

F32 = jnp.float32
BF16 = jnp.bfloat16

D_MODEL = 1024
CHUNK = 64
N_META = 16
FRONT = CHUNK - N_META
ROPE_THETA = 500000.0
NORM_EPS = 1e-6
LANES = 128
SUBLANES = 8

HGRN_HEADS = 4
HGRN_HEAD_DIM = 128
HGRN_WIDTH = HGRN_HEADS * HGRN_HEAD_DIM
SUB = 8
HGRN_MAX_GROUP = 11

ATT_HEADS = 8
ATT_KV_HEADS = 2
ATT_HEAD_DIM = 64
ATT_GROUP = ATT_HEADS // ATT_KV_HEADS
ATT_WIDTH = ATT_HEADS * ATT_HEAD_DIM
ATT_KV_WIDTH = ATT_KV_HEADS * ATT_HEAD_DIM
ATT_ROPE_HALF = ATT_HEAD_DIM // 4 // 2
IDX_HEADS = 8
IDX_HEAD_DIM = 32
IDX_WIDTH = IDX_HEADS * IDX_HEAD_DIM
IDX_ROPE_HALF = IDX_HEAD_DIM // 4 // 2
IDX_HEADS_PER_LANE_GROUP = LANES // IDX_HEAD_DIM
TOPK_MAX = 256
Q_BLOCK = 128
DSA_EXTENT_STEP = 4
FOLD = 64

N_GROUPS = 4
EXPERTS_PER_GROUP = 8
N_EXPERTS = N_GROUPS * EXPERTS_PER_GROUP
EXPERT_FF = D_MODEL // 4
MERGE_TILE = 512
MOE_SLOT_TILE = 512

COL_HG = 0
COL_ATT = 4 * HGRN_WIDTH
COL_IDX = COL_ATT + ATT_WIDTH + 2 * ATT_KV_WIDTH
IDX_RAW = IDX_WIDTH + IDX_HEAD_DIM + IDX_HEADS
IDX_PAD = IDX_WIDTH + LANES
COL_GATE = COL_IDX + IDX_PAD
IN_WIDTH_PAD = COL_GATE + 2 * D_MODEL

INT_MIN = np.int32(-2 ** 31)
NEG_FLT_MAX_KEY = np.int32(-2 ** 31 + 0x800000)
NEG_INF = float("-inf")
POS_INF = float("inf")

VMEM_LIMIT = 56 * 1024 * 1024


def _dot(a, b):
    return jnp.dot(a, b, preferred_element_type=F32)


def _dot_nt(a, b):
    return lax.dot_general(a, b, (((1,), (1,)), ((), ())), preferred_element_type=F32)


def _rope(x, c, s1, s2, half):
    return x * c + pltpu.roll(x, half, 1) * s1 + pltpu.roll(x, LANES - half, 1) * s2


def _inproj_kernel(h_ref, g_ref, w_ref, tab_ref, ikg_ref, rep_ref,
                   hg_ref, gate_ref, q_ref, k_ref, v_ref, iq_ref, ik_ref, iw_ref):
    x = h_ref[...]
    xn = (x * lax.rsqrt(jnp.mean(x * x, axis=-1, keepdims=True) + NORM_EPS) * g_ref[...]).astype(BF16)
    hg_ref[...] = _dot(xn, w_ref[:, COL_HG:COL_ATT])
    gate_ref[...] = _dot(xn, w_ref[:, COL_GATE:IN_WIDTH_PAD])
    att = _dot(xn, w_ref[:, COL_ATT:COL_IDX])
    idx = _dot(xn, w_ref[:, COL_IDX:COL_GATE])
    ca, s1a, s2a = tab_ref[0], tab_ref[1], tab_ref[2]
    ci, s1i, s2i = tab_ref[3], tab_ref[4], tab_ref[5]
    scale = ATT_HEAD_DIM ** -0.5
    for m in range(ATT_WIDTH // LANES):
        sl = slice(m * LANES, (m + 1) * LANES)
        q_ref[:, sl] = (_rope(att[:, sl], ca, s1a, s2a, ATT_ROPE_HALF) * scale).astype(BF16)
    k_ref[...] = _rope(att[:, ATT_WIDTH:ATT_WIDTH + LANES], ca, s1a, s2a, ATT_ROPE_HALF).astype(BF16)
    v_ref[...] = att[:, ATT_WIDTH + LANES:].astype(BF16)
    for m in range(IDX_WIDTH // LANES):
        sl = slice(m * LANES, (m + 1) * LANES)
        iq_ref[:, sl] = _rope(idx[:, sl], ci, s1i, s2i, IDX_ROPE_HALF).astype(BF16)
    xk = idx[:, IDX_WIDTH:]
    lane = lax.broadcasted_iota(jnp.int32, xk.shape, 1)
    ms = jnp.sum(jnp.where(lane < IDX_HEAD_DIM, xk * xk, 0.0), axis=-1, keepdims=True) / IDX_HEAD_DIM
    ikn = xk * lax.rsqrt(ms + NORM_EPS) * ikg_ref[...]
    ikr = _rope(ikn, ci, s1i, s2i, IDX_ROPE_HALF).astype(BF16)
    ik_ref[...] = _dot(ikr, rep_ref[...]).astype(BF16)
    iw_ref[...] = xk * (IDX_HEADS ** -0.5 * IDX_HEAD_DIM ** -0.5)


def _inproj(h, g, w, tabs, ikg, rep, lp):
    n = h.shape[0]
    tm = lp // 8
    per = lp // tm
    row = lambda i: (i, 0)
    const2 = lambda i: (0, 0)
    out_shape = (
        jax.ShapeDtypeStruct((n, 4 * HGRN_WIDTH), F32),
        jax.ShapeDtypeStruct((n, 2 * D_MODEL), F32),
        jax.ShapeDtypeStruct((n, ATT_WIDTH), BF16),
        jax.ShapeDtypeStruct((n, LANES), BF16),
        jax.ShapeDtypeStruct((n, LANES), BF16),
        jax.ShapeDtypeStruct((n, IDX_WIDTH), BF16),
        jax.ShapeDtypeStruct((n, LANES), BF16),
        jax.ShapeDtypeStruct((n, LANES), F32),
    )
    return pl.pallas_call(
        _inproj_kernel,
        grid=(n // tm,),
        in_specs=[
            pl.BlockSpec((tm, D_MODEL), row),
            pl.BlockSpec((1, D_MODEL), const2),
            pl.BlockSpec((D_MODEL, IN_WIDTH_PAD), const2),
            pl.BlockSpec((6, tm, LANES), lambda i: (0, i % per, 0)),
            pl.BlockSpec((1, LANES), const2),
            pl.BlockSpec((LANES, LANES), const2),
        ],
        out_specs=tuple(pl.BlockSpec((tm, s.shape[1]), row) for s in out_shape),
        out_shape=out_shape,
        compiler_params=pltpu.CompilerParams(
            dimension_semantics=("arbitrary",), vmem_limit_bytes=VMEM_LIMIT),
        name="inproj",
    )(h, g, w, tabs, ikg, rep)


def _hgrn_kernel(q_ref, f_ref, i_ref, g_ref, lb_ref, ng_ref, o_ref, st_ref, kb_ref, bb_ref, vb_ref,
                 *, n_chunks, group):
    C = CHUNK
    D = HGRN_HEAD_DIM
    st_ref[...] = jnp.zeros_like(st_ref)
    kb_ref[...] = jnp.zeros_like(kb_ref)
    bb_ref[...] = jnp.zeros_like(bb_ref)
    vb_ref[...] = jnp.zeros_like(vb_ref)
    lb = lb_ref[...]
    ng = ng_ref[...]
    R = group * C
    tri_row = lax.broadcasted_iota(jnp.int32, (C, C), 0)
    tri_col = lax.broadcasted_iota(jnp.int32, (C, C), 1)
    tri = (tri_row >= tri_col).astype(F32)
    local_row = lax.broadcasted_iota(jnp.int32, (R, 1), 0)
    sub_pos = local_row & (SUB - 1)

    def chunk(x, u):
        return x[u * C:(u + 1) * C, :]

    def body(it, carry):
        rows = pl.ds(pl.multiple_of(it * R, C), R)
        z = f_ref[rows, :]
        q = q_ref[rows, :]
        v = i_ref[rows, :]
        go = g_ref[rows, :]
        pad = (it * R + local_row) < FRONT
        f = lb + (1.0 - lb) * jax.nn.sigmoid(z)
        logf = jnp.where(pad, 0.0, jnp.log(f))
        k = jnp.where(pad, 0.0, (1.0 - lb) * jax.nn.sigmoid(-z))
        v = jnp.where(pad, 0.0, v)
        b = jnp.concatenate(
            [jnp.dot(tri, chunk(logf, u), preferred_element_type=F32, precision=lax.Precision.HIGHEST)
             for u in range(group)], axis=0)
        b3 = b.reshape(group, C, D)
        b_last = b3[:, C - 1:C, :]
        qd = (q * jnp.exp(b)).astype(BF16)
        kd = (k.reshape(group, C, D) * jnp.exp(b_last - b3)).reshape(R, D).astype(BF16)
        e_last = jnp.exp(b_last)
        upd =[_dot(chunk(v, u).T.astype(BF16), chunk(kd, u)) for u in range(group)]

        st = st_ref[...]
        before = []
        for u in range(group):
            before.append(st.astype(BF16))
            st = st * e_last[u] + upd[u]
        st_ref[...] = st
        o = jnp.concatenate([_dot_nt(chunk(qd, u), before[u]) for u in range(group)], axis=0)

        kb_ref[SUB:, :] = k
        bb_ref[SUB:, :] = b
        vb_ref[SUB:, :] = v
        for j in range(SUB):
            ks = kb_ref[SUB - j:SUB - j + R, :]
            bs = bb_ref[SUB - j:SUB - j + R, :]
            vs = vb_ref[SUB - j:SUB - j + R, :]
            w = jnp.exp(jnp.minimum(b - bs, 0.0))
            a = jnp.sum(q * ks * w, axis=-1, keepdims=True)
            o = o + jnp.where(sub_pos >= j, a, 0.0) * vs

        att = [jnp.zeros((C, C), F32) for _ in range(group)]
        span = C
        while span > SUB:
            half = span // 2
            upper = (local_row & (span - 1)) >= half
            bs3 = b.reshape(R // span, span, D)
            mid = bs3[:, half - 1:half, :]
            qs = jnp.where(upper, (q.reshape(bs3.shape) * jnp.exp(jnp.minimum(bs3 - mid, 0.0))).reshape(R, D), 0.0)
            ks = jnp.where(upper, 0.0, (k.reshape(bs3.shape) * jnp.exp(jnp.minimum(mid - bs3, 0.0))).reshape(R, D))
            qs, ks = qs.astype(BF16), ks.astype(BF16)
            same_block = (tri_row & -span) == (tri_col & -span)
            att = [att[u] + jnp.where(same_block, _dot_nt(chunk(qs, u), chunk(ks, u)), 0.0) for u in range(group)]
            span = half
        o = o + jnp.concatenate([_dot(att[u].astype(BF16), chunk(v, u).astype(BF16)) for u in range(group)], axis=0)

        on = o * lax.rsqrt(jnp.mean(o * o, axis=-1, keepdims=True) + NORM_EPS) * ng
        o_ref[rows, :] = (on * (go * jax.nn.sigmoid(go))).astype(o_ref.dtype)
        return carry

    lax.fori_loop(0, n_chunks // group, body, 0)
    tail = o_ref.shape[0] - n_chunks * C
    if tail:
        o_ref[n_chunks * C:, :] = jnp.zeros((tail, D), o_ref.dtype)


def _hgrn(hg, lb, ng, batch, lp, n_chunks):
    n = hg.shape[0]
    H, D = HGRN_HEADS, HGRN_HEAD_DIM
    group = max(g for g in range(1, HGRN_MAX_GROUP + 1) if n_chunks % g == 0)

    def col(k):
        return pl.BlockSpec((lp, D), lambda b, h: (b, k * H + h))

    vec = pl.BlockSpec((1, D), lambda b, h: (0, h))
    return pl.pallas_call(
        functools.partial(_hgrn_kernel, n_chunks=n_chunks, group=group),
        grid=(batch, H),
        in_specs=[col(0), col(1), col(2), col(3), vec, vec],
        out_specs=pl.BlockSpec((lp, D), lambda b, h: (b, h)),
        out_shape=jax.ShapeDtypeStruct((n, HGRN_WIDTH), BF16),
        scratch_shapes=[
            pltpu.VMEM((D, D), F32),
            pltpu.VMEM((group * CHUNK + SUB, D), F32),
            pltpu.VMEM((group * CHUNK + SUB, D), F32),
            pltpu.VMEM((group * CHUNK + SUB, D), F32),
        ],
        compiler_params=pltpu.CompilerParams(
            dimension_semantics=("arbitrary", "arbitrary"), vmem_limit_bytes=VMEM_LIMIT),
        name="hgrn2",
    )(hg, hg, hg, hg, lb, ng)


def _sortable_to_f32(key):
    bits = jnp.where(key < 0, key ^ jnp.int32(0x7FFFFFFF), key)
    return pltpu.bitcast(bits, F32)


def _fold_rows(x, op):
    rows = x.shape[0]
    if rows > FOLD and rows % FOLD == 0:
        x = op(x.reshape(rows // FOLD, FOLD, x.shape[1]), axis=0)
    return op(x, axis=0, keepdims=True)


def _dsa_kernel(q_ref, iq_ref, iw_ref, k_ref, v_ref, ik_ref, ltri_ref, o_ref,
                sc_ref, vt_ref, iq8_ref, qg_ref, ot_ref, *, topk, n_valid, extents):
    QB = Q_BLOCK
    lp = k_ref.shape[0]
    t = pl.program_id(1)
    kf = float(topk)

    @pl.when(t == 0)
    def _():
        for kb in range(lp // QB):
            sl = slice(kb * QB, (kb + 1) * QB)
            vt_ref[:, sl] = v_ref[sl, :].astype(F32).T.astype(BF16)

    lane = lax.broadcasted_iota(jnp.int32, (QB, LANES), 1)
    iq = iq_ref[...]
    for h in range(IDX_HEADS):
        grp, slot = divmod(h, IDX_HEADS_PER_LANE_GROUP)
        part = iq[:, grp * LANES:(grp + 1) * LANES]
        iq8_ref[h * QB:(h + 1) * QB, :] = jnp.where((lane >> 5) == slot, part, jnp.zeros_like(part))
    wt = iw_ref[...].T
    lane_half = lane >> 6
    for h in range(ATT_HEADS):
        g, r = divmod(h, ATT_GROUP)
        m, p = divmod(h, LANES // ATT_HEAD_DIM)
        qh = q_ref[:, m * LANES:(m + 1) * LANES].astype(F32)
        if p != g:
            qh = pltpu.roll(qh, ATT_HEAD_DIM, 1)
        qg_ref[g, r * QB:(r + 1) * QB, :] = jnp.where(lane_half == g, qh, 0.0).astype(BF16)

    qpos = t * QB + lax.broadcasted_iota(jnp.int32, (1, QB), 1)
    qchunk = qpos >> 6
    n_adm = jnp.minimum((qchunk + 1) * CHUNK, n_valid) - FRONT
    take_all = n_adm <= topk
    ltri = ltri_ref[...]

    def tile_body(n_blocks):
        ke = n_blocks * QB
        keys = slice(0, ke)
        kpos = lax.broadcasted_iota(jnp.int32, (ke, 1), 0)
        adm = ((kpos >> 6) <= qchunk) & (kpos >= FRONT) & (kpos < n_valid)

        ik = ik_ref[keys, :]
        acc = jnp.zeros((ke, QB), F32)
        for h in range(IDX_HEADS):
            rel = jnp.maximum(_dot_nt(ik, iq8_ref[h * QB:(h + 1) * QB, :]), 0.0)
            acc = acc + rel * wt[IDX_HEAD_DIM + h:IDX_HEAD_DIM + h + 1, :]
        acc = jnp.where(acc == 0.0, 0.0, acc)
        sc_ref[keys, :] = jnp.where(adm, acc, NEG_INF)

        def count_ge(cand):
            return _fold_rows(jnp.where(sc_ref[keys, :] >= cand, 1.0, 0.0), jnp.sum)

        def counts(cand):
            x = sc_ref[keys, :]
            return (_fold_rows(jnp.where(x >= cand, 1.0, 0.0), jnp.sum),
                    _fold_rows(jnp.where(x > cand, 1.0, 0.0), jnp.sum))

        def search(i, tkey):
            cand = tkey + lax.shift_left(jnp.int32(1), 31 - i)
            return jnp.where(count_ge(_sortable_to_f32(cand)) >= kf, cand, tkey)

        tkey = lax.fori_loop(0, 32, search, jnp.full((1, QB), INT_MIN, jnp.int32))
        thr = _sortable_to_f32(jnp.where(take_all, NEG_FLT_MAX_KEY, tkey))
        c_ge, c_gt = counts(thr)

        def unsettled(c_ge, c_gt):
            live = jnp.logical_not(take_all)
            return (c_gt >= kf) & live, (c_ge < kf) & live

        def cond(carry):
            up, dn = unsettled(carry[1], carry[2])
            return jnp.max(jnp.where(up | dn, 1.0, 0.0)) > 0.0

        def fix(carry):
            thr, c_ge, c_gt = carry
            up, dn = unsettled(c_ge, c_gt)
            x = sc_ref[keys, :]
            above = _fold_rows(jnp.where(x > thr, x, POS_INF), jnp.min)
            below = _fold_rows(jnp.where(x < thr, x, NEG_INF), jnp.max)
            thr = jnp.where(up, above, jnp.where(dn, below, thr))
            return (thr,) + counts(thr)

        thr, c_ge, c_gt = lax.while_loop(cond, fix, (thr, c_ge, c_gt))

        need = kf - c_gt
        seen = jnp.zeros((1, QB), F32)
        for kb in range(n_blocks):
            blk = slice(kb * QB, (kb + 1) * QB)
            x = sc_ref[blk, :]
            eq = x == thr
            pre = _dot(ltri, jnp.where(eq, 1.0, 0.0).astype(BF16))
            take = (x > thr) | (eq & (pre + seen <= need))
            sc_ref[blk, :] = jnp.where(take, 0.0, NEG_INF)
            seen = seen + pre[QB - 1:QB, :]

        bias = sc_ref[keys, :]
        bias = jnp.concatenate([bias] * ATT_GROUP, axis=1)
        kk = k_ref[keys, :]
        for g in range(ATT_KV_HEADS):
            s = _dot_nt(kk, qg_ref[g]) + bias
            e = jnp.exp(s - _fold_rows(s, jnp.max))
            vt = vt_ref[g * ATT_HEAD_DIM:(g + 1) * ATT_HEAD_DIM, keys]
            pv = _dot(vt, e.astype(BF16)) / _fold_rows(e, jnp.sum)
            for r in range(ATT_GROUP):
                h = g * ATT_GROUP + r
                ot_ref[h * ATT_HEAD_DIM:(h + 1) * ATT_HEAD_DIM, :] = pv[:, r * QB:(r + 1) * QB]
        o_ref[...] = ot_ref[...].T.astype(o_ref.dtype)

    lo = 0
    for n_blocks in extents:
        pl.when((t >= lo) & (t < n_blocks))(functools.partial(tile_body, n_blocks))
        lo = n_blocks


def _dsa(q, k, v, iq, ik, iw, ltri, batch, lp, topk, n_valid):
    n = q.shape[0]
    nq = lp // Q_BLOCK
    extents = tuple(sorted({nq - DSA_EXTENT_STEP * i for i in range(-(-nq // DSA_EXTENT_STEP))}))
    qblk = lambda b, i: (b * nq + i, 0)
    bat = lambda b, i: (b, 0)
    return pl.pallas_call(
        functools.partial(_dsa_kernel, topk=topk, n_valid=n_valid, extents=extents),
        grid=(batch, nq),
        in_specs=[
            pl.BlockSpec((Q_BLOCK, ATT_WIDTH), qblk),
            pl.BlockSpec((Q_BLOCK, IDX_WIDTH), qblk),
            pl.BlockSpec((Q_BLOCK, LANES), qblk),
            pl.BlockSpec((lp, LANES), bat),
            pl.BlockSpec((lp, LANES), bat),
            pl.BlockSpec((lp, LANES), bat),
            pl.BlockSpec((Q_BLOCK, Q_BLOCK), lambda b, i: (0, 0)),
        ],
        out_specs=pl.BlockSpec((Q_BLOCK, ATT_WIDTH), qblk),
        out_shape=jax.ShapeDtypeStruct((n, ATT_WIDTH), BF16),
        scratch_shapes=[
            pltpu.VMEM((lp, Q_BLOCK), F32),
            pltpu.VMEM((LANES, lp), BF16),
            pltpu.VMEM((IDX_HEADS * Q_BLOCK, LANES), BF16),
            pltpu.VMEM((ATT_KV_HEADS, ATT_GROUP * Q_BLOCK, LANES), BF16),
            pltpu.VMEM((ATT_WIDTH, Q_BLOCK), F32),
        ],
        compiler_params=pltpu.CompilerParams(
            dimension_semantics=("arbitrary", "arbitrary"), vmem_limit_bytes=VMEM_LIMIT),
        name="dsa",
    )(q, iq, iw, k, v, ik, ltri)


def _merge_kernel(ya_ref, yb_ref, gate_ref, h_ref, wa_ref, wb_ref, wo_ref, fg_ref, wr_ref, br_ref, ltri_ref,
                  ho_ref, xc_ref, route_ref, cnt_ref):
    @pl.when(pl.program_id(0) == 0)
    def _():
        cnt_ref[...] = jnp.zeros_like(cnt_ref)

    gate = gate_ref[...]
    pa = _dot(ya_ref[...], wa_ref[...])
    pb = _dot(yb_ref[...], wb_ref[...])
    merged = jax.nn.sigmoid(gate[:, :D_MODEL]) * pa + jax.nn.sigmoid(gate[:, D_MODEL:]) * pb
    h = h_ref[...] + _dot(merged.astype(BF16), wo_ref[...])
    ho_ref[...] = h
    xn = h * lax.rsqrt(jnp.mean(h * h, axis=-1, keepdims=True) + NORM_EPS) * fg_ref[...]
    xh = xn.astype(BF16)
    xc_ref[:, :D_MODEL] = xh.astype(F32)

    xl = (xn - xh.astype(F32)).astype(BF16)
    r = _dot(xh, wr_ref[...])
    lg = r[:, :LANES] + r[:, LANES:] + _dot(xl, wr_ref[:, :LANES]) + br_ref[...]
    lane = lax.broadcasted_iota(jnp.int32, lg.shape, 1)
    lane_f = lane.astype(F32)
    big = float(LANES)
    is_g = (lane >= N_EXPERTS) & (lane < N_EXPERTS + N_GROUPS)
    gl = jnp.where(is_g, lg, NEG_INF)
    gmax = jnp.max(gl, axis=-1, keepdims=True)
    pg_top = 1.0 / jnp.sum(jnp.exp(gl - gmax), axis=-1, keepdims=True)
    g_lane = jnp.min(jnp.where(gl == gmax, lane_f, big), axis=-1, keepdims=True)
    e_lo = (g_lane - N_EXPERTS) * EXPERTS_PER_GROUP
    in_grp = (lane_f >= e_lo) & (lane_f < e_lo + EXPERTS_PER_GROUP)
    el = jnp.where(in_grp, lg, NEG_INF)
    ee = jnp.exp(el - jnp.max(el, axis=-1, keepdims=True))
    pe = ee / jnp.sum(ee, axis=-1, keepdims=True)
    pe = jnp.where(in_grp, pe, -1.0)
    p1 = jnp.max(pe, axis=-1, keepdims=True)
    i1 = jnp.min(jnp.where(pe == p1, lane_f, big), axis=-1, keepdims=True)
    pe2 = jnp.where(lane_f == i1, -1.0, pe)
    p2 = jnp.max(pe2, axis=-1, keepdims=True)
    i2 = jnp.min(jnp.where(pe2 == p2, lane_f, big), axis=-1, keepdims=True)
    tot = p1 + p2
    xc_ref[:, D_MODEL:] = jnp.where(lane_f == i1, p1 / tot * pg_top,
                                    jnp.where(lane_f == i2, p2 / tot * pg_top, 0.0))

    grp = g_lane - N_EXPERTS
    onehot = jnp.where(lane_f == grp, 1.0, 0.0)
    incl = _dot(ltri_ref[...], onehot.astype(BF16))
    seen = cnt_ref[0:1, :]
    rank = jnp.sum(onehot * (incl - 1.0 + seen), axis=-1, keepdims=True)
    route_ref[...] = jnp.where(lane == 0, grp, jnp.where(lane == 1, rank, 0.0))
    cnt_ref[...] = jnp.broadcast_to(seen + incl[incl.shape[0] - 1:, :], cnt_ref.shape)


def _merge(ya, yb, gates, h, wa, wb, wo, fg, wr, br):
    n = h.shape[0]
    tm = min(MERGE_TILE, n)
    ltri = jnp.asarray(np.tril(np.ones((tm, tm), np.float32)), BF16)
    row = lambda i: (i, 0)
    const = lambda i: (0, 0)
    return pl.pallas_call(
        _merge_kernel,
        grid=(n // tm,),
        in_specs=[
            pl.BlockSpec((tm, HGRN_WIDTH), row),
            pl.BlockSpec((tm, ATT_WIDTH), row),
            pl.BlockSpec((tm, 2 * D_MODEL), row),
            pl.BlockSpec((tm, D_MODEL), row),
            pl.BlockSpec((HGRN_WIDTH, D_MODEL), const),
            pl.BlockSpec((ATT_WIDTH, D_MODEL), const),
            pl.BlockSpec((D_MODEL, D_MODEL), const),
            pl.BlockSpec((1, D_MODEL), const),
            pl.BlockSpec((D_MODEL, 2 * LANES), const),
            pl.BlockSpec((1, LANES), const),
            pl.BlockSpec((tm, tm), const),
        ],
        out_specs=(pl.BlockSpec((tm, D_MODEL), row), pl.BlockSpec((tm, D_MODEL + LANES), row),
                   pl.BlockSpec((tm, LANES), row), pl.BlockSpec((SUBLANES, LANES), const)),
        out_shape=(jax.ShapeDtypeStruct((n, D_MODEL), F32), jax.ShapeDtypeStruct((n, D_MODEL + LANES), F32),
                   jax.ShapeDtypeStruct((n, LANES), F32), jax.ShapeDtypeStruct((SUBLANES, LANES), F32)),
        compiler_params=pltpu.CompilerParams(
            dimension_semantics=("arbitrary",), vmem_limit_bytes=VMEM_LIMIT),
        name="merge_router",
    )(ya, yb, gates, h, wa, wb, wo, fg, wr, br, ltri)


def _row_copies(n_rows, make_copy):
    def issue(r, c):
        make_copy(r).start()
        return c

    def drain(r, c):
        make_copy(r).wait()
        return c

    lax.fori_loop(0, n_rows, issue, 0)
    lax.fori_loop(0, n_rows, drain, 0)


def _dispatch_kernel(dest_ref, x_ref, zeros_ref, xs_ref, sem):
    del zeros_ref

    def row_copy(r):
        return pltpu.make_async_copy(x_ref.at[pl.ds(r, 1)], xs_ref.at[pl.ds(dest_ref[0, 0, r], 1)], sem)

    _row_copies(x_ref.shape[0], row_copy)


def _experts_kernel(tile_group_ref, tile_on_ref, xs_ref, wg_ref, wu_ref, wd_ref, ys_ref):
    t = pl.program_id(0)
    tm = xs_ref.shape[0]

    @pl.when(tile_on_ref[t] == 1)
    def _():
        x = xs_ref[:, :D_MODEL].astype(BF16)
        c = xs_ref[:, D_MODEL:]
        lane = lax.broadcasted_iota(jnp.int32, c.shape, 1)
        first = tile_group_ref[t] * EXPERTS_PER_GROUP
        scale = jnp.concatenate(
            [jnp.broadcast_to(jnp.sum(jnp.where(lane == first + e, c, 0.0), axis=-1, keepdims=True), (tm, EXPERT_FF))
             for e in range(EXPERTS_PER_GROUP)], axis=1)
        a = _dot(x, wg_ref[0])
        u = _dot(x, wu_ref[0])
        act = (a * jax.nn.sigmoid(a)) * u * scale
        ys_ref[...] = _dot(act.astype(BF16), wd_ref[0])

    @pl.when(tile_on_ref[t] == 0)
    def _():
        ys_ref[...] = jnp.zeros_like(ys_ref)


def _combine_kernel(dest_ref, h_ref, ys_ref, o_ref, buf_ref, sem):
    def row_copy(r):
        return pltpu.make_async_copy(ys_ref.at[pl.ds(dest_ref[0, 0, r], 1)], buf_ref.at[pl.ds(r, 1)], sem)

    _row_copies(h_ref.shape[0], row_copy)
    o_ref[...] = h_ref[...] + buf_ref[...]


def _moe(xc, route, counts, h, wg, wu, wd):
    n = h.shape[0]
    tm = min(MERGE_TILE, n)
    ts = min(MOE_SLOT_TILE, n)
    n_tiles = n // ts + N_GROUPS
    n_slots = n_tiles * ts

    grp = route[:, 0].astype(jnp.int32)
    rank = route[:, 1].astype(jnp.int32)
    cnt = counts[0, :N_GROUPS].astype(jnp.int32)
    padded = (cnt + ts - 1) // ts * ts
    ends = jnp.cumsum(padded)
    starts = ends - padded
    dest = (starts[grp] + rank).reshape(n // tm, 1, tm)
    tile_start = jnp.arange(n_tiles, dtype=jnp.int32) * ts
    tile_group = jnp.minimum(jnp.sum(tile_start[:, None] >= ends[None, :], axis=1), N_GROUPS - 1).astype(jnp.int32)
    tile_on = (tile_start < ends[N_GROUPS - 1]).astype(jnp.int32)

    tok = lambda i: (i, 0)
    smem_idx = pl.BlockSpec((1, 1, tm), lambda i: (i, 0, 0), memory_space=pltpu.SMEM)
    hbm = pl.BlockSpec(memory_space=pl.ANY)
    xs = pl.pallas_call(
        _dispatch_kernel,
        grid=(n // tm,),
        in_specs=[smem_idx, pl.BlockSpec((tm, D_MODEL + LANES), tok), hbm],
        out_specs=hbm,
        out_shape=jax.ShapeDtypeStruct((n_slots, D_MODEL + LANES), F32),
        scratch_shapes=[pltpu.SemaphoreType.DMA(())],
        input_output_aliases={2: 0},
        compiler_params=pltpu.CompilerParams(
            dimension_semantics=("arbitrary",), vmem_limit_bytes=VMEM_LIMIT),
        name="moe_dispatch",
    )(dest, xc, jnp.zeros((n_slots, D_MODEL + LANES), F32))

    wspec = lambda shape: pl.BlockSpec((1,) + shape, lambda t, tg, on: (tg[t], 0, 0))
    ys = pl.pallas_call(
        _experts_kernel,
        grid_spec=pltpu.PrefetchScalarGridSpec(
            num_scalar_prefetch=2,
            grid=(n_tiles,),
            in_specs=[pl.BlockSpec((ts, D_MODEL + LANES), lambda t, tg, on: (t, 0)),
                      wspec((D_MODEL, EXPERTS_PER_GROUP * EXPERT_FF)),
                      wspec((D_MODEL, EXPERTS_PER_GROUP * EXPERT_FF)),
                      wspec((EXPERTS_PER_GROUP * EXPERT_FF, D_MODEL))],
            out_specs=pl.BlockSpec((ts, D_MODEL), lambda t, tg, on: (t, 0)),
        ),
        out_shape=jax.ShapeDtypeStruct((n_slots, D_MODEL), F32),
        compiler_params=pltpu.CompilerParams(
            dimension_semantics=("arbitrary",), vmem_limit_bytes=VMEM_LIMIT),
        name="moe_experts",
    )(tile_group, tile_on, xs, wg, wu, wd)

    return pl.pallas_call(
        _combine_kernel,
        grid=(n // tm,),
        in_specs=[smem_idx, pl.BlockSpec((tm, D_MODEL), tok), hbm],
        out_specs=pl.BlockSpec((tm, D_MODEL), tok),
        out_shape=jax.ShapeDtypeStruct((n, D_MODEL), F32),
        scratch_shapes=[pltpu.VMEM((tm, D_MODEL), F32), pltpu.SemaphoreType.DMA(())],
        compiler_params=pltpu.CompilerParams(
            dimension_semantics=("arbitrary",), vmem_limit_bytes=VMEM_LIMIT),
        name="moe_combine",
    )(dest, h, ys)


def _final_kernel(h_ref, g_ref, o_ref, *, seq):
    step = 256 if seq % 256 == 0 else CHUNK
    for r in range(0, seq, step):
        x = h_ref[CHUNK + r:CHUNK + r + step, :]
        o_ref[0, r:r + step, :] = x * lax.rsqrt(jnp.mean(x * x, axis=-1, keepdims=True) + NORM_EPS) * g_ref[...]


def _final_norm(h, g, batch, lp, seq):
    return pl.pallas_call(
        functools.partial(_final_kernel, seq=seq),
        grid=(batch,),
        in_specs=[pl.BlockSpec((lp, D_MODEL), lambda b: (b, 0)), pl.BlockSpec((1, D_MODEL), lambda b: (0, 0))],
        out_specs=pl.BlockSpec((1, seq, D_MODEL), lambda b: (b, 0, 0)),
        out_shape=jax.ShapeDtypeStruct((batch, seq, D_MODEL), F32),
        compiler_params=pltpu.CompilerParams(
            dimension_semantics=("arbitrary",), vmem_limit_bytes=VMEM_LIMIT),
        name="final_norm",
    )(h, g)


def _rope_lane_tables(pos, head_dim):
    rot = head_dim // 4
    half = rot // 2
    inv = ROPE_THETA ** (-jnp.arange(0, rot, 2, dtype=F32) / rot)
    ang = pos[:, None] * inv[None, :]
    cos, sin = jnp.cos(ang), jnp.sin(ang)
    jj = np.arange(LANES) % head_dim
    first = jnp.asarray(jj < half)
    second = jnp.asarray((jj >= half) & (jj < rot))
    fidx = jnp.asarray(np.where(jj < half, jj, np.where(jj < rot, jj - half, 0)))
    cl, sl = cos[:, fidx], sin[:, fidx]
    c = jnp.where(first | second, cl, 1.0)
    s1 = jnp.where(second, sl, 0.0)
    s2 = jnp.where(first, -sl, 0.0)
    return c, s1, s2


def kernel(x, meta_tokens, mix_norm_g, w_in, hgrn_lb_logits, hgrn_norm_g, idx_k_norm_g, w_branch_hgrn,
           w_branch_dsa, w_out, ffn_norm_g, router_group_w, router_group_b, router_expert_w,
           router_expert_b, w_expert_gate, w_expert_up, w_expert_down, final_norm_g):
    batch, seq, _ = x.shape
    depth = w_in.shape[0]
    n_valid = CHUNK + seq
    assert seq % CHUNK == 0
    lp = -(-n_valid // Q_BLOCK) * Q_BLOCK
    n_chunks = n_valid // CHUNK
    topk = min(TOPK_MAX, (N_META + seq) // 4)

    meta = jnp.broadcast_to(meta_tokens[None].astype(x.dtype), (batch, N_META, D_MODEL))
    h = jnp.concatenate([jnp.zeros((batch, FRONT, D_MODEL), x.dtype), meta, x,
                         jnp.zeros((batch, lp - n_valid, D_MODEL), x.dtype)], axis=1)
    h = h.reshape(batch * lp, D_MODEL)

    pos = jnp.clip(jnp.arange(lp, dtype=jnp.int32) - FRONT, 0, N_META + seq - 1).astype(F32)
    tabs = jnp.stack(_rope_lane_tables(pos, ATT_HEAD_DIM) + _rope_lane_tables(pos, IDX_HEAD_DIM))

    p = jax.nn.softmax(hgrn_lb_logits.astype(F32), axis=0)
    cs = jnp.cumsum(p, axis=0)
    lower_bounds = cs - cs[0:1]

    rep = np.zeros((LANES, LANES), np.float32)
    for slot in range(IDX_HEADS_PER_LANE_GROUP):
        rep[np.arange(IDX_HEAD_DIM), slot * IDX_HEAD_DIM + np.arange(IDX_HEAD_DIM)] = 1.0
    rep = jnp.asarray(rep, BF16)
    ltri = jnp.asarray(np.tril(np.ones((Q_BLOCK, Q_BLOCK), np.float32)), BF16)

    pad_cols = jnp.zeros((D_MODEL, IDX_PAD - IDX_RAW), F32)
    for layer in range(depth):
        w = w_in[layer]
        w_p = jnp.concatenate([w[:, :COL_IDX + IDX_RAW], pad_cols, w[:, COL_IDX + IDX_RAW:]], axis=1).astype(BF16)
        ikg = jnp.pad(idx_k_norm_g[layer].astype(F32), (0, LANES - IDX_HEAD_DIM))[None]
        hg, gates, q, k, v, iq, ik, iw = _inproj(h, mix_norm_g[layer][None].astype(F32), w_p, tabs, ikg, rep, lp)
        ya = _hgrn(hg, lower_bounds[layer][None], hgrn_norm_g[layer][None].astype(F32), batch, lp, n_chunks)
        yb = _dsa(q, k, v, iq, ik, iw, ltri, batch, lp, topk, n_valid)
        wr = jnp.concatenate([router_expert_w[layer], router_group_w[layer],
                              jnp.zeros((D_MODEL, LANES - N_EXPERTS - N_GROUPS), F32)], axis=1)
        wr_hi = wr.astype(BF16)
        wr = jnp.concatenate([wr_hi, (wr - wr_hi.astype(F32)).astype(BF16)], axis=1)
        br = jnp.concatenate([router_expert_b[layer], router_group_b[layer],
                              jnp.zeros((LANES - N_EXPERTS - N_GROUPS,), F32)])[None]
        h, xc, route, counts = _merge(ya, yb, gates, h, w_branch_hgrn[layer].astype(BF16),
                                      w_branch_dsa[layer].astype(BF16), w_out[layer].astype(BF16),
                                      ffn_norm_g[layer][None].astype(F32), wr, br)
        by_group = lambda w: (w.reshape(N_GROUPS, EXPERTS_PER_GROUP, D_MODEL, EXPERT_FF).transpose(0, 2, 1, 3)
                              .reshape(N_GROUPS, D_MODEL, EXPERTS_PER_GROUP * EXPERT_FF).astype(BF16))
        wd = w_expert_down[layer].reshape(N_GROUPS, EXPERTS_PER_GROUP * EXPERT_FF, D_MODEL).astype(BF16)
        h = _moe(xc, route, counts, h, by_group(w_expert_gate[layer]), by_group(w_expert_up[layer]), wd)
    return _final_norm(h, final_norm_g[None].astype(F32), batch, lp, seq)
```

```python
import functools

import jax
import jax.numpy as jnp
import numpy as np
from jax import lax
from jax.experimental import pallas as pl
from jax.experimental.pallas import tpu as pltpu

F32 = jnp.float32
BF16 = jnp.bfloat16

D_MODEL = 1024
CHUNK = 64
N_META = 16
FRONT = CHUNK - N_META
ROPE_THETA = 500000.0
NORM_EPS = 1e-6
LANES = 128
SUBLANES = 8

HGRN_HEADS = 4
HGRN_HEAD_DIM = 128
HGRN_WIDTH = HGRN_HEADS * HGRN_HEAD_DIM
SUB = 8
HGRN_MAX_GROUP = 11

ATT_HEADS = 8
ATT_KV_HEADS = 2
ATT_HEAD_DIM = 64
ATT_GROUP = ATT_HEADS // ATT_KV_HEADS
ATT_WIDTH = ATT_HEADS * ATT_HEAD_DIM
ATT_KV_WIDTH = ATT_KV_HEADS * ATT_HEAD_DIM
ATT_ROPE_HALF = ATT_HEAD_DIM // 4 // 2
IDX_HEADS = 8
IDX_HEAD_DIM = 32
IDX_WIDTH = IDX_HEADS * IDX_HEAD_DIM
IDX_ROPE_HALF = IDX_HEAD_DIM // 4 // 2
IDX_HEADS_PER_LANE_GROUP = LANES // IDX_HEAD_DIM
TOPK_MAX = 256
Q_BLOCK = 128
DSA_EXTENT_STEP = 2
FOLD = 64

N_GROUPS = 4
EXPERTS_PER_GROUP = 8
N_EXPERTS = N_GROUPS * EXPERTS_PER_GROUP
EXPERT_FF = D_MODEL // 4
MERGE_TILE = 512
MOE_SLOT_TILE = 512
ROW_DMA_UNROLL = 8

COL_HG = 0
COL_ATT = 4 * HGRN_WIDTH
COL_IDX = COL_ATT + ATT_WIDTH + 2 * ATT_KV_WIDTH
IDX_RAW = IDX_WIDTH + IDX_HEAD_DIM + IDX_HEADS
IDX_PAD = IDX_WIDTH + LANES
COL_GATE = COL_IDX + IDX_PAD
IN_WIDTH_PAD = COL_GATE + 2 * D_MODEL

INT_MIN = np.int32(-2 ** 31)
NEG_FLT_MAX_KEY = np.int32(-2 ** 31 + 0x800000)
NEG_INF = float("-inf")
POS_INF = float("inf")

VMEM_LIMIT = 56 * 1024 * 1024


def _dot(a, b):
    return jnp.dot(a, b, preferred_element_type=F32)


def _dot_nt(a, b):
    return lax.dot_general(a, b, (((1,), (1,)), ((), ())), preferred_element_type=F32)


def _rope(x, c, s1, s2, half):
    return x * c + pltpu.roll(x, half, 1) * s1 + pltpu.roll(x, LANES - half, 1) * s2


def _inproj_kernel(h_ref, g_ref, w_ref, tab_ref, ikg_ref, rep_ref,
                   hg_ref, gate_ref, q_ref, k_ref, v_ref, iq_ref, ik_ref, iw_ref):
    x = h_ref[...]
    xn = (x * lax.rsqrt(jnp.mean(x * x, axis=-1, keepdims=True) + NORM_EPS) * g_ref[...]).astype(BF16)
    hg_ref[...] = _dot(xn, w_ref[:, COL_HG:COL_ATT])
    gate_ref[...] = _dot(xn, w_ref[:, COL_GATE:IN_WIDTH_PAD])
    att = _dot(xn, w_ref[:, COL_ATT:COL_IDX])
    idx = _dot(xn, w_ref[:, COL_IDX:COL_GATE])
    ca, s1a, s2a = tab_ref[0], tab_ref[1], tab_ref[2]
    ci, s1i, s2i = tab_ref[3], tab_ref[4], tab_ref[5]
    scale = ATT_HEAD_DIM ** -0.5
    for m in range(ATT_WIDTH // LANES):
        sl = slice(m * LANES, (m + 1) * LANES)
        q_ref[:, sl] = (_rope(att[:, sl], ca, s1a, s2a, ATT_ROPE_HALF) * scale).astype(BF16)
    k_ref[...] = _rope(att[:, ATT_WIDTH:ATT_WIDTH + LANES], ca, s1a, s2a, ATT_ROPE_HALF).astype(BF16)
    v_ref[...] = att[:, ATT_WIDTH + LANES:].astype(BF16)
    for m in range(IDX_WIDTH // LANES):
        sl = slice(m * LANES, (m + 1) * LANES)
        iq_ref[:, sl] = _rope(idx[:, sl], ci, s1i, s2i, IDX_ROPE_HALF).astype(BF16)
    xk = idx[:, IDX_WIDTH:]
    lane = lax.broadcasted_iota(jnp.int32, xk.shape, 1)
    ms = jnp.sum(jnp.where(lane < IDX_HEAD_DIM, xk * xk, 0.0), axis=-1, keepdims=True) / IDX_HEAD_DIM
    ikn = xk * lax.rsqrt(ms + NORM_EPS) * ikg_ref[...]
    ikr = _rope(ikn, ci, s1i, s2i, IDX_ROPE_HALF).astype(BF16)
    ik_ref[...] = _dot(ikr, rep_ref[...]).astype(BF16)
    iw_ref[...] = xk * (IDX_HEADS ** -0.5 * IDX_HEAD_DIM ** -0.5)


def _inproj(h, g, w, tabs, ikg, rep, lp):
    n = h.shape[0]
    tm = lp // 8
    per = lp // tm
    row = lambda i: (i, 0)
    const2 = lambda i: (0, 0)
    out_shape = (
        jax.ShapeDtypeStruct((n, 4 * HGRN_WIDTH), F32),
        jax.ShapeDtypeStruct((n, 2 * D_MODEL), F32),
        jax.ShapeDtypeStruct((n, ATT_WIDTH), BF16),
        jax.ShapeDtypeStruct((n, LANES), BF16),
        jax.ShapeDtypeStruct((n, LANES), BF16),
        jax.ShapeDtypeStruct((n, IDX_WIDTH), BF16),
        jax.ShapeDtypeStruct((n, LANES), BF16),
        jax.ShapeDtypeStruct((n, LANES), F32),
    )
    return pl.pallas_call(
        _inproj_kernel,
        grid=(n // tm,),
        in_specs=[
            pl.BlockSpec((tm, D_MODEL), row),
            pl.BlockSpec((1, D_MODEL), const2),
            pl.BlockSpec((D_MODEL, IN_WIDTH_PAD), const2),
            pl.BlockSpec((6, tm, LANES), lambda i: (0, i % per, 0)),
            pl.BlockSpec((1, LANES), const2),
            pl.BlockSpec((LANES, LANES), const2),
        ],
        out_specs=tuple(pl.BlockSpec((tm, s.shape[1]), row) for s in out_shape),
        out_shape=out_shape,
        compiler_params=pltpu.CompilerParams(
            dimension_semantics=("arbitrary",), vmem_limit_bytes=VMEM_LIMIT),
        name="inproj",
    )(h, g, w, tabs, ikg, rep)


def _hgrn_kernel(q_ref, f_ref, i_ref, g_ref, lb_ref, ng_ref, o_ref, st_ref, kb_ref, bb_ref, vb_ref,
                 *, n_chunks, group):
    C = CHUNK
    D = HGRN_HEAD_DIM
    st_ref[...] = jnp.zeros_like(st_ref)
    kb_ref[...] = jnp.zeros_like(kb_ref)
    bb_ref[...] = jnp.zeros_like(bb_ref)
    vb_ref[...] = jnp.zeros_like(vb_ref)
    lb = lb_ref[...]
    ng = ng_ref[...]
    R = group * C
    tri_row = lax.broadcasted_iota(jnp.int32, (C, C), 0)
    tri_col = lax.broadcasted_iota(jnp.int32, (C, C), 1)
    tri = (tri_row >= tri_col).astype(F32)
    local_row = lax.broadcasted_iota(jnp.int32, (R, 1), 0)
    sub_pos = local_row & (SUB - 1)

    def chunk(x, u):
        return x[u * C:(u + 1) * C, :]

    def body(it, carry):
        rows = pl.ds(pl.multiple_of(it * R, C), R)
        z = f_ref[rows, :]
        q = q_ref[rows, :]
        v = i_ref[rows, :]
        go = g_ref[rows, :]
        pad = (it * R + local_row) < FRONT
        f = lb + (1.0 - lb) * jax.nn.sigmoid(z)
        logf = jnp.where(pad, 0.0, jnp.log(f))
        k = jnp.where(pad, 0.0, (1.0 - lb) * jax.nn.sigmoid(-z))
        v = jnp.where(pad, 0.0, v)
        b = jnp.concatenate(
            [jnp.dot(tri, chunk(logf, u), preferred_element_type=F32, precision=lax.Precision.HIGHEST)
             for u in range(group)], axis=0)
        b3 = b.reshape(group, C, D)
        b_last = b3[:, C - 1:C, :]
        qd = (q * jnp.exp(b)).astype(BF16)
        kd = (k.reshape(group, C, D) * jnp.exp(b_last - b3)).reshape(R, D).astype(BF16)
        e_last = jnp.exp(b_last)
        upd =[_dot(chunk(v, u).T.astype(BF16), chunk(kd, u)) for u in range(group)]

        st = st_ref[...]
        before = []
        for u in range(group):
            before.append(st.astype(BF16))
            st = st * e_last[u] + upd[u]
        st_ref[...] = st
        o = jnp.concatenate([_dot_nt(chunk(qd, u), before[u]) for u in range(group)], axis=0)

        kb_ref[SUB:, :] = k
        bb_ref[SUB:, :] = b
        vb_ref[SUB:, :] = v
        for j in range(SUB):
            ks = kb_ref[SUB - j:SUB - j + R, :]
            bs = bb_ref[SUB - j:SUB - j + R, :]
            vs = vb_ref[SUB - j:SUB - j + R, :]
            w = jnp.exp(jnp.minimum(b - bs, 0.0))
            a = jnp.sum(q * ks * w, axis=-1, keepdims=True)
            o = o + jnp.where(sub_pos >= j, a, 0.0) * vs

        att = [jnp.zeros((C, C), F32) for _ in range(group)]
        span = C
        while span > SUB:
            half = span // 2
            upper = (local_row & (span - 1)) >= half
            bs3 = b.reshape(R // span, span, D)
            mid = bs3[:, half - 1:half, :]
            qs = jnp.where(upper, (q.reshape(bs3.shape) * jnp.exp(jnp.minimum(bs3 - mid, 0.0))).reshape(R, D), 0.0)
            ks = jnp.where(upper, 0.0, (k.reshape(bs3.shape) * jnp.exp(jnp.minimum(mid - bs3, 0.0))).reshape(R, D))
            qs, ks = qs.astype(BF16), ks.astype(BF16)
            same_block = (tri_row & -span) == (tri_col & -span)
            att = [att[u] + jnp.where(same_block, _dot_nt(chunk(qs, u), chunk(ks, u)), 0.0) for u in range(group)]
            span = half
        o = o + jnp.concatenate([_dot(att[u].astype(BF16), chunk(v, u).astype(BF16)) for u in range(group)], axis=0)

        on = o * lax.rsqrt(jnp.mean(o * o, axis=-1, keepdims=True) + NORM_EPS) * ng
        o_ref[rows, :] = (on * (go * jax.nn.sigmoid(go))).astype(o_ref.dtype)
        return carry

    lax.fori_loop(0, n_chunks // group, body, 0)
    tail = o_ref.shape[0] - n_chunks * C
    if tail:
        o_ref[n_chunks * C:, :] = jnp.zeros((tail, D), o_ref.dtype)


def _hgrn(hg, lb, ng, batch, lp, n_chunks):
    n = hg.shape[0]
    H, D = HGRN_HEADS, HGRN_HEAD_DIM
    group = max(g for g in range(1, HGRN_MAX_GROUP + 1) if n_chunks % g == 0)

    def col(k):
        return pl.BlockSpec((lp, D), lambda b, h: (b, k * H + h))

    vec = pl.BlockSpec((1, D), lambda b, h: (0, h))
    return pl.pallas_call(
        functools.partial(_hgrn_kernel, n_chunks=n_chunks, group=group),
        grid=(batch, H),
        in_specs=[col(0), col(1), col(2), col(3), vec, vec],
        out_specs=pl.BlockSpec((lp, D), lambda b, h: (b, h)),
        out_shape=jax.ShapeDtypeStruct((n, HGRN_WIDTH), BF16),
        scratch_shapes=[
            pltpu.VMEM((D, D), F32),
            pltpu.VMEM((group * CHUNK + SUB, D), F32),
            pltpu.VMEM((group * CHUNK + SUB, D), F32),
            pltpu.VMEM((group * CHUNK + SUB, D), F32),
        ],
        compiler_params=pltpu.CompilerParams(
            dimension_semantics=("arbitrary", "arbitrary"), vmem_limit_bytes=VMEM_LIMIT),
        name="hgrn2",
    )(hg, hg, hg, hg, lb, ng)


def _sortable_to_f32(key):
    bits = jnp.where(key < 0, key ^ jnp.int32(0x7FFFFFFF), key)
    return pltpu.bitcast(bits, F32)


def _fold_rows(x, op):
    rows = x.shape[0]
    if rows > FOLD and rows % FOLD == 0:
        x = op(x.reshape(rows // FOLD, FOLD, x.shape[1]), axis=0)
    return op(x, axis=0, keepdims=True)


def _dsa_kernel(q_ref, iq_ref, iw_ref, k_ref, v_ref, ik_ref, ltri_ref, o_ref,
                sc_ref, vt_ref, iq8_ref, qg_ref, ot_ref, *, topk, n_valid, extents):
    QB = Q_BLOCK
    lp = k_ref.shape[0]
    t = pl.program_id(1)
    kf = float(topk)

    @pl.when(t == 0)
    def _():
        for kb in range(lp // QB):
            sl = slice(kb * QB, (kb + 1) * QB)
            vt_ref[:, sl] = v_ref[sl, :].astype(F32).T.astype(BF16)

    lane = lax.broadcasted_iota(jnp.int32, (QB, LANES), 1)
    iq = iq_ref[...]
    for h in range(IDX_HEADS):
        grp, slot = divmod(h, IDX_HEADS_PER_LANE_GROUP)
        part = iq[:, grp * LANES:(grp + 1) * LANES]
        iq8_ref[h * QB:(h + 1) * QB, :] = jnp.where((lane >> 5) == slot, part, jnp.zeros_like(part))
    wt = iw_ref[...].T
    lane_half = lane >> 6
    for h in range(ATT_HEADS):
        g, r = divmod(h, ATT_GROUP)
        m, p = divmod(h, LANES // ATT_HEAD_DIM)
        qh = q_ref[:, m * LANES:(m + 1) * LANES].astype(F32)
        if p != g:
            qh = pltpu.roll(qh, ATT_HEAD_DIM, 1)
        qg_ref[g, r * QB:(r + 1) * QB, :] = jnp.where(lane_half == g, qh, 0.0).astype(BF16)

    qpos = t * QB + lax.broadcasted_iota(jnp.int32, (1, QB), 1)
    qchunk = qpos >> 6
    n_adm = jnp.minimum((qchunk + 1) * CHUNK, n_valid) - FRONT
    take_all = n_adm <= topk
    ltri = ltri_ref[...]

    def tile_body(n_blocks):
        ke = n_blocks * QB
        keys = slice(0, ke)
        kpos = lax.broadcasted_iota(jnp.int32, (ke, 1), 0)
        adm = ((kpos >> 6) <= qchunk) & (kpos >= FRONT) & (kpos < n_valid)

        ik = ik_ref[keys, :]
        acc = jnp.zeros((ke, QB), F32)
        for h in range(IDX_HEADS):
            rel = jnp.maximum(_dot_nt(ik, iq8_ref[h * QB:(h + 1) * QB, :]), 0.0)
            acc = acc + rel * wt[IDX_HEAD_DIM + h:IDX_HEAD_DIM + h + 1, :]
        acc = jnp.where(acc == 0.0, 0.0, acc)
        sc_ref[keys, :] = jnp.where(adm, acc, NEG_INF)

        def count_ge(cand):
            return _fold_rows(jnp.where(sc_ref[keys, :] >= cand, 1.0, 0.0), jnp.sum)

        def counts(cand):
            x = sc_ref[keys, :]
            return (_fold_rows(jnp.where(x >= cand, 1.0, 0.0), jnp.sum),
                    _fold_rows(jnp.where(x > cand, 1.0, 0.0), jnp.sum))

        def search(i, tkey):
            cand = tkey + lax.shift_left(jnp.int32(1), 31 - i)
            return jnp.where(count_ge(_sortable_to_f32(cand)) >= kf, cand, tkey)

        tkey = lax.fori_loop(0, 32, search, jnp.full((1, QB), INT_MIN, jnp.int32))
        thr = _sortable_to_f32(jnp.where(take_all, NEG_FLT_MAX_KEY, tkey))
        c_ge, c_gt = counts(thr)

        def unsettled(c_ge, c_gt):
            live = jnp.logical_not(take_all)
            return (c_gt >= kf) & live, (c_ge < kf) & live

        def cond(carry):
            up, dn = unsettled(carry[1], carry[2])
            return jnp.max(jnp.where(up | dn, 1.0, 0.0)) > 0.0

        def fix(carry):
            thr, c_ge, c_gt = carry
            up, dn = unsettled(c_ge, c_gt)
            x = sc_ref[keys, :]
            above = _fold_rows(jnp.where(x > thr, x, POS_INF), jnp.min)
            below = _fold_rows(jnp.where(x < thr, x, NEG_INF), jnp.max)
            thr = jnp.where(up, above, jnp.where(dn, below, thr))
            return (thr,) + counts(thr)

        thr, c_ge, c_gt = lax.while_loop(cond, fix, (thr, c_ge, c_gt))

        need = kf - c_gt
        seen = jnp.zeros((1, QB), F32)
        for kb in range(n_blocks):
            blk = slice(kb * QB, (kb + 1) * QB)
            x = sc_ref[blk, :]
            eq = x == thr
            pre = _dot(ltri, jnp.where(eq, 1.0, 0.0).astype(BF16))
            take = (x > thr) | (eq & (pre + seen <= need))
            sc_ref[blk, :] = jnp.where(take, 0.0, NEG_INF)
            seen = seen + pre[QB - 1:QB, :]

        bias = sc_ref[keys, :]
        bias = jnp.concatenate([bias] * ATT_GROUP, axis=1)
        kk = k_ref[keys, :]
        for g in range(ATT_KV_HEADS):
            s = _dot_nt(kk, qg_ref[g]) + bias
            e = jnp.exp(s - _fold_rows(s, jnp.max))
            vt = vt_ref[g * ATT_HEAD_DIM:(g + 1) * ATT_HEAD_DIM, keys]
            pv = _dot(vt, e.astype(BF16)) / _fold_rows(e, jnp.sum)
            for r in range(ATT_GROUP):
                h = g * ATT_GROUP + r
                ot_ref[h * ATT_HEAD_DIM:(h + 1) * ATT_HEAD_DIM, :] = pv[:, r * QB:(r + 1) * QB]
        o_ref[...] = ot_ref[...].T.astype(o_ref.dtype)

    lo = 0
    for n_blocks in extents:
        pl.when((t >= lo) & (t < n_blocks))(functools.partial(tile_body, n_blocks))
        lo = n_blocks


def _dsa(q, k, v, iq, ik, iw, ltri, batch, lp, topk, n_valid):
    n = q.shape[0]
    nq = lp // Q_BLOCK
    extents = tuple(sorted({nq - DSA_EXTENT_STEP * i for i in range(-(-nq // DSA_EXTENT_STEP))}))
    qblk = lambda b, i: (b * nq + i, 0)
    bat = lambda b, i: (b, 0)
    return pl.pallas_call(
        functools.partial(_dsa_kernel, topk=topk, n_valid=n_valid, extents=extents),
        grid=(batch, nq),
        in_specs=[
            pl.BlockSpec((Q_BLOCK, ATT_WIDTH), qblk),
            pl.BlockSpec((Q_BLOCK, IDX_WIDTH), qblk),
            pl.BlockSpec((Q_BLOCK, LANES), qblk),
            pl.BlockSpec((lp, LANES), bat),
            pl.BlockSpec((lp, LANES), bat),
            pl.BlockSpec((lp, LANES), bat),
            pl.BlockSpec((Q_BLOCK, Q_BLOCK), lambda b, i: (0, 0)),
        ],
        out_specs=pl.BlockSpec((Q_BLOCK, ATT_WIDTH), qblk),
        out_shape=jax.ShapeDtypeStruct((n, ATT_WIDTH), BF16),
        scratch_shapes=[
            pltpu.VMEM((lp, Q_BLOCK), F32),
            pltpu.VMEM((LANES, lp), BF16),
            pltpu.VMEM((IDX_HEADS * Q_BLOCK, LANES), BF16),
            pltpu.VMEM((ATT_KV_HEADS, ATT_GROUP * Q_BLOCK, LANES), BF16),
            pltpu.VMEM((ATT_WIDTH, Q_BLOCK), F32),
        ],
        compiler_params=pltpu.CompilerParams(
            dimension_semantics=("arbitrary", "arbitrary"), vmem_limit_bytes=VMEM_LIMIT),
        name="dsa",
    )(q, iq, iw, k, v, ik, ltri)


def _merge_kernel(ya_ref, yb_ref, gate_ref, h_ref, wa_ref, wb_ref, wo_ref, fg_ref, wr_ref, br_ref, ltri_ref,
                  ho_ref, xc_ref, route_ref, cnt_ref):
    @pl.when(pl.program_id(0) == 0)
    def _():
        cnt_ref[...] = jnp.zeros_like(cnt_ref)

    gate = gate_ref[...]
    pa = _dot(ya_ref[...], wa_ref[...])
    pb = _dot(yb_ref[...], wb_ref[...])
    merged = jax.nn.sigmoid(gate[:, :D_MODEL]) * pa + jax.nn.sigmoid(gate[:, D_MODEL:]) * pb
    h = h_ref[...] + _dot(merged.astype(BF16), wo_ref[...])
    ho_ref[...] = h
    xn = h * lax.rsqrt(jnp.mean(h * h, axis=-1, keepdims=True) + NORM_EPS) * fg_ref[...]
    xh = xn.astype(BF16)
    xc_ref[:, :D_MODEL] = xh.astype(F32)

    xl = (xn - xh.astype(F32)).astype(BF16)
    r = _dot(xh, wr_ref[...])
    lg = r[:, :LANES] + r[:, LANES:] + _dot(xl, wr_ref[:, :LANES]) + br_ref[...]
    lane = lax.broadcasted_iota(jnp.int32, lg.shape, 1)
    lane_f = lane.astype(F32)
    big = float(LANES)
    is_g = (lane >= N_EXPERTS) & (lane < N_EXPERTS + N_GROUPS)
    gl = jnp.where(is_g, lg, NEG_INF)
    gmax = jnp.max(gl, axis=-1, keepdims=True)
    pg_top = 1.0 / jnp.sum(jnp.exp(gl - gmax), axis=-1, keepdims=True)
    g_lane = jnp.min(jnp.where(gl == gmax, lane_f, big), axis=-1, keepdims=True)
    e_lo = (g_lane - N_EXPERTS) * EXPERTS_PER_GROUP
    in_grp = (lane_f >= e_lo) & (lane_f < e_lo + EXPERTS_PER_GROUP)
    el = jnp.where(in_grp, lg, NEG_INF)
    ee = jnp.exp(el - jnp.max(el, axis=-1, keepdims=True))
    pe = ee / jnp.sum(ee, axis=-1, keepdims=True)
    pe = jnp.where(in_grp, pe, -1.0)
    p1 = jnp.max(pe, axis=-1, keepdims=True)
    i1 = jnp.min(jnp.where(pe == p1, lane_f, big), axis=-1, keepdims=True)
    pe2 = jnp.where(lane_f == i1, -1.0, pe)
    p2 = jnp.max(pe2, axis=-1, keepdims=True)
    i2 = jnp.min(jnp.where(pe2 == p2, lane_f, big), axis=-1, keepdims=True)
    tot = p1 + p2
    xc_ref[:, D_MODEL:] = jnp.where(lane_f == i1, p1 / tot * pg_top,
                                    jnp.where(lane_f == i2, p2 / tot * pg_top, 0.0))

    grp = g_lane - N_EXPERTS
    onehot = jnp.where(lane_f == grp, 1.0, 0.0)
    incl = _dot(ltri_ref[...], onehot.astype(BF16))
    seen = cnt_ref[0:1, :]
    rank = jnp.sum(onehot * (incl - 1.0 + seen), axis=-1, keepdims=True)
    route_ref[...] = jnp.where(lane == 0, grp, jnp.where(lane == 1, rank, 0.0))
    cnt_ref[...] = jnp.broadcast_to(seen + incl[incl.shape[0] - 1:, :], cnt_ref.shape)


def _merge(ya, yb, gates, h, wa, wb, wo, fg, wr, br):
    n = h.shape[0]
    tm = min(MERGE_TILE, n)
    ltri = jnp.asarray(np.tril(np.ones((tm, tm), np.float32)), BF16)
    row = lambda i: (i, 0)
    const = lambda i: (0, 0)
    return pl.pallas_call(
        _merge_kernel,
        grid=(n // tm,),
        in_specs=[
            pl.BlockSpec((tm, HGRN_WIDTH), row),
            pl.BlockSpec((tm, ATT_WIDTH), row),
            pl.BlockSpec((tm, 2 * D_MODEL), row),
            pl.BlockSpec((tm, D_MODEL), row),
            pl.BlockSpec((HGRN_WIDTH, D_MODEL), const),
            pl.BlockSpec((ATT_WIDTH, D_MODEL), const),
            pl.BlockSpec((D_MODEL, D_MODEL), const),
            pl.BlockSpec((1, D_MODEL), const),
            pl.BlockSpec((D_MODEL, 2 * LANES), const),
            pl.BlockSpec((1, LANES), const),
            pl.BlockSpec((tm, tm), const),
        ],
        out_specs=(pl.BlockSpec((tm, D_MODEL), row), pl.BlockSpec((tm, D_MODEL + LANES), row),
                   pl.BlockSpec((tm, LANES), row), pl.BlockSpec((SUBLANES, LANES), const)),
        out_shape=(jax.ShapeDtypeStruct((n, D_MODEL), F32), jax.ShapeDtypeStruct((n, D_MODEL + LANES), F32),
                   jax.ShapeDtypeStruct((n, LANES), F32), jax.ShapeDtypeStruct((SUBLANES, LANES), F32)),
        compiler_params=pltpu.CompilerParams(
            dimension_semantics=("arbitrary",), vmem_limit_bytes=VMEM_LIMIT),
        name="merge_router",
    )(ya, yb, gates, h, wa, wb, wo, fg, wr, br, ltri)


def _row_copies(n_rows, make_copy, whole_copy):
    def issue(r, c):
        make_copy(r).start()
        return c

    lax.fori_loop(0, n_rows, issue, 0, unroll=ROW_DMA_UNROLL)
    whole_copy.wait()


def _dispatch_kernel(dest_ref, x_ref, zeros_ref, xs_ref, sem):
    del zeros_ref
    tm = x_ref.shape[0]

    def row_copy(r):
        return pltpu.make_async_copy(x_ref.at[pl.ds(r, 1)], xs_ref.at[pl.ds(dest_ref[0, 0, r], 1)], sem)

    _row_copies(tm, row_copy, pltpu.make_async_copy(x_ref, xs_ref.at[pl.ds(0, tm)], sem))


def _experts_kernel(tile_group_ref, tile_on_ref, xs_ref, wg_ref, wu_ref, wd_ref, ys_ref):
    t = pl.program_id(0)
    tm = xs_ref.shape[0]

    @pl.when(tile_on_ref[t] == 1)
    def _():
        x = xs_ref[:, :D_MODEL].astype(BF16)
        c = xs_ref[:, D_MODEL:]
        lane = lax.broadcasted_iota(jnp.int32, c.shape, 1)
        first = tile_group_ref[t] * EXPERTS_PER_GROUP
        scale = jnp.concatenate(
            [jnp.broadcast_to(jnp.sum(jnp.where(lane == first + e, c, 0.0), axis=-1, keepdims=True), (tm, EXPERT_FF))
             for e in range(EXPERTS_PER_GROUP)], axis=1)
        a = _dot(x, wg_ref[0])
        u = _dot(x, wu_ref[0])
        act = (a * jax.nn.sigmoid(a)) * u * scale
        ys_ref[...] = _dot(act.astype(BF16), wd_ref[0])

    @pl.when(tile_on_ref[t] == 0)
    def _():
        ys_ref[...] = jnp.zeros_like(ys_ref)


def _combine_kernel(dest_ref, h_ref, ys_ref, o_ref, buf_ref, sem):
    tm = h_ref.shape[0]

    def row_copy(r):
        return pltpu.make_async_copy(ys_ref.at[pl.ds(dest_ref[0, 0, r], 1)], buf_ref.at[pl.ds(r, 1)], sem)

    _row_copies(tm, row_copy, pltpu.make_async_copy(ys_ref.at[pl.ds(0, tm)], buf_ref, sem))
    o_ref[...] = h_ref[...] + buf_ref[...]


def _expert_weights_kernel(g_ref, u_ref, d_ref, go_ref, uo_ref, do_ref):
    go_ref[0] = g_ref[0].astype(BF16)
    uo_ref[0] = u_ref[0].astype(BF16)
    do_ref[0] = d_ref[0].astype(BF16)


def _expert_weights(w_gate, w_up, w_down):
    per = EXPERTS_PER_GROUP
    src = lambda shape: pl.BlockSpec((1,) + shape, lambda e: (e, 0, 0))
    side = pl.BlockSpec((1, D_MODEL, EXPERT_FF), lambda e: (e // per, 0, e % per))
    stack = pl.BlockSpec((1, EXPERT_FF, D_MODEL), lambda e: (e // per, e % per, 0))
    return pl.pallas_call(
        _expert_weights_kernel,
        grid=(N_EXPERTS,),
        in_specs=[src((D_MODEL, EXPERT_FF)), src((D_MODEL, EXPERT_FF)), src((EXPERT_FF, D_MODEL))],
        out_specs=(side, side, stack),
        out_shape=(jax.ShapeDtypeStruct((N_GROUPS, D_MODEL, per * EXPERT_FF), BF16),
                   jax.ShapeDtypeStruct((N_GROUPS, D_MODEL, per * EXPERT_FF), BF16),
                   jax.ShapeDtypeStruct((N_GROUPS, per * EXPERT_FF, D_MODEL), BF16)),
        compiler_params=pltpu.CompilerParams(
            dimension_semantics=("arbitrary",), vmem_limit_bytes=VMEM_LIMIT),
        name="expert_weights",
    )(w_gate, w_up, w_down)


def _moe(xc, route, counts, h, wg, wu, wd):
    n = h.shape[0]
    tm = min(MERGE_TILE, n)
    ts = min(MOE_SLOT_TILE, n)
    n_tiles = n // ts + N_GROUPS
    n_slots = n_tiles * ts

    grp = route[:, 0].astype(jnp.int32)
    rank = route[:, 1].astype(jnp.int32)
    cnt = counts[0, :N_GROUPS].astype(jnp.int32)
    padded = (cnt + ts - 1) // ts * ts
    ends = jnp.cumsum(padded)
    starts = ends - padded
    dest = (starts[grp] + rank).reshape(n // tm, 1, tm)
    tile_start = jnp.arange(n_tiles, dtype=jnp.int32) * ts
    tile_group = jnp.minimum(jnp.sum(tile_start[:, None] >= ends[None, :], axis=1), N_GROUPS - 1).astype(jnp.int32)
    tile_on = (tile_start < ends[N_GROUPS - 1]).astype(jnp.int32)

    tok = lambda i: (i, 0)
    smem_idx = pl.BlockSpec((1, 1, tm), lambda i: (i, 0, 0), memory_space=pltpu.SMEM)
    hbm = pl.BlockSpec(memory_space=pl.ANY)
    xs = pl.pallas_call(
        _dispatch_kernel,
        grid=(n // tm,),
        in_specs=[smem_idx, pl.BlockSpec((tm, D_MODEL + LANES), tok), hbm],
        out_specs=hbm,
        out_shape=jax.ShapeDtypeStruct((n_slots, D_MODEL + LANES), F32),
        scratch_shapes=[pltpu.SemaphoreType.DMA(())],
        input_output_aliases={2: 0},
        compiler_params=pltpu.CompilerParams(
            dimension_semantics=("arbitrary",), vmem_limit_bytes=VMEM_LIMIT),
        name="moe_dispatch",
    )(dest, xc, jnp.zeros((n_slots, D_MODEL + LANES), F32))

    wspec = lambda shape: pl.BlockSpec((1,) + shape, lambda t, tg, on: (tg[t], 0, 0))
    ys = pl.pallas_call(
        _experts_kernel,
        grid_spec=pltpu.PrefetchScalarGridSpec(
            num_scalar_prefetch=2,
            grid=(n_tiles,),
            in_specs=[pl.BlockSpec((ts, D_MODEL + LANES), lambda t, tg, on: (t, 0)),
                      wspec((D_MODEL, EXPERTS_PER_GROUP * EXPERT_FF)),
                      wspec((D_MODEL, EXPERTS_PER_GROUP * EXPERT_FF)),
                      wspec((EXPERTS_PER_GROUP * EXPERT_FF, D_MODEL))],
            out_specs=pl.BlockSpec((ts, D_MODEL), lambda t, tg, on: (t, 0)),
        ),
        out_shape=jax.ShapeDtypeStruct((n_slots, D_MODEL), F32),
        compiler_params=pltpu.CompilerParams(
            dimension_semantics=("arbitrary",), vmem_limit_bytes=VMEM_LIMIT),
        name="moe_experts",
    )(tile_group, tile_on, xs, wg, wu, wd)

    return pl.pallas_call(
        _combine_kernel,
        grid=(n // tm,),
        in_specs=[smem_idx, pl.BlockSpec((tm, D_MODEL), tok), hbm],
        out_specs=pl.BlockSpec((tm, D_MODEL), tok),
        out_shape=jax.ShapeDtypeStruct((n, D_MODEL), F32),
        scratch_shapes=[pltpu.VMEM((tm, D_MODEL), F32), pltpu.SemaphoreType.DMA(())],
        compiler_params=pltpu.CompilerParams(
            dimension_semantics=("arbitrary",), vmem_limit_bytes=VMEM_LIMIT),
        name="moe_combine",
    )(dest, h, ys)


def _final_kernel(h_ref, g_ref, o_ref, *, seq):
    step = 256 if seq % 256 == 0 else CHUNK
    for r in range(0, seq, step):
        x = h_ref[CHUNK + r:CHUNK + r + step, :]
        o_ref[0, r:r + step, :] = x * lax.rsqrt(jnp.mean(x * x, axis=-1, keepdims=True) + NORM_EPS) * g_ref[...]


def _final_norm(h, g, batch, lp, seq):
    return pl.pallas_call(
        functools.partial(_final_kernel, seq=seq),
        grid=(batch,),
        in_specs=[pl.BlockSpec((lp, D_MODEL), lambda b: (b, 0)), pl.BlockSpec((1, D_MODEL), lambda b: (0, 0))],
        out_specs=pl.BlockSpec((1, seq, D_MODEL), lambda b: (b, 0, 0)),
        out_shape=jax.ShapeDtypeStruct((batch, seq, D_MODEL), F32),
        compiler_params=pltpu.CompilerParams(
            dimension_semantics=("arbitrary",), vmem_limit_bytes=VMEM_LIMIT),
        name="final_norm",
    )(h, g)


def _rope_lane_tables(pos, head_dim):
    rot = head_dim // 4
    half = rot // 2
    inv = ROPE_THETA ** (-jnp.arange(0, rot, 2, dtype=F32) / rot)
    ang = pos[:, None] * inv[None, :]
    cos, sin = jnp.cos(ang), jnp.sin(ang)
    jj = np.arange(LANES) % head_dim
    first = jnp.asarray(jj < half)
    second = jnp.asarray((jj >= half) & (jj < rot))
    fidx = jnp.asarray(np.where(jj < half, jj, np.where(jj < rot, jj - half, 0)))
    cl, sl = cos[:, fidx], sin[:, fidx]
    c = jnp.where(first | second, cl, 1.0)
    s1 = jnp.where(second, sl, 0.0)
    s2 = jnp.where(first, -sl, 0.0)
    return c, s1, s2


def kernel(x, meta_tokens, mix_norm_g, w_in, hgrn_lb_logits, hgrn_norm_g, idx_k_norm_g, w_branch_hgrn,
           w_branch_dsa, w_out, ffn_norm_g, router_group_w, router_group_b, router_expert_w,
           router_expert_b, w_expert_gate, w_expert_up, w_expert_down, final_norm_g):
    batch, seq, _ = x.shape
    depth = w_in.shape[0]
    n_valid = CHUNK + seq
    assert seq % CHUNK == 0
    lp = -(-n_valid // Q_BLOCK) * Q_BLOCK
    n_chunks = n_valid // CHUNK
    topk = min(TOPK_MAX, (N_META + seq) // 4)

    meta = jnp.broadcast_to(meta_tokens[None].astype(x.dtype), (batch, N_META, D_MODEL))
    h = jnp.concatenate([jnp.zeros((batch, FRONT, D_MODEL), x.dtype), meta, x,
                         jnp.zeros((batch, lp - n_valid, D_MODEL), x.dtype)], axis=1)
    h = h.reshape(batch * lp, D_MODEL)

    pos = jnp.clip(jnp.arange(lp, dtype=jnp.int32) - FRONT, 0, N_META + seq - 1).astype(F32)
    tabs = jnp.stack(_rope_lane_tables(pos, ATT_HEAD_DIM) + _rope_lane_tables(pos, IDX_HEAD_DIM))

    p = jax.nn.softmax(hgrn_lb_logits.astype(F32), axis=0)
    cs = jnp.cumsum(p, axis=0)
    lower_bounds = cs - cs[0:1]

    rep = np.zeros((LANES, LANES), np.float32)
    for slot in range(IDX_HEADS_PER_LANE_GROUP):
        rep[np.arange(IDX_HEAD_DIM), slot * IDX_HEAD_DIM + np.arange(IDX_HEAD_DIM)] = 1.0
    rep = jnp.asarray(rep, BF16)
    ltri = jnp.asarray(np.tril(np.ones((Q_BLOCK, Q_BLOCK), np.float32)), BF16)

    pad_cols = jnp.zeros((D_MODEL, IDX_PAD - IDX_RAW), F32)
    for layer in range(depth):
        w = w_in[layer]
        w_p = jnp.concatenate([w[:, :COL_IDX + IDX_RAW], pad_cols, w[:, COL_IDX + IDX_RAW:]], axis=1).astype(BF16)
        ikg = jnp.pad(idx_k_norm_g[layer].astype(F32), (0, LANES - IDX_HEAD_DIM))[None]
        hg, gates, q, k, v, iq, ik, iw = _inproj(h, mix_norm_g[layer][None].astype(F32), w_p, tabs, ikg, rep, lp)
        ya = _hgrn(hg, lower_bounds[layer][None], hgrn_norm_g[layer][None].astype(F32), batch, lp, n_chunks)
        yb = _dsa(q, k, v, iq, ik, iw, ltri, batch, lp, topk, n_valid)
        wr = jnp.concatenate([router_expert_w[layer], router_group_w[layer],
                              jnp.zeros((D_MODEL, LANES - N_EXPERTS - N_GROUPS), F32)], axis=1)
        wr_hi = wr.astype(BF16)
        wr = jnp.concatenate([wr_hi, (wr - wr_hi.astype(F32)).astype(BF16)], axis=1)
        br = jnp.concatenate([router_expert_b[layer], router_group_b[layer],
                              jnp.zeros((LANES - N_EXPERTS - N_GROUPS,), F32)])[None]
        h, xc, route, counts = _merge(ya, yb, gates, h, w_branch_hgrn[layer].astype(BF16),
                                      w_branch_dsa[layer].astype(BF16), w_out[layer].astype(BF16),
                                      ffn_norm_g[layer][None].astype(F32), wr, br)
        wg, wu, wd = _expert_weights(w_expert_gate[layer], w_expert_up[layer], w_expert_down[layer])
        h = _moe(xc, route, counts, h, wg, wu, wd)
    return _final_norm(h, final_norm_g[None].astype(F32), batch, lp, seq)
```

```python
import functools

import jax
import jax.numpy as jnp
import numpy as np
from jax import lax
from jax.experimental import pallas as pl
from jax.experimental.pallas import tpu as pltpu

F32 = jnp.float32
BF16 = jnp.bfloat16

D_MODEL = 1024
CHUNK = 64
N_META = 16
FRONT = CHUNK - N_META
ROPE_THETA = 500000.0
NORM_EPS = 1e-6
LANES = 128
SUBLANES = 8

HGRN_HEADS = 4
HGRN_HEAD_DIM = 128
HGRN_WIDTH = HGRN_HEADS * HGRN_HEAD_DIM
SUB = 8
HGRN_MAX_GROUP = 11

ATT_HEADS = 8
ATT_KV_HEADS = 2
ATT_HEAD_DIM = 64
ATT_GROUP = ATT_HEADS // ATT_KV_HEADS
ATT_WIDTH = ATT_HEADS * ATT_HEAD_DIM
ATT_KV_WIDTH = ATT_KV_HEADS * ATT_HEAD_DIM
ATT_ROPE_HALF = ATT_HEAD_DIM // 4 // 2
IDX_HEADS = 8
IDX_HEAD_DIM = 32
IDX_WIDTH = IDX_HEADS * IDX_HEAD_DIM
IDX_ROPE_HALF = IDX_HEAD_DIM // 4 // 2
IDX_HEADS_PER_LANE_GROUP = LANES // IDX_HEAD_DIM
TOPK_MAX = 256
Q_BLOCK = 128
DSA_EXTENT_STEP = 2
FOLD = 64
ONES_ROWS = 16

N_GROUPS = 4
EXPERTS_PER_GROUP = 8
N_EXPERTS = N_GROUPS * EXPERTS_PER_GROUP
EXPERT_FF = D_MODEL // 4
MERGE_TILE = 512
MOE_SLOT_TILE = 512
ROW_DMA_UNROLL = 8

COL_HG = 0
COL_ATT = 4 * HGRN_WIDTH
COL_IDX = COL_ATT + ATT_WIDTH + 2 * ATT_KV_WIDTH
IDX_RAW = IDX_WIDTH + IDX_HEAD_DIM + IDX_HEADS
IDX_PAD = IDX_WIDTH + LANES
COL_GATE = COL_IDX + IDX_PAD
IN_WIDTH_PAD = COL_GATE + 2 * D_MODEL

INT_MIN = np.int32(-2 ** 31)
NEG_FLT_MAX_KEY = np.int32(-2 ** 31 + 0x800000)
NEG_INF = float("-inf")
POS_INF = float("inf")

VMEM_LIMIT = 56 * 1024 * 1024


def _dot(a, b):
    return jnp.dot(a, b, preferred_element_type=F32)


def _dot_nt(a, b):
    return lax.dot_general(a, b, (((1,), (1,)), ((), ())), preferred_element_type=F32)


def _rope(x, c, s1, s2, half):
    return x * c + pltpu.roll(x, half, 1) * s1 + pltpu.roll(x, LANES - half, 1) * s2


def _inproj_kernel(h_ref, g_ref, w_ref, tab_ref, ikg_ref, rep_ref,
                   hg_ref, gate_ref, q_ref, k_ref, v_ref, iq_ref, ik_ref, iw_ref):
    x = h_ref[...]
    xn = (x * lax.rsqrt(jnp.mean(x * x, axis=-1, keepdims=True) + NORM_EPS) * g_ref[...]).astype(BF16)
    hg_ref[...] = _dot(xn, w_ref[:, COL_HG:COL_ATT])
    gate_ref[...] = _dot(xn, w_ref[:, COL_GATE:IN_WIDTH_PAD])
    att = _dot(xn, w_ref[:, COL_ATT:COL_IDX])
    idx = _dot(xn, w_ref[:, COL_IDX:COL_GATE])
    ca, s1a, s2a = tab_ref[0], tab_ref[1], tab_ref[2]
    ci, s1i, s2i = tab_ref[3], tab_ref[4], tab_ref[5]
    scale = ATT_HEAD_DIM ** -0.5
    for m in range(ATT_WIDTH // LANES):
        sl = slice(m * LANES, (m + 1) * LANES)
        q_ref[:, sl] = (_rope(att[:, sl], ca, s1a, s2a, ATT_ROPE_HALF) * scale).astype(BF16)
    k_ref[...] = _rope(att[:, ATT_WIDTH:ATT_WIDTH + LANES], ca, s1a, s2a, ATT_ROPE_HALF).astype(BF16)
    v_ref[...] = att[:, ATT_WIDTH + LANES:].astype(BF16)
    for m in range(IDX_WIDTH // LANES):
        sl = slice(m * LANES, (m + 1) * LANES)
        iq_ref[:, sl] = _rope(idx[:, sl], ci, s1i, s2i, IDX_ROPE_HALF).astype(BF16)
    xk = idx[:, IDX_WIDTH:]
    lane = lax.broadcasted_iota(jnp.int32, xk.shape, 1)
    ms = jnp.sum(jnp.where(lane < IDX_HEAD_DIM, xk * xk, 0.0), axis=-1, keepdims=True) / IDX_HEAD_DIM
    ikn = xk * lax.rsqrt(ms + NORM_EPS) * ikg_ref[...]
    ikr = _rope(ikn, ci, s1i, s2i, IDX_ROPE_HALF).astype(BF16)
    ik_ref[...] = _dot(ikr, rep_ref[...]).astype(BF16)
    iw_ref[...] = xk * (IDX_HEADS ** -0.5 * IDX_HEAD_DIM ** -0.5)


def _inproj(h, g, w, tabs, ikg, rep, lp):
    n = h.shape[0]
    tm = lp // 8
    per = lp // tm
    row = lambda i: (i, 0)
    const2 = lambda i: (0, 0)
    out_shape = (
        jax.ShapeDtypeStruct((n, 4 * HGRN_WIDTH), F32),
        jax.ShapeDtypeStruct((n, 2 * D_MODEL), F32),
        jax.ShapeDtypeStruct((n, ATT_WIDTH), BF16),
        jax.ShapeDtypeStruct((n, LANES), BF16),
        jax.ShapeDtypeStruct((n, LANES), BF16),
        jax.ShapeDtypeStruct((n, IDX_WIDTH), BF16),
        jax.ShapeDtypeStruct((n, LANES), BF16),
        jax.ShapeDtypeStruct((n, LANES), F32),
    )
    return pl.pallas_call(
        _inproj_kernel,
        grid=(n // tm,),
        in_specs=[
            pl.BlockSpec((tm, D_MODEL), row),
            pl.BlockSpec((1, D_MODEL), const2),
            pl.BlockSpec((D_MODEL, IN_WIDTH_PAD), const2),
            pl.BlockSpec((6, tm, LANES), lambda i: (0, i % per, 0)),
            pl.BlockSpec((1, LANES), const2),
            pl.BlockSpec((LANES, LANES), const2),
        ],
        out_specs=tuple(pl.BlockSpec((tm, s.shape[1]), row) for s in out_shape),
        out_shape=out_shape,
        compiler_params=pltpu.CompilerParams(
            dimension_semantics=("arbitrary",), vmem_limit_bytes=VMEM_LIMIT),
        name="inproj",
    )(h, g, w, tabs, ikg, rep)


def _hgrn_kernel(q_ref, f_ref, i_ref, g_ref, lb_ref, ng_ref, o_ref, st_ref, kb_ref, bb_ref, vb_ref,
                 *, n_chunks, group):
    C = CHUNK
    D = HGRN_HEAD_DIM
    st_ref[...] = jnp.zeros_like(st_ref)
    kb_ref[...] = jnp.zeros_like(kb_ref)
    bb_ref[...] = jnp.zeros_like(bb_ref)
    vb_ref[...] = jnp.zeros_like(vb_ref)
    lb = lb_ref[...]
    ng = ng_ref[...]
    R = group * C
    tri_row = lax.broadcasted_iota(jnp.int32, (C, C), 0)
    tri_col = lax.broadcasted_iota(jnp.int32, (C, C), 1)
    tri = (tri_row >= tri_col).astype(F32)
    local_row = lax.broadcasted_iota(jnp.int32, (R, 1), 0)
    sub_pos = local_row & (SUB - 1)

    def chunk(x, u):
        return x[u * C:(u + 1) * C, :]

    def body(it, carry):
        rows = pl.ds(pl.multiple_of(it * R, C), R)
        z = f_ref[rows, :]
        q = q_ref[rows, :]
        v = i_ref[rows, :]
        go = g_ref[rows, :]
        pad = (it * R + local_row) < FRONT
        f = lb + (1.0 - lb) * jax.nn.sigmoid(z)
        logf = jnp.where(pad, 0.0, jnp.log(f))
        k = jnp.where(pad, 0.0, (1.0 - lb) * jax.nn.sigmoid(-z))
        v = jnp.where(pad, 0.0, v)
        b = jnp.concatenate(
            [jnp.dot(tri, chunk(logf, u), preferred_element_type=F32, precision=lax.Precision.HIGHEST)
             for u in range(group)], axis=0)
        b3 = b.reshape(group, C, D)
        b_last = b3[:, C - 1:C, :]
        qd = (q * jnp.exp(b)).astype(BF16)
        kd = (k.reshape(group, C, D) * jnp.exp(b_last - b3)).reshape(R, D).astype(BF16)
        e_last = jnp.exp(b_last)
        upd =[_dot(chunk(v, u).T.astype(BF16), chunk(kd, u)) for u in range(group)]

        st = st_ref[...]
        before = []
        for u in range(group):
            before.append(st.astype(BF16))
            st = st * e_last[u] + upd[u]
        st_ref[...] = st
        o = jnp.concatenate([_dot_nt(chunk(qd, u), before[u]) for u in range(group)], axis=0)

        kb_ref[SUB:, :] = k
        bb_ref[SUB:, :] = b
        vb_ref[SUB:, :] = v
        for j in range(SUB):
            ks = kb_ref[SUB - j:SUB - j + R, :]
            bs = bb_ref[SUB - j:SUB - j + R, :]
            vs = vb_ref[SUB - j:SUB - j + R, :]
            w = jnp.exp(jnp.minimum(b - bs, 0.0))
            a = jnp.sum(q * ks * w, axis=-1, keepdims=True)
            o = o + jnp.where(sub_pos >= j, a, 0.0) * vs

        att = [jnp.zeros((C, C), F32) for _ in range(group)]
        span = C
        while span > SUB:
            half = span // 2
            upper = (local_row & (span - 1)) >= half
            bs3 = b.reshape(R // span, span, D)
            mid = bs3[:, half - 1:half, :]
            qs = jnp.where(upper, (q.reshape(bs3.shape) * jnp.exp(jnp.minimum(bs3 - mid, 0.0))).reshape(R, D), 0.0)
            ks = jnp.where(upper, 0.0, (k.reshape(bs3.shape) * jnp.exp(jnp.minimum(mid - bs3, 0.0))).reshape(R, D))
            qs, ks = qs.astype(BF16), ks.astype(BF16)
            same_block = (tri_row & -span) == (tri_col & -span)
            att = [att[u] + jnp.where(same_block, _dot_nt(chunk(qs, u), chunk(ks, u)), 0.0) for u in range(group)]
            span = half
        o = o + jnp.concatenate([_dot(att[u].astype(BF16), chunk(v, u).astype(BF16)) for u in range(group)], axis=0)

        on = o * lax.rsqrt(jnp.mean(o * o, axis=-1, keepdims=True) + NORM_EPS) * ng
        o_ref[rows, :] = (on * (go * jax.nn.sigmoid(go))).astype(o_ref.dtype)
        return carry

    lax.fori_loop(0, n_chunks // group, body, 0)
    tail = o_ref.shape[0] - n_chunks * C
    if tail:
        o_ref[n_chunks * C:, :] = jnp.zeros((tail, D), o_ref.dtype)


def _hgrn(hg, lb, ng, batch, lp, n_chunks):
    n = hg.shape[0]
    H, D = HGRN_HEADS, HGRN_HEAD_DIM
    group = max(g for g in range(1, HGRN_MAX_GROUP + 1) if n_chunks % g == 0)

    def col(k):
        return pl.BlockSpec((lp, D), lambda b, h: (b, k * H + h))

    vec = pl.BlockSpec((1, D), lambda b, h: (0, h))
    return pl.pallas_call(
        functools.partial(_hgrn_kernel, n_chunks=n_chunks, group=group),
        grid=(batch, H),
        in_specs=[col(0), col(1), col(2), col(3), vec, vec],
        out_specs=pl.BlockSpec((lp, D), lambda b, h: (b, h)),
        out_shape=jax.ShapeDtypeStruct((n, HGRN_WIDTH), BF16),
        scratch_shapes=[
            pltpu.VMEM((D, D), F32),
            pltpu.VMEM((group * CHUNK + SUB, D), F32),
            pltpu.VMEM((group * CHUNK + SUB, D), F32),
            pltpu.VMEM((group * CHUNK + SUB, D), F32),
        ],
        compiler_params=pltpu.CompilerParams(
            dimension_semantics=("arbitrary", "arbitrary"), vmem_limit_bytes=VMEM_LIMIT),
        name="hgrn2",
    )(hg, hg, hg, hg, lb, ng)


def _sortable_to_f32(key):
    bits = jnp.where(key < 0, key ^ jnp.int32(0x7FFFFFFF), key)
    return pltpu.bitcast(bits, F32)


def _fold_rows(x, op):
    rows = x.shape[0]
    if rows > FOLD and rows % FOLD == 0:
        x = op(x.reshape(rows // FOLD, FOLD, x.shape[1]), axis=0)
    return op(x, axis=0, keepdims=True)


def _dsa_kernel(q_ref, iq_ref, iw_ref, k_ref, v_ref, ik_ref, ltri_ref, o_ref,
                sc_ref, vt_ref, iq8_ref, qg_ref, ot_ref, *, topk, n_valid, extents):
    QB = Q_BLOCK
    lp = k_ref.shape[0]
    t = pl.program_id(1)
    kf = float(topk)

    @pl.when(t == 0)
    def _():
        for kb in range(lp // QB):
            sl = slice(kb * QB, (kb + 1) * QB)
            vt = v_ref[sl, :].astype(F32).T.astype(BF16)
            for g in range(ATT_KV_HEADS):
                vt_ref[g, :ATT_HEAD_DIM, sl] = vt[g * ATT_HEAD_DIM:(g + 1) * ATT_HEAD_DIM, :]
        for g in range(ATT_KV_HEADS):
            vt_ref[g, ATT_HEAD_DIM:, :] = jnp.ones((ONES_ROWS, lp), BF16)

    lane = lax.broadcasted_iota(jnp.int32, (QB, LANES), 1)
    iq = iq_ref[...]
    for h in range(IDX_HEADS):
        grp, slot = divmod(h, IDX_HEADS_PER_LANE_GROUP)
        part = iq[:, grp * LANES:(grp + 1) * LANES]
        iq8_ref[h * QB:(h + 1) * QB, :] = jnp.where((lane >> 5) == slot, part, jnp.zeros_like(part))
    wt = iw_ref[...].T
    lane_half = lane >> 6
    for h in range(ATT_HEADS):
        g, r = divmod(h, ATT_GROUP)
        m, p = divmod(h, LANES // ATT_HEAD_DIM)
        qh = q_ref[:, m * LANES:(m + 1) * LANES].astype(F32)
        if p != g:
            qh = pltpu.roll(qh, ATT_HEAD_DIM, 1)
        qg_ref[g, r * QB:(r + 1) * QB, :] = jnp.where(lane_half == g, qh, 0.0).astype(BF16)

    qpos = t * QB + lax.broadcasted_iota(jnp.int32, (1, QB), 1)
    qchunk = qpos >> 6
    n_adm = jnp.minimum((qchunk + 1) * CHUNK, n_valid) - FRONT
    take_all = n_adm <= topk
    ltri = ltri_ref[...]

    def tile_body(n_blocks):
        ke = n_blocks * QB
        keys = slice(0, ke)
        kpos = lax.broadcasted_iota(jnp.int32, (ke, 1), 0)
        adm = ((kpos >> 6) <= qchunk) & (kpos >= FRONT) & (kpos < n_valid)

        ik = ik_ref[keys, :]
        acc = jnp.zeros((ke, QB), F32)
        for h in range(IDX_HEADS):
            rel = jnp.maximum(_dot_nt(ik, iq8_ref[h * QB:(h + 1) * QB, :]), 0.0)
            acc = acc + rel * wt[IDX_HEAD_DIM + h:IDX_HEAD_DIM + h + 1, :]
        acc = jnp.where(acc == 0.0, 0.0, acc)
        sc_ref[keys, :] = jnp.where(adm, acc, NEG_INF)

        def count_ge(cand):
            return _fold_rows(jnp.where(sc_ref[keys, :] >= cand, 1.0, 0.0), jnp.sum)

        def counts(cand):
            x = sc_ref[keys, :]
            return (_fold_rows(jnp.where(x >= cand, 1.0, 0.0), jnp.sum),
                    _fold_rows(jnp.where(x > cand, 1.0, 0.0), jnp.sum))

        def search(i, tkey):
            cand = tkey + lax.shift_left(jnp.int32(1), 31 - i)
            return jnp.where(count_ge(_sortable_to_f32(cand)) >= kf, cand, tkey)

        tkey = lax.fori_loop(0, 32, search, jnp.full((1, QB), INT_MIN, jnp.int32))
        thr = _sortable_to_f32(jnp.where(take_all, NEG_FLT_MAX_KEY, tkey))
        c_ge, c_gt = counts(thr)

        def unsettled(c_ge, c_gt):
            live = jnp.logical_not(take_all)
            return (c_gt >= kf) & live, (c_ge < kf) & live

        def cond(carry):
            up, dn = unsettled(carry[1], carry[2])
            return jnp.max(jnp.where(up | dn, 1.0, 0.0)) > 0.0

        def fix(carry):
            thr, c_ge, c_gt = carry
            up, dn = unsettled(c_ge, c_gt)
            x = sc_ref[keys, :]
            above = _fold_rows(jnp.where(x > thr, x, POS_INF), jnp.min)
            below = _fold_rows(jnp.where(x < thr, x, NEG_INF), jnp.max)
            thr = jnp.where(up, above, jnp.where(dn, below, thr))
            return (thr,) + counts(thr)

        thr, c_ge, c_gt = lax.while_loop(cond, fix, (thr, c_ge, c_gt))

        need = kf - c_gt
        seen = jnp.zeros((1, QB), F32)
        for kb in range(n_blocks):
            blk = slice(kb * QB, (kb + 1) * QB)
            x = sc_ref[blk, :]
            eq = x == thr
            pre = _dot(ltri, jnp.where(eq, 1.0, 0.0).astype(BF16))
            take = (x > thr) | (eq & (pre + seen <= need))
            sc_ref[blk, :] = jnp.where(take, 0.0, NEG_INF)
            seen = seen + pre[QB - 1:QB, :]

        bias = sc_ref[keys, :]
        bias = jnp.concatenate([bias] * ATT_GROUP, axis=1)
        kk = k_ref[keys, :]
        for g in range(ATT_KV_HEADS):
            s = _dot_nt(kk, qg_ref[g]) + bias
            e = jnp.exp((s - _fold_rows(s, jnp.max)).astype(BF16))
            pv = _dot(vt_ref[g, :, keys], e)
            pv = pv[:ATT_HEAD_DIM, :] / pv[ATT_HEAD_DIM:ATT_HEAD_DIM + 1, :]
            for r in range(ATT_GROUP):
                h = g * ATT_GROUP + r
                ot_ref[h * ATT_HEAD_DIM:(h + 1) * ATT_HEAD_DIM, :] = pv[:, r * QB:(r + 1) * QB]
        o_ref[...] = ot_ref[...].T.astype(o_ref.dtype)

    lo = 0
    for n_blocks in extents:
        pl.when((t >= lo) & (t < n_blocks))(functools.partial(tile_body, n_blocks))
        lo = n_blocks


def _dsa(q, k, v, iq, ik, iw, ltri, batch, lp, topk, n_valid):
    n = q.shape[0]
    nq = lp // Q_BLOCK
    extents = tuple(sorted({nq - DSA_EXTENT_STEP * i for i in range(-(-nq // DSA_EXTENT_STEP))}))
    qblk = lambda b, i: (b * nq + i, 0)
    bat = lambda b, i: (b, 0)
    return pl.pallas_call(
        functools.partial(_dsa_kernel, topk=topk, n_valid=n_valid, extents=extents),
        grid=(batch, nq),
        in_specs=[
            pl.BlockSpec((Q_BLOCK, ATT_WIDTH), qblk),
            pl.BlockSpec((Q_BLOCK, IDX_WIDTH), qblk),
            pl.BlockSpec((Q_BLOCK, LANES), qblk),
            pl.BlockSpec((lp, LANES), bat),
            pl.BlockSpec((lp, LANES), bat),
            pl.BlockSpec((lp, LANES), bat),
            pl.BlockSpec((Q_BLOCK, Q_BLOCK), lambda b, i: (0, 0)),
        ],
        out_specs=pl.BlockSpec((Q_BLOCK, ATT_WIDTH), qblk),
        out_shape=jax.ShapeDtypeStruct((n, ATT_WIDTH), BF16),
        scratch_shapes=[
            pltpu.VMEM((lp, Q_BLOCK), F32),
            pltpu.VMEM((ATT_KV_HEADS, ATT_HEAD_DIM + ONES_ROWS, lp), BF16),
            pltpu.VMEM((IDX_HEADS * Q_BLOCK, LANES), BF16),
            pltpu.VMEM((ATT_KV_HEADS, ATT_GROUP * Q_BLOCK, LANES), BF16),
            pltpu.VMEM((ATT_WIDTH, Q_BLOCK), F32),
        ],
        compiler_params=pltpu.CompilerParams(
            dimension_semantics=("arbitrary", "arbitrary"), vmem_limit_bytes=VMEM_LIMIT),
        name="dsa",
    )(q, iq, iw, k, v, ik, ltri)


def _merge_kernel(ya_ref, yb_ref, gate_ref, h_ref, wa_ref, wb_ref, wo_ref, fg_ref, wr_ref, br_ref, ltri_ref,
                  ho_ref, xc_ref, route_ref, cnt_ref):
    @pl.when(pl.program_id(0) == 0)
    def _():
        cnt_ref[...] = jnp.zeros_like(cnt_ref)

    gate = gate_ref[...]
    pa = _dot(ya_ref[...], wa_ref[...])
    pb = _dot(yb_ref[...], wb_ref[...])
    merged = jax.nn.sigmoid(gate[:, :D_MODEL]) * pa + jax.nn.sigmoid(gate[:, D_MODEL:]) * pb
    h = h_ref[...] + _dot(merged.astype(BF16), wo_ref[...])
    ho_ref[...] = h
    xn = h * lax.rsqrt(jnp.mean(h * h, axis=-1, keepdims=True) + NORM_EPS) * fg_ref[...]
    xh = xn.astype(BF16)
    xc_ref[:, :D_MODEL] = xh.astype(F32)

    xl = (xn - xh.astype(F32)).astype(BF16)
    r = _dot(xh, wr_ref[...])
    lg = r[:, :LANES] + r[:, LANES:] + _dot(xl, wr_ref[:, :LANES]) + br_ref[...]
    lane = lax.broadcasted_iota(jnp.int32, lg.shape, 1)
    lane_f = lane.astype(F32)
    big = float(LANES)
    is_g = (lane >= N_EXPERTS) & (lane < N_EXPERTS + N_GROUPS)
    gl = jnp.where(is_g, lg, NEG_INF)
    gmax = jnp.max(gl, axis=-1, keepdims=True)
    pg_top = 1.0 / jnp.sum(jnp.exp(gl - gmax), axis=-1, keepdims=True)
    g_lane = jnp.min(jnp.where(gl == gmax, lane_f, big), axis=-1, keepdims=True)
    e_lo = (g_lane - N_EXPERTS) * EXPERTS_PER_GROUP
    in_grp = (lane_f >= e_lo) & (lane_f < e_lo + EXPERTS_PER_GROUP)
    el = jnp.where(in_grp, lg, NEG_INF)
    ee = jnp.exp(el - jnp.max(el, axis=-1, keepdims=True))
    pe = ee / jnp.sum(ee, axis=-1, keepdims=True)
    pe = jnp.where(in_grp, pe, -1.0)
    p1 = jnp.max(pe, axis=-1, keepdims=True)
    i1 = jnp.min(jnp.where(pe == p1, lane_f, big), axis=-1, keepdims=True)
    pe2 = jnp.where(lane_f == i1, -1.0, pe)
    p2 = jnp.max(pe2, axis=-1, keepdims=True)
    i2 = jnp.min(jnp.where(pe2 == p2, lane_f, big), axis=-1, keepdims=True)
    tot = p1 + p2
    xc_ref[:, D_MODEL:] = jnp.where(lane_f == i1, p1 / tot * pg_top,
                                    jnp.where(lane_f == i2, p2 / tot * pg_top, 0.0))

    grp = g_lane - N_EXPERTS
    onehot = jnp.where(lane_f == grp, 1.0, 0.0)
    incl = _dot(ltri_ref[...], onehot.astype(BF16))
    seen = cnt_ref[0:1, :]
    rank = jnp.sum(onehot * (incl - 1.0 + seen), axis=-1, keepdims=True)
    route_ref[...] = jnp.where(lane == 0, grp, jnp.where(lane == 1, rank, 0.0))
    cnt_ref[...] = jnp.broadcast_to(seen + incl[incl.shape[0] - 1:, :], cnt_ref.shape)


def _merge(ya, yb, gates, h, wa, wb, wo, fg, wr, br):
    n = h.shape[0]
    tm = min(MERGE_TILE, n)
    ltri = jnp.asarray(np.tril(np.ones((tm, tm), np.float32)), BF16)
    row = lambda i: (i, 0)
    const = lambda i: (0, 0)
    return pl.pallas_call(
        _merge_kernel,
        grid=(n // tm,),
        in_specs=[
            pl.BlockSpec((tm, HGRN_WIDTH), row),
            pl.BlockSpec((tm, ATT_WIDTH), row),
            pl.BlockSpec((tm, 2 * D_MODEL), row),
            pl.BlockSpec((tm, D_MODEL), row),
            pl.BlockSpec((HGRN_WIDTH, D_MODEL), const),
            pl.BlockSpec((ATT_WIDTH, D_MODEL), const),
            pl.BlockSpec((D_MODEL, D_MODEL), const),
            pl.BlockSpec((1, D_MODEL), const),
            pl.BlockSpec((D_MODEL, 2 * LANES), const),
            pl.BlockSpec((1, LANES), const),
            pl.BlockSpec((tm, tm), const),
        ],
        out_specs=(pl.BlockSpec((tm, D_MODEL), row), pl.BlockSpec((tm, D_MODEL + LANES), row),
                   pl.BlockSpec((tm, LANES), row), pl.BlockSpec((SUBLANES, LANES), const)),
        out_shape=(jax.ShapeDtypeStruct((n, D_MODEL), F32), jax.ShapeDtypeStruct((n, D_MODEL + LANES), F32),
                   jax.ShapeDtypeStruct((n, LANES), F32), jax.ShapeDtypeStruct((SUBLANES, LANES), F32)),
        compiler_params=pltpu.CompilerParams(
            dimension_semantics=("arbitrary",), vmem_limit_bytes=VMEM_LIMIT),
        name="merge_router",
    )(ya, yb, gates, h, wa, wb, wo, fg, wr, br, ltri)


def _row_copies(n_rows, make_copy, whole_copy):
    def issue(r, c):
        make_copy(r).start()
        return c

    lax.fori_loop(0, n_rows, issue, 0, unroll=ROW_DMA_UNROLL)
    whole_copy.wait()


def _dispatch_kernel(dest_ref, x_ref, zeros_ref, xs_ref, sem):
    del zeros_ref
    tm = x_ref.shape[0]

    def row_copy(r):
        return pltpu.make_async_copy(x_ref.at[pl.ds(r, 1)], xs_ref.at[pl.ds(dest_ref[0, 0, r], 1)], sem)

    _row_copies(tm, row_copy, pltpu.make_async_copy(x_ref, xs_ref.at[pl.ds(0, tm)], sem))


def _experts_kernel(tile_group_ref, tile_on_ref, xs_ref, wg_ref, wu_ref, wd_ref, ys_ref):
    t = pl.program_id(0)
    tm = xs_ref.shape[0]

    @pl.when(tile_on_ref[t] == 1)
    def _():
        x = xs_ref[:, :D_MODEL].astype(BF16)
        c = xs_ref[:, D_MODEL:]
        lane = lax.broadcasted_iota(jnp.int32, c.shape, 1)
        first = tile_group_ref[t] * EXPERTS_PER_GROUP
        scale = jnp.concatenate(
            [jnp.broadcast_to(jnp.sum(jnp.where(lane == first + e, c, 0.0), axis=-1, keepdims=True), (tm, EXPERT_FF))
             for e in range(EXPERTS_PER_GROUP)], axis=1)
        a = _dot(x, wg_ref[0])
        u = _dot(x, wu_ref[0])
        act = (a * jax.nn.sigmoid(a)) * u * scale
        ys_ref[...] = _dot(act.astype(BF16), wd_ref[0])

    @pl.when(tile_on_ref[t] == 0)
    def _():
        ys_ref[...] = jnp.zeros_like(ys_ref)


def _combine_kernel(dest_ref, h_ref, ys_ref, o_ref, buf_ref, sem):
    tm = h_ref.shape[0]

    def row_copy(r):
        return pltpu.make_async_copy(ys_ref.at[pl.ds(dest_ref[0, 0, r], 1)], buf_ref.at[pl.ds(r, 1)], sem)

    _row_copies(tm, row_copy, pltpu.make_async_copy(ys_ref.at[pl.ds(0, tm)], buf_ref, sem))
    o_ref[...] = h_ref[...] + buf_ref[...]


def _expert_weights_kernel(g_ref, u_ref, d_ref, go_ref, uo_ref, do_ref):
    go_ref[0] = g_ref[0].astype(BF16)
    uo_ref[0] = u_ref[0].astype(BF16)
    do_ref[0] = d_ref[0].astype(BF16)


def _expert_weights(w_gate, w_up, w_down, layer):
    per = EXPERTS_PER_GROUP
    src = lambda shape: pl.BlockSpec((1,) + shape, lambda e: (layer * N_EXPERTS + e, 0, 0))
    side = pl.BlockSpec((1, D_MODEL, EXPERT_FF), lambda e: (e // per, 0, e % per))
    stack = pl.BlockSpec((1, EXPERT_FF, D_MODEL), lambda e: (e // per, e % per, 0))
    return pl.pallas_call(
        _expert_weights_kernel,
        grid=(N_EXPERTS,),
        in_specs=[src((D_MODEL, EXPERT_FF)), src((D_MODEL, EXPERT_FF)), src((EXPERT_FF, D_MODEL))],
        out_specs=(side, side, stack),
        out_shape=(jax.ShapeDtypeStruct((N_GROUPS, D_MODEL, per * EXPERT_FF), BF16),
                   jax.ShapeDtypeStruct((N_GROUPS, D_MODEL, per * EXPERT_FF), BF16),
                   jax.ShapeDtypeStruct((N_GROUPS, per * EXPERT_FF, D_MODEL), BF16)),
        compiler_params=pltpu.CompilerParams(
            dimension_semantics=("arbitrary",), vmem_limit_bytes=VMEM_LIMIT),
        name="expert_weights",
    )(w_gate, w_up, w_down)


def _moe(xc, route, counts, h, wg, wu, wd):
    n = h.shape[0]
    tm = min(MERGE_TILE, n)
    ts = min(MOE_SLOT_TILE, n)
    n_tiles = n // ts + N_GROUPS
    n_slots = n_tiles * ts

    grp = route[:, 0].astype(jnp.int32)
    rank = route[:, 1].astype(jnp.int32)
    cnt = counts[0, :N_GROUPS].astype(jnp.int32)
    padded = (cnt + ts - 1) // ts * ts
    ends = jnp.cumsum(padded)
    starts = ends - padded
    dest = (starts[grp] + rank).reshape(n // tm, 1, tm)
    tile_start = jnp.arange(n_tiles, dtype=jnp.int32) * ts
    tile_group = jnp.minimum(jnp.sum(tile_start[:, None] >= ends[None, :], axis=1), N_GROUPS - 1).astype(jnp.int32)
    tile_on = (tile_start < ends[N_GROUPS - 1]).astype(jnp.int32)

    tok = lambda i: (i, 0)
    smem_idx = pl.BlockSpec((1, 1, tm), lambda i: (i, 0, 0), memory_space=pltpu.SMEM)
    hbm = pl.BlockSpec(memory_space=pl.ANY)
    xs = pl.pallas_call(
        _dispatch_kernel,
        grid=(n // tm,),
        in_specs=[smem_idx, pl.BlockSpec((tm, D_MODEL + LANES), tok), hbm],
        out_specs=hbm,
        out_shape=jax.ShapeDtypeStruct((n_slots, D_MODEL + LANES), F32),
        scratch_shapes=[pltpu.SemaphoreType.DMA(())],
        input_output_aliases={2: 0},
        compiler_params=pltpu.CompilerParams(
            dimension_semantics=("arbitrary",), vmem_limit_bytes=VMEM_LIMIT),
        name="moe_dispatch",
    )(dest, xc, jnp.zeros((n_slots, D_MODEL + LANES), F32))

    wspec = lambda shape: pl.BlockSpec((1,) + shape, lambda t, tg, on: (tg[t], 0, 0))
    ys = pl.pallas_call(
        _experts_kernel,
        grid_spec=pltpu.PrefetchScalarGridSpec(
            num_scalar_prefetch=2,
            grid=(n_tiles,),
            in_specs=[pl.BlockSpec((ts, D_MODEL + LANES), lambda t, tg, on: (t, 0)),
                      wspec((D_MODEL, EXPERTS_PER_GROUP * EXPERT_FF)),
                      wspec((D_MODEL, EXPERTS_PER_GROUP * EXPERT_FF)),
                      wspec((EXPERTS_PER_GROUP * EXPERT_FF, D_MODEL))],
            out_specs=pl.BlockSpec((ts, D_MODEL), lambda t, tg, on: (t, 0)),
        ),
        out_shape=jax.ShapeDtypeStruct((n_slots, D_MODEL), F32),
        compiler_params=pltpu.CompilerParams(
            dimension_semantics=("arbitrary",), vmem_limit_bytes=VMEM_LIMIT),
        name="moe_experts",
    )(tile_group, tile_on, xs, wg, wu, wd)

    return pl.pallas_call(
        _combine_kernel,
        grid=(n // tm,),
        in_specs=[smem_idx, pl.BlockSpec((tm, D_MODEL), tok), hbm],
        out_specs=pl.BlockSpec((tm, D_MODEL), tok),
        out_shape=jax.ShapeDtypeStruct((n, D_MODEL), F32),
        scratch_shapes=[pltpu.VMEM((tm, D_MODEL), F32), pltpu.SemaphoreType.DMA(())],
        compiler_params=pltpu.CompilerParams(
            dimension_semantics=("arbitrary",), vmem_limit_bytes=VMEM_LIMIT),
        name="moe_combine",
    )(dest, h, ys)


def _final_kernel(h_ref, g_ref, o_ref, *, seq):
    step = 256 if seq % 256 == 0 else CHUNK
    for r in range(0, seq, step):
        x = h_ref[CHUNK + r:CHUNK + r + step, :]
        o_ref[0, r:r + step, :] = x * lax.rsqrt(jnp.mean(x * x, axis=-1, keepdims=True) + NORM_EPS) * g_ref[...]


def _final_norm(h, g, batch, lp, seq):
    return pl.pallas_call(
        functools.partial(_final_kernel, seq=seq),
        grid=(batch,),
        in_specs=[pl.BlockSpec((lp, D_MODEL), lambda b: (b, 0)), pl.BlockSpec((1, D_MODEL), lambda b: (0, 0))],
        out_specs=pl.BlockSpec((1, seq, D_MODEL), lambda b: (b, 0, 0)),
        out_shape=jax.ShapeDtypeStruct((batch, seq, D_MODEL), F32),
        compiler_params=pltpu.CompilerParams(
            dimension_semantics=("arbitrary",), vmem_limit_bytes=VMEM_LIMIT),
        name="final_norm",
    )(h, g)


def _rope_lane_tables(pos, head_dim):
    rot = head_dim // 4
    half = rot // 2
    inv = ROPE_THETA ** (-jnp.arange(0, rot, 2, dtype=F32) / rot)
    ang = pos[:, None] * inv[None, :]
    cos, sin = jnp.cos(ang), jnp.sin(ang)
    jj = np.arange(LANES) % head_dim
    first = jnp.asarray(jj < half)
    second = jnp.asarray((jj >= half) & (jj < rot))
    fidx = jnp.asarray(np.where(jj < half, jj, np.where(jj < rot, jj - half, 0)))
    cl, sl = cos[:, fidx], sin[:, fidx]
    c = jnp.where(first | second, cl, 1.0)
    s1 = jnp.where(second, sl, 0.0)
    s2 = jnp.where(first, -sl, 0.0)
    return c, s1, s2


def kernel(x, meta_tokens, mix_norm_g, w_in, hgrn_lb_logits, hgrn_norm_g, idx_k_norm_g, w_branch_hgrn,
           w_branch_dsa, w_out, ffn_norm_g, router_group_w, router_group_b, router_expert_w,
           router_expert_b, w_expert_gate, w_expert_up, w_expert_down, final_norm_g):
    batch, seq, _ = x.shape
    depth = w_in.shape[0]
    n_valid = CHUNK + seq
    assert seq % CHUNK == 0
    lp = -(-n_valid // Q_BLOCK) * Q_BLOCK
    n_chunks = n_valid // CHUNK
    topk = min(TOPK_MAX, (N_META + seq) // 4)

    meta = jnp.broadcast_to(meta_tokens[None].astype(x.dtype), (batch, N_META, D_MODEL))
    h = jnp.concatenate([jnp.zeros((batch, FRONT, D_MODEL), x.dtype), meta, x,
                         jnp.zeros((batch, lp - n_valid, D_MODEL), x.dtype)], axis=1)
    h = h.reshape(batch * lp, D_MODEL)

    pos = jnp.clip(jnp.arange(lp, dtype=jnp.int32) - FRONT, 0, N_META + seq - 1).astype(F32)
    tabs = jnp.stack(_rope_lane_tables(pos, ATT_HEAD_DIM) + _rope_lane_tables(pos, IDX_HEAD_DIM))

    p = jax.nn.softmax(hgrn_lb_logits.astype(F32), axis=0)
    cs = jnp.cumsum(p, axis=0)
    lower_bounds = cs - cs[0:1]

    rep = np.zeros((LANES, LANES), np.float32)
    for slot in range(IDX_HEADS_PER_LANE_GROUP):
        rep[np.arange(IDX_HEAD_DIM), slot * IDX_HEAD_DIM + np.arange(IDX_HEAD_DIM)] = 1.0
    rep = jnp.asarray(rep, BF16)
    ltri = jnp.asarray(np.tril(np.ones((Q_BLOCK, Q_BLOCK), np.float32)), BF16)

    pad_cols = jnp.zeros((D_MODEL, IDX_PAD - IDX_RAW), F32)
    for layer in range(depth):
        w = w_in[layer]
        w_p = jnp.concatenate([w[:, :COL_IDX + IDX_RAW], pad_cols, w[:, COL_IDX + IDX_RAW:]], axis=1).astype(BF16)
        ikg = jnp.pad(idx_k_norm_g[layer].astype(F32), (0, LANES - IDX_HEAD_DIM))[None]
        hg, gates, q, k, v, iq, ik, iw = _inproj(h, mix_norm_g[layer][None].astype(F32), w_p, tabs, ikg, rep, lp)
        ya = _hgrn(hg, lower_bounds[layer][None], hgrn_norm_g[layer][None].astype(F32), batch, lp, n_chunks)
        yb = _dsa(q, k, v, iq, ik, iw, ltri, batch, lp, topk, n_valid)
        wr = jnp.concatenate([router_expert_w[layer], router_group_w[layer],
                              jnp.zeros((D_MODEL, LANES - N_EXPERTS - N_GROUPS), F32)], axis=1)
        wr_hi = wr.astype(BF16)
        wr = jnp.concatenate([wr_hi, (wr - wr_hi.astype(F32)).astype(BF16)], axis=1)
        br = jnp.concatenate([router_expert_b[layer], router_group_b[layer],
                              jnp.zeros((LANES - N_EXPERTS - N_GROUPS,), F32)])[None]
        h, xc, route, counts = _merge(ya, yb, gates, h, w_branch_hgrn[layer].astype(BF16),
                                      w_branch_dsa[layer].astype(BF16), w_out[layer].astype(BF16),
                                      ffn_norm_g[layer][None].astype(F32), wr, br)
        wg, wu, wd = _expert_weights(w_expert_gate.reshape(-1, D_MODEL, EXPERT_FF),
                                     w_expert_up.reshape(-1, D_MODEL, EXPERT_FF),
                                     w_expert_down.reshape(-1, EXPERT_FF, D_MODEL), layer)
        h = _moe(xc, route, counts, h, wg, wu, wd)
    return _final_norm(h, final_norm_g[None].astype(F32), batch, lp, seq)
```

```python
import functools

import jax
import jax.numpy as jnp
import numpy as np
from jax import lax
from jax.experimental import pallas as pl
from jax.experimental.pallas import tpu as pltpu

F32 = jnp.float32
BF16 = jnp.bfloat16

D_MODEL = 1024
CHUNK = 64
N_META = 16
FRONT = CHUNK - N_META
ROPE_THETA = 500000.0
NORM_EPS = 1e-6
LANES = 128
SUBLANES = 8

HGRN_HEADS = 4
HGRN_HEAD_DIM = 128
HGRN_WIDTH = HGRN_HEADS * HGRN_HEAD_DIM
SUB = 8
HGRN_MAX_GROUP = 11

ATT_HEADS = 8
ATT_KV_HEADS = 2
ATT_HEAD_DIM = 64
ATT_GROUP = ATT_HEADS // ATT_KV_HEADS
ATT_WIDTH = ATT_HEADS * ATT_HEAD_DIM
ATT_KV_WIDTH = ATT_KV_HEADS * ATT_HEAD_DIM
ATT_ROPE_HALF = ATT_HEAD_DIM // 4 // 2
IDX_HEADS = 8
IDX_HEAD_DIM = 32
IDX_WIDTH = IDX_HEADS * IDX_HEAD_DIM
IDX_ROPE_HALF = IDX_HEAD_DIM // 4 // 2
IDX_HEADS_PER_LANE_GROUP = LANES // IDX_HEAD_DIM
TOPK_MAX = 256
Q_BLOCK = 128
DSA_EXTENT_STEP = 2
FOLD = 64
ONES_ROWS = 16

N_GROUPS = 4
EXPERTS_PER_GROUP = 8
N_EXPERTS = N_GROUPS * EXPERTS_PER_GROUP
EXPERT_FF = D_MODEL // 4
MERGE_TILE = 512
MOE_SLOT_TILE = 512
ROW_DMA_UNROLL = 8

COL_HG = 0
COL_ATT = 4 * HGRN_WIDTH
COL_IDX = COL_ATT + ATT_WIDTH + 2 * ATT_KV_WIDTH
IDX_RAW = IDX_WIDTH + IDX_HEAD_DIM + IDX_HEADS
IDX_PAD = IDX_WIDTH + LANES
COL_GATE = COL_IDX + IDX_PAD
IN_WIDTH_PAD = COL_GATE + 2 * D_MODEL

INT_MIN = np.int32(-2 ** 31)
NEG_FLT_MAX_KEY = np.int32(-2 ** 31 + 0x800000)
NEG_INF = float("-inf")
POS_INF = float("inf")

VMEM_LIMIT = 56 * 1024 * 1024


def _dot(a, b):
    return jnp.dot(a, b, preferred_element_type=F32)


def _dot_nt(a, b):
    return lax.dot_general(a, b, (((1,), (1,)), ((), ())), preferred_element_type=F32)


def _rope(x, c, s1, s2, half):
    return x * c + pltpu.roll(x, half, 1) * s1 + pltpu.roll(x, LANES - half, 1) * s2


def _inproj_kernel(h_ref, g_ref, w_ref, tab_ref, ikg_ref, rep_ref,
                   hg_ref, gate_ref, q_ref, k_ref, v_ref, iq_ref, ik_ref, iw_ref):
    x = h_ref[...]
    xn = (x * lax.rsqrt(jnp.mean(x * x, axis=-1, keepdims=True) + NORM_EPS) * g_ref[...]).astype(BF16)
    hg_ref[...] = _dot(xn, w_ref[:, COL_HG:COL_ATT])
    gate_ref[...] = _dot(xn, w_ref[:, COL_GATE:IN_WIDTH_PAD])
    att = _dot(xn, w_ref[:, COL_ATT:COL_IDX])
    idx = _dot(xn, w_ref[:, COL_IDX:COL_GATE])
    ca, s1a, s2a = tab_ref[0], tab_ref[1], tab_ref[2]
    ci, s1i, s2i = tab_ref[3], tab_ref[4], tab_ref[5]
    scale = ATT_HEAD_DIM ** -0.5
    for m in range(ATT_WIDTH // LANES):
        sl = slice(m * LANES, (m + 1) * LANES)
        q_ref[:, sl] = (_rope(att[:, sl], ca, s1a, s2a, ATT_ROPE_HALF) * scale).astype(BF16)
    k_ref[...] = _rope(att[:, ATT_WIDTH:ATT_WIDTH + LANES], ca, s1a, s2a, ATT_ROPE_HALF).astype(BF16)
    v_ref[...] = att[:, ATT_WIDTH + LANES:].astype(BF16)
    for m in range(IDX_WIDTH // LANES):
        sl = slice(m * LANES, (m + 1) * LANES)
        iq_ref[:, sl] = _rope(idx[:, sl], ci, s1i, s2i, IDX_ROPE_HALF).astype(BF16)
    xk = idx[:, IDX_WIDTH:]
    lane = lax.broadcasted_iota(jnp.int32, xk.shape, 1)
    ms = jnp.sum(jnp.where(lane < IDX_HEAD_DIM, xk * xk, 0.0), axis=-1, keepdims=True) / IDX_HEAD_DIM
    ikn = xk * lax.rsqrt(ms + NORM_EPS) * ikg_ref[...]
    ikr = _rope(ikn, ci, s1i, s2i, IDX_ROPE_HALF).astype(BF16)
    ik_ref[...] = _dot(ikr, rep_ref[...]).astype(BF16)
    iw_ref[...] = xk * (IDX_HEADS ** -0.5 * IDX_HEAD_DIM ** -0.5)


def _inproj(h, g, w, tabs, ikg, rep, lp):
    n = h.shape[0]
    tm = lp // 8
    per = lp // tm
    row = lambda i: (i, 0)
    const2 = lambda i: (0, 0)
    out_shape = (
        jax.ShapeDtypeStruct((n, 4 * HGRN_WIDTH), F32),
        jax.ShapeDtypeStruct((n, 2 * D_MODEL), F32),
        jax.ShapeDtypeStruct((n, ATT_WIDTH), BF16),
        jax.ShapeDtypeStruct((n, LANES), BF16),
        jax.ShapeDtypeStruct((n, LANES), BF16),
        jax.ShapeDtypeStruct((n, IDX_WIDTH), BF16),
        jax.ShapeDtypeStruct((n, LANES), BF16),
        jax.ShapeDtypeStruct((n, LANES), F32),
    )
    return pl.pallas_call(
        _inproj_kernel,
        grid=(n // tm,),
        in_specs=[
            pl.BlockSpec((tm, D_MODEL), row),
            pl.BlockSpec((1, D_MODEL), const2),
            pl.BlockSpec((D_MODEL, IN_WIDTH_PAD), const2),
            pl.BlockSpec((6, tm, LANES), lambda i: (0, i % per, 0)),
            pl.BlockSpec((1, LANES), const2),
            pl.BlockSpec((LANES, LANES), const2),
        ],
        out_specs=tuple(pl.BlockSpec((tm, s.shape[1]), row) for s in out_shape),
        out_shape=out_shape,
        compiler_params=pltpu.CompilerParams(
            dimension_semantics=("arbitrary",), vmem_limit_bytes=VMEM_LIMIT),
        name="inproj",
    )(h, g, w, tabs, ikg, rep)


def _hgrn_kernel(q_ref, f_ref, i_ref, g_ref, lb_ref, ng_ref, o_ref, st_ref, kb_ref, bb_ref, vb_ref,
                 *, n_chunks, group):
    C = CHUNK
    D = HGRN_HEAD_DIM
    st_ref[...] = jnp.zeros_like(st_ref)
    kb_ref[...] = jnp.zeros_like(kb_ref)
    bb_ref[...] = jnp.zeros_like(bb_ref)
    vb_ref[...] = jnp.zeros_like(vb_ref)
    lb = lb_ref[...]
    ng = ng_ref[...]
    R = group * C
    tri_row = lax.broadcasted_iota(jnp.int32, (C, C), 0)
    tri_col = lax.broadcasted_iota(jnp.int32, (C, C), 1)
    tri = (tri_row >= tri_col).astype(F32)
    local_row = lax.broadcasted_iota(jnp.int32, (R, 1), 0)
    sub_pos = local_row & (SUB - 1)

    def chunk(x, u):
        return x[u * C:(u + 1) * C, :]

    def body(it, carry):
        rows = pl.ds(pl.multiple_of(it * R, C), R)
        z = f_ref[rows, :]
        q = q_ref[rows, :]
        v = i_ref[rows, :]
        go = g_ref[rows, :]
        pad = (it * R + local_row) < FRONT
        f = lb + (1.0 - lb) * jax.nn.sigmoid(z)
        logf = jnp.where(pad, 0.0, jnp.log(f))
        k = jnp.where(pad, 0.0, (1.0 - lb) * jax.nn.sigmoid(-z))
        v = jnp.where(pad, 0.0, v)
        b = jnp.concatenate(
            [jnp.dot(tri, chunk(logf, u), preferred_element_type=F32, precision=lax.Precision.HIGHEST)
             for u in range(group)], axis=0)
        b3 = b.reshape(group, C, D)
        b_last = b3[:, C - 1:C, :]
        qd = (q * jnp.exp(b)).astype(BF16)
        kd = (k.reshape(group, C, D) * jnp.exp(b_last - b3)).reshape(R, D).astype(BF16)
        e_last = jnp.exp(b_last)
        upd =[_dot(chunk(v, u).T.astype(BF16), chunk(kd, u)) for u in range(group)]

        st = st_ref[...]
        before = []
        for u in range(group):
            before.append(st.astype(BF16))
            st = st * e_last[u] + upd[u]
        st_ref[...] = st
        o = jnp.concatenate([_dot_nt(chunk(qd, u), before[u]) for u in range(group)], axis=0)

        kb_ref[SUB:, :] = k
        bb_ref[SUB:, :] = b
        vb_ref[SUB:, :] = v
        for j in range(SUB):
            ks = kb_ref[SUB - j:SUB - j + R, :]
            bs = bb_ref[SUB - j:SUB - j + R, :]
            vs = vb_ref[SUB - j:SUB - j + R, :]
            w = jnp.exp(jnp.minimum(b - bs, 0.0))
            a = jnp.sum(q * ks * w, axis=-1, keepdims=True)
            o = o + jnp.where(sub_pos >= j, a, 0.0) * vs

        att = [jnp.zeros((C, C), F32) for _ in range(group)]
        span = C
        while span > SUB:
            half = span // 2
            upper = (local_row & (span - 1)) >= half
            bs3 = b.reshape(R // span, span, D)
            mid = bs3[:, half - 1:half, :]
            qs = jnp.where(upper, (q.reshape(bs3.shape) * jnp.exp(jnp.minimum(bs3 - mid, 0.0))).reshape(R, D), 0.0)
            ks = jnp.where(upper, 0.0, (k.reshape(bs3.shape) * jnp.exp(jnp.minimum(mid - bs3, 0.0))).reshape(R, D))
            qs, ks = qs.astype(BF16), ks.astype(BF16)
            same_block = (tri_row & -span) == (tri_col & -span)
            att = [att[u] + jnp.where(same_block, _dot_nt(chunk(qs, u), chunk(ks, u)), 0.0) for u in range(group)]
            span = half
        o = o + jnp.concatenate([_dot(att[u].astype(BF16), chunk(v, u).astype(BF16)) for u in range(group)], axis=0)

        on = o * lax.rsqrt(jnp.mean(o * o, axis=-1, keepdims=True) + NORM_EPS) * ng
        o_ref[rows, :] = (on * (go * jax.nn.sigmoid(go))).astype(o_ref.dtype)
        return carry

    lax.fori_loop(0, n_chunks // group, body, 0)
    tail = o_ref.shape[0] - n_chunks * C
    if tail:
        o_ref[n_chunks * C:, :] = jnp.zeros((tail, D), o_ref.dtype)


def _hgrn(hg, lb, ng, batch, lp, n_chunks):
    n = hg.shape[0]
    H, D = HGRN_HEADS, HGRN_HEAD_DIM
    group = max(g for g in range(1, HGRN_MAX_GROUP + 1) if n_chunks % g == 0)

    def col(k):
        return pl.BlockSpec((lp, D), lambda b, h: (b, k * H + h))

    vec = pl.BlockSpec((1, D), lambda b, h: (0, h))
    return pl.pallas_call(
        functools.partial(_hgrn_kernel, n_chunks=n_chunks, group=group),
        grid=(batch, H),
        in_specs=[col(0), col(1), col(2), col(3), vec, vec],
        out_specs=pl.BlockSpec((lp, D), lambda b, h: (b, h)),
        out_shape=jax.ShapeDtypeStruct((n, HGRN_WIDTH), BF16),
        scratch_shapes=[
            pltpu.VMEM((D, D), F32),
            pltpu.VMEM((group * CHUNK + SUB, D), F32),
            pltpu.VMEM((group * CHUNK + SUB, D), F32),
            pltpu.VMEM((group * CHUNK + SUB, D), F32),
        ],
        compiler_params=pltpu.CompilerParams(
            dimension_semantics=("arbitrary", "arbitrary"), vmem_limit_bytes=VMEM_LIMIT),
        name="hgrn2",
    )(hg, hg, hg, hg, lb, ng)


def _sortable_to_f32(key):
    bits = jnp.where(key < 0, key ^ jnp.int32(0x7FFFFFFF), key)
    return pltpu.bitcast(bits, F32)


def _fold_rows(x, op):
    rows = x.shape[0]
    if rows > FOLD and rows % FOLD == 0:
        x = op(x.reshape(rows // FOLD, FOLD, x.shape[1]), axis=0)
    return op(x, axis=0, keepdims=True)


def _dsa_kernel(q_ref, iq_ref, iw_ref, k_ref, v_ref, ik_ref, ltri_ref, o_ref,
                sc_ref, vt_ref, iq8_ref, qg_ref, ot_ref, *, topk, n_valid, extents):
    QB = Q_BLOCK
    lp = k_ref.shape[0]
    t = pl.program_id(1)
    kf = float(topk)

    @pl.when(t == 0)
    def _():
        for kb in range(lp // QB):
            sl = slice(kb * QB, (kb + 1) * QB)
            vt = v_ref[sl, :].astype(F32).T.astype(BF16)
            for g in range(ATT_KV_HEADS):
                vt_ref[g, :ATT_HEAD_DIM, sl] = vt[g * ATT_HEAD_DIM:(g + 1) * ATT_HEAD_DIM, :]
        for g in range(ATT_KV_HEADS):
            vt_ref[g, ATT_HEAD_DIM:, :] = jnp.ones((ONES_ROWS, lp), BF16)

    lane = lax.broadcasted_iota(jnp.int32, (QB, LANES), 1)
    iq = iq_ref[...]
    for h in range(IDX_HEADS):
        grp, slot = divmod(h, IDX_HEADS_PER_LANE_GROUP)
        part = iq[:, grp * LANES:(grp + 1) * LANES]
        iq8_ref[h * QB:(h + 1) * QB, :] = jnp.where((lane >> 5) == slot, part, jnp.zeros_like(part))
    wt = iw_ref[...].T
    lane_half = lane >> 6
    for h in range(ATT_HEADS):
        g, r = divmod(h, ATT_GROUP)
        m, p = divmod(h, LANES // ATT_HEAD_DIM)
        qh = q_ref[:, m * LANES:(m + 1) * LANES].astype(F32)
        if p != g:
            qh = pltpu.roll(qh, ATT_HEAD_DIM, 1)
        qg_ref[g, r * QB:(r + 1) * QB, :] = jnp.where(lane_half == g, qh, 0.0).astype(BF16)

    qpos = t * QB + lax.broadcasted_iota(jnp.int32, (1, QB), 1)
    qchunk = qpos >> 6
    n_adm = jnp.minimum((qchunk + 1) * CHUNK, n_valid) - FRONT
    take_all = n_adm <= topk
    ltri = ltri_ref[...]

    def tile_body(n_blocks):
        ke = n_blocks * QB
        keys = slice(0, ke)
        kpos = lax.broadcasted_iota(jnp.int32, (ke, 1), 0)
        adm = ((kpos >> 6) <= qchunk) & (kpos >= FRONT) & (kpos < n_valid)

        ik = ik_ref[keys, :]
        acc = jnp.zeros((ke, QB), F32)
        for h in range(IDX_HEADS):
            rel = jnp.maximum(_dot_nt(ik, iq8_ref[h * QB:(h + 1) * QB, :]), 0.0)
            acc = acc + rel * wt[IDX_HEAD_DIM + h:IDX_HEAD_DIM + h + 1, :]
        acc = jnp.where(acc == 0.0, 0.0, acc)
        sc_ref[keys, :] = jnp.where(adm, acc, NEG_INF)

        def count_ge(cand):
            return _fold_rows(jnp.where(sc_ref[keys, :] >= cand, 1.0, 0.0), jnp.sum)

        def counts(cand):
            x = sc_ref[keys, :]
            return (_fold_rows(jnp.where(x >= cand, 1.0, 0.0), jnp.sum),
                    _fold_rows(jnp.where(x > cand, 1.0, 0.0), jnp.sum))

        def search(i, tkey):
            cand = tkey + lax.shift_left(jnp.int32(1), 31 - i)
            return jnp.where(count_ge(_sortable_to_f32(cand)) >= kf, cand, tkey)

        tkey = lax.fori_loop(0, 32, search, jnp.full((1, QB), INT_MIN, jnp.int32))
        thr = _sortable_to_f32(jnp.where(take_all, NEG_FLT_MAX_KEY, tkey))
        c_ge, c_gt = counts(thr)

        def unsettled(c_ge, c_gt):
            live = jnp.logical_not(take_all)
            return (c_gt >= kf) & live, (c_ge < kf) & live

        def cond(carry):
            up, dn = unsettled(carry[1], carry[2])
            return jnp.max(jnp.where(up | dn, 1.0, 0.0)) > 0.0

        def fix(carry):
            thr, c_ge, c_gt = carry
            up, dn = unsettled(c_ge, c_gt)
            x = sc_ref[keys, :]
            above = _fold_rows(jnp.where(x > thr, x, POS_INF), jnp.min)
            below = _fold_rows(jnp.where(x < thr, x, NEG_INF), jnp.max)
            thr = jnp.where(up, above, jnp.where(dn, below, thr))
            return (thr,) + counts(thr)

        thr, c_ge, c_gt = lax.while_loop(cond, fix, (thr, c_ge, c_gt))

        need = kf - c_gt
        seen = jnp.zeros((1, QB), F32)
        for kb in range(n_blocks):
            blk = slice(kb * QB, (kb + 1) * QB)
            x = sc_ref[blk, :]
            eq = x == thr
            pre = _dot(ltri, jnp.where(eq, 1.0, 0.0).astype(BF16))
            take = (x > thr) | (eq & (pre + seen <= need))
            sc_ref[blk, :] = jnp.where(take, 0.0, NEG_INF)
            seen = seen + pre[QB - 1:QB, :]

        bias = sc_ref[keys, :]
        bias = jnp.concatenate([bias] * ATT_GROUP, axis=1)
        kk = k_ref[keys, :]
        for g in range(ATT_KV_HEADS):
            s = _dot_nt(kk, qg_ref[g]) + bias
            e = jnp.exp((s - _fold_rows(s, jnp.max)).astype(BF16))
            pv = _dot(vt_ref[g, :, keys], e)
            pv = pv[:ATT_HEAD_DIM, :] / pv[ATT_HEAD_DIM:ATT_HEAD_DIM + 1, :]
            for r in range(ATT_GROUP):
                h = g * ATT_GROUP + r
                ot_ref[h * ATT_HEAD_DIM:(h + 1) * ATT_HEAD_DIM, :] = pv[:, r * QB:(r + 1) * QB]
        o_ref[...] = ot_ref[...].T.astype(o_ref.dtype)

    lo = 0
    for n_blocks in extents:
        pl.when((t >= lo) & (t < n_blocks))(functools.partial(tile_body, n_blocks))
        lo = n_blocks


def _dsa(q, k, v, iq, ik, iw, ltri, batch, lp, topk, n_valid):
    n = q.shape[0]
    nq = lp // Q_BLOCK
    extents = tuple(sorted({nq - DSA_EXTENT_STEP * i for i in range(-(-nq // DSA_EXTENT_STEP))}))
    qblk = lambda b, i: (b * nq + i, 0)
    bat = lambda b, i: (b, 0)
    return pl.pallas_call(
        functools.partial(_dsa_kernel, topk=topk, n_valid=n_valid, extents=extents),
        grid=(batch, nq),
        in_specs=[
            pl.BlockSpec((Q_BLOCK, ATT_WIDTH), qblk),
            pl.BlockSpec((Q_BLOCK, IDX_WIDTH), qblk),
            pl.BlockSpec((Q_BLOCK, LANES), qblk),
            pl.BlockSpec((lp, LANES), bat),
            pl.BlockSpec((lp, LANES), bat),
            pl.BlockSpec((lp, LANES), bat),
            pl.BlockSpec((Q_BLOCK, Q_BLOCK), lambda b, i: (0, 0)),
        ],
        out_specs=pl.BlockSpec((Q_BLOCK, ATT_WIDTH), qblk),
        out_shape=jax.ShapeDtypeStruct((n, ATT_WIDTH), BF16),
        scratch_shapes=[
            pltpu.VMEM((lp, Q_BLOCK), F32),
            pltpu.VMEM((ATT_KV_HEADS, ATT_HEAD_DIM + ONES_ROWS, lp), BF16),
            pltpu.VMEM((IDX_HEADS * Q_BLOCK, LANES), BF16),
            pltpu.VMEM((ATT_KV_HEADS, ATT_GROUP * Q_BLOCK, LANES), BF16),
            pltpu.VMEM((ATT_WIDTH, Q_BLOCK), F32),
        ],
        compiler_params=pltpu.CompilerParams(
            dimension_semantics=("arbitrary", "arbitrary"), vmem_limit_bytes=VMEM_LIMIT),
        name="dsa",
    )(q, iq, iw, k, v, ik, ltri)


def _merge_kernel(ya_ref, yb_ref, gate_ref, h_ref, wa_ref, wb_ref, wo_ref, fg_ref, wr_ref, br_ref, ltri_ref,
                  ho_ref, xc_ref, route_ref, cnt_ref):
    @pl.when(pl.program_id(0) == 0)
    def _():
        cnt_ref[...] = jnp.zeros_like(cnt_ref)

    gate = gate_ref[...]
    pa = _dot(ya_ref[...], wa_ref[...])
    pb = _dot(yb_ref[...], wb_ref[...])
    merged = jax.nn.sigmoid(gate[:, :D_MODEL]) * pa + jax.nn.sigmoid(gate[:, D_MODEL:]) * pb
    h = h_ref[...] + _dot(merged.astype(BF16), wo_ref[...])
    ho_ref[...] = h
    xn = h * lax.rsqrt(jnp.mean(h * h, axis=-1, keepdims=True) + NORM_EPS) * fg_ref[...]
    xh = xn.astype(BF16)
    xc_ref[:, :D_MODEL] = xh.astype(F32)

    xl = (xn - xh.astype(F32)).astype(BF16)
    r = _dot(xh, wr_ref[...])
    lg = r[:, :LANES] + r[:, LANES:] + _dot(xl, wr_ref[:, :LANES]) + br_ref[...]
    lane = lax.broadcasted_iota(jnp.int32, lg.shape, 1)
    lane_f = lane.astype(F32)
    big = float(LANES)
    is_g = (lane >= N_EXPERTS) & (lane < N_EXPERTS + N_GROUPS)
    gl = jnp.where(is_g, lg, NEG_INF)
    gmax = jnp.max(gl, axis=-1, keepdims=True)
    pg_top = 1.0 / jnp.sum(jnp.exp(gl - gmax), axis=-1, keepdims=True)
    g_lane = jnp.min(jnp.where(gl == gmax, lane_f, big), axis=-1, keepdims=True)
    e_lo = (g_lane - N_EXPERTS) * EXPERTS_PER_GROUP
    in_grp = (lane_f >= e_lo) & (lane_f < e_lo + EXPERTS_PER_GROUP)
    el = jnp.where(in_grp, lg, NEG_INF)
    ee = jnp.exp(el - jnp.max(el, axis=-1, keepdims=True))
    pe = ee / jnp.sum(ee, axis=-1, keepdims=True)
    pe = jnp.where(in_grp, pe, -1.0)
    p1 = jnp.max(pe, axis=-1, keepdims=True)
    i1 = jnp.min(jnp.where(pe == p1, lane_f, big), axis=-1, keepdims=True)
    pe2 = jnp.where(lane_f == i1, -1.0, pe)
    p2 = jnp.max(pe2, axis=-1, keepdims=True)
    i2 = jnp.min(jnp.where(pe2 == p2, lane_f, big), axis=-1, keepdims=True)
    tot = p1 + p2
    xc_ref[:, D_MODEL:] = jnp.where(lane_f == i1, p1 / tot * pg_top,
                                    jnp.where(lane_f == i2, p2 / tot * pg_top, 0.0))

    grp = g_lane - N_EXPERTS
    onehot = jnp.where(lane_f == grp, 1.0, 0.0)
    incl = _dot(ltri_ref[...], onehot.astype(BF16))
    seen = cnt_ref[0:1, :]
    rank = jnp.sum(onehot * (incl - 1.0 + seen), axis=-1, keepdims=True)
    route_ref[...] = jnp.where(lane == 0, grp, jnp.where(lane == 1, rank, 0.0))
    cnt_ref[...] = jnp.broadcast_to(seen + incl[incl.shape[0] - 1:, :], cnt_ref.shape)


def _merge(ya, yb, gates, h, wa, wb, wo, fg, wr, br):
    n = h.shape[0]
    tm = min(MERGE_TILE, n)
    ltri = jnp.asarray(np.tril(np.ones((tm, tm), np.float32)), BF16)
    row = lambda i: (i, 0)
    const = lambda i: (0, 0)
    return pl.pallas_call(
        _merge_kernel,
        grid=(n // tm,),
        in_specs=[
            pl.BlockSpec((tm, HGRN_WIDTH), row),
            pl.BlockSpec((tm, ATT_WIDTH), row),
            pl.BlockSpec((tm, 2 * D_MODEL), row),
            pl.BlockSpec((tm, D_MODEL), row),
            pl.BlockSpec((HGRN_WIDTH, D_MODEL), const),
            pl.BlockSpec((ATT_WIDTH, D_MODEL), const),
            pl.BlockSpec((D_MODEL, D_MODEL), const),
            pl.BlockSpec((1, D_MODEL), const),
            pl.BlockSpec((D_MODEL, 2 * LANES), const),
            pl.BlockSpec((1, LANES), const),
            pl.BlockSpec((tm, tm), const),
        ],
        out_specs=(pl.BlockSpec((tm, D_MODEL), row), pl.BlockSpec((tm, D_MODEL + LANES), row),
                   pl.BlockSpec((tm, LANES), row), pl.BlockSpec((SUBLANES, LANES), const)),
        out_shape=(jax.ShapeDtypeStruct((n, D_MODEL), F32), jax.ShapeDtypeStruct((n, D_MODEL + LANES), F32),
                   jax.ShapeDtypeStruct((n, LANES), F32), jax.ShapeDtypeStruct((SUBLANES, LANES), F32)),
        compiler_params=pltpu.CompilerParams(
            dimension_semantics=("arbitrary",), vmem_limit_bytes=VMEM_LIMIT),
        name="merge_router",
    )(ya, yb, gates, h, wa, wb, wo, fg, wr, br, ltri)


def _row_copies(n_rows, make_copy, whole_copy):
    def issue(r, c):
        make_copy(r).start()
        return c

    lax.fori_loop(0, n_rows, issue, 0, unroll=ROW_DMA_UNROLL)
    whole_copy.wait()


def _experts_kernel(tile_group_ref, tile_on_ref, src_ref, src_next_ref, xc_ref, wg_ref, wu_ref, wd_ref, ys_ref,
                    xbuf_ref, sem):
    t = pl.program_id(0)
    last = pl.num_programs(0) - 1
    tm = ys_ref.shape[0]
    cur = t % 2

    def fetch(idx_ref, buf):
        def row_copy(r, c):
            pltpu.make_async_copy(xc_ref.at[pl.ds(idx_ref[0, 0, r], 1)], xbuf_ref.at[buf, pl.ds(r, 1)],
                                  sem.at[buf]).start()
            return c

        lax.fori_loop(0, tm, row_copy, 0, unroll=ROW_DMA_UNROLL)

    @pl.when(t == 0)
    def _():
        fetch(src_ref, 0)

    @pl.when(jnp.logical_and(t < last, tile_on_ref[jnp.minimum(t + 1, last)] == 1))
    def _():
        fetch(src_next_ref, 1 - cur)

    @pl.when(tile_on_ref[t] == 1)
    def _():
        pltpu.make_async_copy(xc_ref.at[pl.ds(0, tm)], xbuf_ref.at[cur], sem.at[cur]).wait()
        x = xbuf_ref[cur, :, :D_MODEL].astype(BF16)
        c = xbuf_ref[cur, :, D_MODEL:]
        lane = lax.broadcasted_iota(jnp.int32, c.shape, 1)
        first = tile_group_ref[t] * EXPERTS_PER_GROUP
        scale = jnp.concatenate(
            [jnp.broadcast_to(jnp.sum(jnp.where(lane == first + e, c, 0.0), axis=-1, keepdims=True), (tm, EXPERT_FF))
             for e in range(EXPERTS_PER_GROUP)], axis=1)
        a = _dot(x, wg_ref[0])
        u = _dot(x, wu_ref[0])
        act = (a * jax.nn.sigmoid(a)) * u * scale
        ys_ref[...] = _dot(act.astype(BF16), wd_ref[0])

    @pl.when(tile_on_ref[t] == 0)
    def _():
        ys_ref[...] = jnp.zeros_like(ys_ref)


def _combine_kernel(dest_ref, h_ref, ys_ref, o_ref, buf_ref, sem):
    tm = h_ref.shape[0]

    def row_copy(r):
        return pltpu.make_async_copy(ys_ref.at[pl.ds(dest_ref[0, 0, r], 1)], buf_ref.at[pl.ds(r, 1)], sem)

    _row_copies(tm, row_copy, pltpu.make_async_copy(ys_ref.at[pl.ds(0, tm)], buf_ref, sem))
    o_ref[...] = h_ref[...] + buf_ref[...]


def _expert_weights_kernel(g_ref, u_ref, d_ref, go_ref, uo_ref, do_ref):
    go_ref[0] = g_ref[0].astype(BF16)
    uo_ref[0] = u_ref[0].astype(BF16)
    do_ref[0] = d_ref[0].astype(BF16)


def _expert_weights(w_gate, w_up, w_down, layer):
    per = EXPERTS_PER_GROUP
    src = lambda shape: pl.BlockSpec((1,) + shape, lambda e: (layer * N_EXPERTS + e, 0, 0))
    side = pl.BlockSpec((1, D_MODEL, EXPERT_FF), lambda e: (e // per, 0, e % per))
    stack = pl.BlockSpec((1, EXPERT_FF, D_MODEL), lambda e: (e // per, e % per, 0))
    return pl.pallas_call(
        _expert_weights_kernel,
        grid=(N_EXPERTS,),
        in_specs=[src((D_MODEL, EXPERT_FF)), src((D_MODEL, EXPERT_FF)), src((EXPERT_FF, D_MODEL))],
        out_specs=(side, side, stack),
        out_shape=(jax.ShapeDtypeStruct((N_GROUPS, D_MODEL, per * EXPERT_FF), BF16),
                   jax.ShapeDtypeStruct((N_GROUPS, D_MODEL, per * EXPERT_FF), BF16),
                   jax.ShapeDtypeStruct((N_GROUPS, per * EXPERT_FF, D_MODEL), BF16)),
        compiler_params=pltpu.CompilerParams(
            dimension_semantics=("arbitrary",), vmem_limit_bytes=VMEM_LIMIT),
        name="expert_weights",
    )(w_gate, w_up, w_down)


def _moe(xc, route, counts, h, wg, wu, wd):
    n = h.shape[0]
    tm = min(MERGE_TILE, n)
    ts = min(MOE_SLOT_TILE, n)
    n_tiles = n // ts + N_GROUPS
    n_slots = n_tiles * ts

    grp = route[:, 0].astype(jnp.int32)
    rank = route[:, 1].astype(jnp.int32)
    cnt = counts[0, :N_GROUPS].astype(jnp.int32)
    padded = (cnt + ts - 1) // ts * ts
    ends = jnp.cumsum(padded)
    starts = ends - padded
    dest = (starts[grp] + rank).reshape(n // tm, 1, tm)
    tile_start = jnp.arange(n_tiles, dtype=jnp.int32) * ts
    tile_group = jnp.minimum(jnp.sum(tile_start[:, None] >= ends[None, :], axis=1), N_GROUPS - 1).astype(jnp.int32)
    tile_on = (tile_start < ends[N_GROUPS - 1]).astype(jnp.int32)

    tok = lambda i: (i, 0)
    smem_idx = pl.BlockSpec((1, 1, tm), lambda i: (i, 0, 0), memory_space=pltpu.SMEM)
    hbm = pl.BlockSpec(memory_space=pl.ANY)
    src = (jnp.zeros((n_slots,), jnp.int32).at[dest.reshape(-1)].set(jnp.arange(n, dtype=jnp.int32), unique_indices=True)
           .reshape(n_tiles, 1, ts))

    wspec = lambda shape: pl.BlockSpec((1,) + shape, lambda t, tg, on: (tg[t], 0, 0))
    slot_idx = lambda step: pl.BlockSpec((1, 1, ts), lambda t, tg, on: (jnp.minimum(t + step, n_tiles - 1), 0, 0),
                                         memory_space=pltpu.SMEM)
    ys = pl.pallas_call(
        _experts_kernel,
        grid_spec=pltpu.PrefetchScalarGridSpec(
            num_scalar_prefetch=2,
            grid=(n_tiles,),
            in_specs=[slot_idx(0), slot_idx(1), hbm,
                      wspec((D_MODEL, EXPERTS_PER_GROUP * EXPERT_FF)),
                      wspec((D_MODEL, EXPERTS_PER_GROUP * EXPERT_FF)),
                      wspec((EXPERTS_PER_GROUP * EXPERT_FF, D_MODEL))],
            out_specs=pl.BlockSpec((ts, D_MODEL), lambda t, tg, on: (t, 0)),
            scratch_shapes=[pltpu.VMEM((2, ts, D_MODEL + LANES), F32), pltpu.SemaphoreType.DMA((2,))],
        ),
        out_shape=jax.ShapeDtypeStruct((n_slots, D_MODEL), F32),
        compiler_params=pltpu.CompilerParams(
            dimension_semantics=("arbitrary",), vmem_limit_bytes=VMEM_LIMIT),
        name="moe_experts",
    )(tile_group, tile_on, src, src, xc, wg, wu, wd)

    return pl.pallas_call(
        _combine_kernel,
        grid=(n // tm,),
        in_specs=[smem_idx, pl.BlockSpec((tm, D_MODEL), tok), hbm],
        out_specs=pl.BlockSpec((tm, D_MODEL), tok),
        out_shape=jax.ShapeDtypeStruct((n, D_MODEL), F32),
        scratch_shapes=[pltpu.VMEM((tm, D_MODEL), F32), pltpu.SemaphoreType.DMA(())],
        compiler_params=pltpu.CompilerParams(
            dimension_semantics=("arbitrary",), vmem_limit_bytes=VMEM_LIMIT),
        name="moe_combine",
    )(dest, h, ys)


def _final_kernel(h_ref, g_ref, o_ref, *, seq):
    step = 256 if seq % 256 == 0 else CHUNK
    for r in range(0, seq, step):
        x = h_ref[CHUNK + r:CHUNK + r + step, :]
        o_ref[0, r:r + step, :] = x * lax.rsqrt(jnp.mean(x * x, axis=-1, keepdims=True) + NORM_EPS) * g_ref[...]


def _final_norm(h, g, batch, lp, seq):
    return pl.pallas_call(
        functools.partial(_final_kernel, seq=seq),
        grid=(batch,),
        in_specs=[pl.BlockSpec((lp, D_MODEL), lambda b: (b, 0)), pl.BlockSpec((1, D_MODEL), lambda b: (0, 0))],
        out_specs=pl.BlockSpec((1, seq, D_MODEL), lambda b: (b, 0, 0)),
        out_shape=jax.ShapeDtypeStruct((batch, seq, D_MODEL), F32),
        compiler_params=pltpu.CompilerParams(
            dimension_semantics=("arbitrary",), vmem_limit_bytes=VMEM_LIMIT),
        name="final_norm",
    )(h, g)


def _rope_lane_tables(pos, head_dim):
    rot = head_dim // 4
    half = rot // 2
    inv = ROPE_THETA ** (-jnp.arange(0, rot, 2, dtype=F32) / rot)
    ang = pos[:, None] * inv[None, :]
    cos, sin = jnp.cos(ang), jnp.sin(ang)
    jj = np.arange(LANES) % head_dim
    first = jnp.asarray(jj < half)
    second = jnp.asarray((jj >= half) & (jj < rot))
    fidx = jnp.asarray(np.where(jj < half, jj, np.where(jj < rot, jj - half, 0)))
    cl, sl = cos[:, fidx], sin[:, fidx]
    c = jnp.where(first | second, cl, 1.0)
    s1 = jnp.where(second, sl, 0.0)
    s2 = jnp.where(first, -sl, 0.0)
    return c, s1, s2


def kernel(x, meta_tokens, mix_norm_g, w_in, hgrn_lb_logits, hgrn_norm_g, idx_k_norm_g, w_branch_hgrn,
           w_branch_dsa, w_out, ffn_norm_g, router_group_w, router_group_b, router_expert_w,
           router_expert_b, w_expert_gate, w_expert_up, w_expert_down, final_norm_g):
    batch, seq, _ = x.shape
    depth = w_in.shape[0]
    n_valid = CHUNK + seq
    assert seq % CHUNK == 0
    lp = -(-n_valid // Q_BLOCK) * Q_BLOCK
    n_chunks = n_valid // CHUNK
    topk = min(TOPK_MAX, (N_META + seq) // 4)

    meta = jnp.broadcast_to(meta_tokens[None].astype(x.dtype), (batch, N_META, D_MODEL))
    h = jnp.concatenate([jnp.zeros((batch, FRONT, D_MODEL), x.dtype), meta, x,
                         jnp.zeros((batch, lp - n_valid, D_MODEL), x.dtype)], axis=1)
    h = h.reshape(batch * lp, D_MODEL)

    pos = jnp.clip(jnp.arange(lp, dtype=jnp.int32) - FRONT, 0, N_META + seq - 1).astype(F32)
    tabs = jnp.stack(_rope_lane_tables(pos, ATT_HEAD_DIM) + _rope_lane_tables(pos, IDX_HEAD_DIM))

    p = jax.nn.softmax(hgrn_lb_logits.astype(F32), axis=0)
    cs = jnp.cumsum(p, axis=0)
    lower_bounds = cs - cs[0:1]

    rep = np.zeros((LANES, LANES), np.float32)
    for slot in range(IDX_HEADS_PER_LANE_GROUP):
        rep[np.arange(IDX_HEAD_DIM), slot * IDX_HEAD_DIM + np.arange(IDX_HEAD_DIM)] = 1.0
    rep = jnp.asarray(rep, BF16)
    ltri = jnp.asarray(np.tril(np.ones((Q_BLOCK, Q_BLOCK), np.float32)), BF16)

    pad_cols = jnp.zeros((D_MODEL, IDX_PAD - IDX_RAW), F32)
    for layer in range(depth):
        w = w_in[layer]
        w_p = jnp.concatenate([w[:, :COL_IDX + IDX_RAW], pad_cols, w[:, COL_IDX + IDX_RAW:]], axis=1).astype(BF16)
        ikg = jnp.pad(idx_k_norm_g[layer].astype(F32), (0, LANES - IDX_HEAD_DIM))[None]
        hg, gates, q, k, v, iq, ik, iw = _inproj(h, mix_norm_g[layer][None].astype(F32), w_p, tabs, ikg, rep, lp)
        ya = _hgrn(hg, lower_bounds[layer][None], hgrn_norm_g[layer][None].astype(F32), batch, lp, n_chunks)
        yb = _dsa(q, k, v, iq, ik, iw, ltri, batch, lp, topk, n_valid)
        wr = jnp.concatenate([router_expert_w[layer], router_group_w[layer],
                              jnp.zeros((D_MODEL, LANES - N_EXPERTS - N_GROUPS), F32)], axis=1)
        wr_hi = wr.astype(BF16)
        wr = jnp.concatenate([wr_hi, (wr - wr_hi.astype(F32)).astype(BF16)], axis=1)
        br = jnp.concatenate([router_expert_b[layer], router_group_b[layer],
                              jnp.zeros((LANES - N_EXPERTS - N_GROUPS,), F32)])[None]
        h, xc, route, counts = _merge(ya, yb, gates, h, w_branch_hgrn[layer].astype(BF16),
                                      w_branch_dsa[layer].astype(BF16), w_out[layer].astype(BF16),
                                      ffn_norm_g[layer][None].astype(F32), wr, br)
        wg, wu, wd = _expert_weights(w_expert_gate.reshape(-1, D_MODEL, EXPERT_FF),
                                     w_expert_up.reshape(-1, D_MODEL, EXPERT_FF),
                                     w_expert_down.reshape(-1, EXPERT_FF, D_MODEL), layer)
        h = _moe(xc, route, counts, h, wg, wu, wd)
    return _final_norm(h, final_norm_g[None].astype(F32), batch, lp, seq)
```

```python
import functools

import jax
import jax.numpy as jnp
import numpy as np
from jax import lax
from jax.experimental import pallas as pl
from jax.experimental.pallas import tpu as pltpu

F32 = jnp.float32
BF16 = jnp.bfloat16

D_MODEL = 1024
CHUNK = 64
N_META = 16
FRONT = CHUNK - N_META
ROPE_THETA = 500000.0
NORM_EPS = 1e-6
LANES = 128
SUBLANES = 8

HGRN_HEADS = 4
HGRN_HEAD_DIM = 128
HGRN_WIDTH = HGRN_HEADS * HGRN_HEAD_DIM
SUB = 8
HGRN_MAX_GROUP = 11

ATT_HEADS = 8
ATT_KV_HEADS = 2
ATT_HEAD_DIM = 64
ATT_GROUP = ATT_HEADS // ATT_KV_HEADS
ATT_WIDTH = ATT_HEADS * ATT_HEAD_DIM
ATT_KV_WIDTH = ATT_KV_HEADS * ATT_HEAD_DIM
ATT_ROPE_HALF = ATT_HEAD_DIM // 4 // 2
IDX_HEADS = 8
IDX_HEAD_DIM = 32
IDX_WIDTH = IDX_HEADS * IDX_HEAD_DIM
IDX_ROPE_HALF = IDX_HEAD_DIM // 4 // 2
IDX_HEADS_PER_LANE_GROUP = LANES // IDX_HEAD_DIM
TOPK_MAX = 256
Q_BLOCK = 128
DSA_EXTENT_STEP = 2
FOLD = 64
ONES_ROWS = 16

N_GROUPS = 4
EXPERTS_PER_GROUP = 8
N_EXPERTS = N_GROUPS * EXPERTS_PER_GROUP
EXPERT_FF = D_MODEL // 4
MERGE_TILE = 512
MOE_SLOT_TILE = 512
ROW_DMA_UNROLL = 8

COL_HG = 0
COL_ATT = 4 * HGRN_WIDTH
COL_IDX = COL_ATT + ATT_WIDTH + 2 * ATT_KV_WIDTH
IDX_RAW = IDX_WIDTH + IDX_HEAD_DIM + IDX_HEADS
IDX_PAD = IDX_WIDTH + LANES
COL_GATE = COL_IDX + IDX_PAD
IN_WIDTH_PAD = COL_GATE + 2 * D_MODEL

INT_MIN = np.int32(-2 ** 31)
NEG_FLT_MAX_KEY = np.int32(-2 ** 31 + 0x800000)
NEG_INF = float("-inf")
POS_INF = float("inf")

VMEM_LIMIT = 56 * 1024 * 1024


def _dot(a, b):
    return jnp.dot(a, b, preferred_element_type=F32)


def _dot_nt(a, b):
    return lax.dot_general(a, b, (((1,), (1,)), ((), ())), preferred_element_type=F32)


def _rope(x, c, s1, s2, half):
    return x * c + pltpu.roll(x, half, 1) * s1 + pltpu.roll(x, LANES - half, 1) * s2


def _inproj_kernel(h_ref, g_ref, w_ref, tab_ref, ikg_ref, rep_ref,
                   hg_ref, gate_ref, q_ref, k_ref, v_ref, iq_ref, ik_ref, iw_ref):
    x = h_ref[...]
    xn = (x * lax.rsqrt(jnp.mean(x * x, axis=-1, keepdims=True) + NORM_EPS) * g_ref[...]).astype(BF16)
    hg_ref[...] = _dot(xn, w_ref[:, COL_HG:COL_ATT])
    gate_ref[...] = _dot(xn, w_ref[:, COL_GATE:IN_WIDTH_PAD])
    att = _dot(xn, w_ref[:, COL_ATT:COL_IDX])
    idx = _dot(xn, w_ref[:, COL_IDX:COL_GATE])
    ca, s1a, s2a = tab_ref[0], tab_ref[1], tab_ref[2]
    ci, s1i, s2i = tab_ref[3], tab_ref[4], tab_ref[5]
    scale = ATT_HEAD_DIM ** -0.5
    for m in range(ATT_WIDTH // LANES):
        sl = slice(m * LANES, (m + 1) * LANES)
        q_ref[:, sl] = (_rope(att[:, sl], ca, s1a, s2a, ATT_ROPE_HALF) * scale).astype(BF16)
    k_ref[...] = _rope(att[:, ATT_WIDTH:ATT_WIDTH + LANES], ca, s1a, s2a, ATT_ROPE_HALF).astype(BF16)
    v_ref[...] = att[:, ATT_WIDTH + LANES:].astype(BF16)
    for m in range(IDX_WIDTH // LANES):
        sl = slice(m * LANES, (m + 1) * LANES)
        iq_ref[:, sl] = _rope(idx[:, sl], ci, s1i, s2i, IDX_ROPE_HALF).astype(BF16)
    xk = idx[:, IDX_WIDTH:]
    lane = lax.broadcasted_iota(jnp.int32, xk.shape, 1)
    ms = jnp.sum(jnp.where(lane < IDX_HEAD_DIM, xk * xk, 0.0), axis=-1, keepdims=True) / IDX_HEAD_DIM
    ikn = xk * lax.rsqrt(ms + NORM_EPS) * ikg_ref[...]
    ikr = _rope(ikn, ci, s1i, s2i, IDX_ROPE_HALF).astype(BF16)
    ik_ref[...] = _dot(ikr, rep_ref[...]).astype(BF16)
    iw_ref[...] = xk * (IDX_HEADS ** -0.5 * IDX_HEAD_DIM ** -0.5)


def _inproj(h, g, w, tabs, ikg, rep, lp):
    n = h.shape[0]
    tm = lp // 8
    per = lp // tm
    row = lambda i: (i, 0)
    const2 = lambda i: (0, 0)
    out_shape = (
        jax.ShapeDtypeStruct((n, 4 * HGRN_WIDTH), F32),
        jax.ShapeDtypeStruct((n, 2 * D_MODEL), F32),
        jax.ShapeDtypeStruct((n, ATT_WIDTH), BF16),
        jax.ShapeDtypeStruct((n, LANES), BF16),
        jax.ShapeDtypeStruct((n, LANES), BF16),
        jax.ShapeDtypeStruct((n, IDX_WIDTH), BF16),
        jax.ShapeDtypeStruct((n, LANES), BF16),
        jax.ShapeDtypeStruct((n, LANES), F32),
    )
    return pl.pallas_call(
        _inproj_kernel,
        grid=(n // tm,),
        in_specs=[
            pl.BlockSpec((tm, D_MODEL), row),
            pl.BlockSpec((1, D_MODEL), const2),
            pl.BlockSpec((D_MODEL, IN_WIDTH_PAD), const2),
            pl.BlockSpec((6, tm, LANES), lambda i: (0, i % per, 0)),
            pl.BlockSpec((1, LANES), const2),
            pl.BlockSpec((LANES, LANES), const2),
        ],
        out_specs=tuple(pl.BlockSpec((tm, s.shape[1]), row) for s in out_shape),
        out_shape=out_shape,
        compiler_params=pltpu.CompilerParams(
            dimension_semantics=("arbitrary",), vmem_limit_bytes=VMEM_LIMIT),
        name="inproj",
    )(h, g, w, tabs, ikg, rep)


def _hgrn_kernel(q_ref, f_ref, i_ref, g_ref, lb_ref, ng_ref, o_ref, st_ref, kb_ref, bb_ref, vb_ref,
                 *, n_chunks, group):
    C = CHUNK
    D = HGRN_HEAD_DIM
    st_ref[...] = jnp.zeros_like(st_ref)
    kb_ref[...] = jnp.zeros_like(kb_ref)
    bb_ref[...] = jnp.zeros_like(bb_ref)
    vb_ref[...] = jnp.zeros_like(vb_ref)
    lb = lb_ref[...]
    ng = ng_ref[...]
    R = group * C
    tri_row = lax.broadcasted_iota(jnp.int32, (C, C), 0)
    tri_col = lax.broadcasted_iota(jnp.int32, (C, C), 1)
    tri = (tri_row >= tri_col).astype(F32)
    local_row = lax.broadcasted_iota(jnp.int32, (R, 1), 0)
    sub_pos = local_row & (SUB - 1)

    def chunk(x, u):
        return x[u * C:(u + 1) * C, :]

    def body(it, carry):
        rows = pl.ds(pl.multiple_of(it * R, C), R)
        z = f_ref[rows, :]
        q = q_ref[rows, :]
        v = i_ref[rows, :]
        go = g_ref[rows, :]
        pad = (it * R + local_row) < FRONT
        f = lb + (1.0 - lb) * jax.nn.sigmoid(z)
        logf = jnp.where(pad, 0.0, jnp.log(f))
        k = jnp.where(pad, 0.0, (1.0 - lb) * jax.nn.sigmoid(-z))
        v = jnp.where(pad, 0.0, v)
        b = jnp.concatenate(
            [jnp.dot(tri, chunk(logf, u), preferred_element_type=F32, precision=lax.Precision.HIGHEST)
             for u in range(group)], axis=0)
        b3 = b.reshape(group, C, D)
        b_last = b3[:, C - 1:C, :]
        qd = (q * jnp.exp(b)).astype(BF16)
        kd = (k.reshape(group, C, D) * jnp.exp(b_last - b3)).reshape(R, D).astype(BF16)
        e_last = jnp.exp(b_last)
        upd =[_dot(chunk(v, u).T.astype(BF16), chunk(kd, u)) for u in range(group)]

        st = st_ref[...]
        before = []
        for u in range(group):
            before.append(st.astype(BF16))
            st = st * e_last[u] + upd[u]
        st_ref[...] = st
        o = jnp.concatenate([_dot_nt(chunk(qd, u), before[u]) for u in range(group)], axis=0)

        kb_ref[SUB:, :] = k
        bb_ref[SUB:, :] = b
        vb_ref[SUB:, :] = v
        for j in range(SUB):
            ks = kb_ref[SUB - j:SUB - j + R, :]
            bs = bb_ref[SUB - j:SUB - j + R, :]
            vs = vb_ref[SUB - j:SUB - j + R, :]
            w = jnp.exp(jnp.minimum(b - bs, 0.0))
            a = jnp.sum(q * ks * w, axis=-1, keepdims=True)
            o = o + jnp.where(sub_pos >= j, a, 0.0) * vs

        att = [jnp.zeros((C, C), F32) for _ in range(group)]
        span = C
        while span > SUB:
            half = span // 2
            upper = (local_row & (span - 1)) >= half
            bs3 = b.reshape(R // span, span, D)
            mid = bs3[:, half - 1:half, :]
            qs = jnp.where(upper, (q.reshape(bs3.shape) * jnp.exp(jnp.minimum(bs3 - mid, 0.0))).reshape(R, D), 0.0)
            ks = jnp.where(upper, 0.0, (k.reshape(bs3.shape) * jnp.exp(jnp.minimum(mid - bs3, 0.0))).reshape(R, D))
            qs, ks = qs.astype(BF16), ks.astype(BF16)
            same_block = (tri_row & -span) == (tri_col & -span)
            att = [att[u] + jnp.where(same_block, _dot_nt(chunk(qs, u), chunk(ks, u)), 0.0) for u in range(group)]
            span = half
        o = o + jnp.concatenate([_dot(att[u].astype(BF16), chunk(v, u).astype(BF16)) for u in range(group)], axis=0)

        on = o * lax.rsqrt(jnp.mean(o * o, axis=-1, keepdims=True) + NORM_EPS) * ng
        o_ref[rows, :] = (on * (go * jax.nn.sigmoid(go))).astype(o_ref.dtype)
        return carry

    lax.fori_loop(0, n_chunks // group, body, 0)
    tail = o_ref.shape[0] - n_chunks * C
    if tail:
        o_ref[n_chunks * C:, :] = jnp.zeros((tail, D), o_ref.dtype)


def _hgrn(hg, lb, ng, batch, lp, n_chunks):
    n = hg.shape[0]
    H, D = HGRN_HEADS, HGRN_HEAD_DIM
    group = max(g for g in range(1, HGRN_MAX_GROUP + 1) if n_chunks % g == 0)

    def col(k):
        return pl.BlockSpec((lp, D), lambda b, h: (b, k * H + h))

    vec = pl.BlockSpec((1, D), lambda b, h: (0, h))
    return pl.pallas_call(
        functools.partial(_hgrn_kernel, n_chunks=n_chunks, group=group),
        grid=(batch, H),
        in_specs=[col(0), col(1), col(2), col(3), vec, vec],
        out_specs=pl.BlockSpec((lp, D), lambda b, h: (b, h)),
        out_shape=jax.ShapeDtypeStruct((n, HGRN_WIDTH), BF16),
        scratch_shapes=[
            pltpu.VMEM((D, D), F32),
            pltpu.VMEM((group * CHUNK + SUB, D), F32),
            pltpu.VMEM((group * CHUNK + SUB, D), F32),
            pltpu.VMEM((group * CHUNK + SUB, D), F32),
        ],
        compiler_params=pltpu.CompilerParams(
            dimension_semantics=("arbitrary", "arbitrary"), vmem_limit_bytes=VMEM_LIMIT),
        name="hgrn2",
    )(hg, hg, hg, hg, lb, ng)


def _sortable_to_f32(key):
    bits = jnp.where(key < 0, key ^ jnp.int32(0x7FFFFFFF), key)
    return pltpu.bitcast(bits, F32)


def _fold_rows(x, op):
    rows = x.shape[0]
    if rows > FOLD and rows % FOLD == 0:
        x = op(x.reshape(rows // FOLD, FOLD, x.shape[1]), axis=0)
    return op(x, axis=0, keepdims=True)


def _dsa_kernel(q_ref, iq_ref, iw_ref, k_ref, v_ref, ik_ref, ltri_ref, o_ref,
                sc_ref, key_ref, hi_ref, vt_ref, iq8_ref, qg_ref, ot_ref, *, topk, n_valid, extents):
    QB = Q_BLOCK
    lp = k_ref.shape[0]
    t = pl.program_id(1)
    kf = float(topk)

    @pl.when(t == 0)
    def _():
        for kb in range(lp // QB):
            sl = slice(kb * QB, (kb + 1) * QB)
            vt = v_ref[sl, :].astype(F32).T.astype(BF16)
            for g in range(ATT_KV_HEADS):
                vt_ref[g, :ATT_HEAD_DIM, sl] = vt[g * ATT_HEAD_DIM:(g + 1) * ATT_HEAD_DIM, :]
        for g in range(ATT_KV_HEADS):
            vt_ref[g, ATT_HEAD_DIM:, :] = jnp.ones((ONES_ROWS, lp), BF16)

    lane = lax.broadcasted_iota(jnp.int32, (QB, LANES), 1)
    iq = iq_ref[...]
    for h in range(IDX_HEADS):
        grp, slot = divmod(h, IDX_HEADS_PER_LANE_GROUP)
        part = iq[:, grp * LANES:(grp + 1) * LANES]
        iq8_ref[h * QB:(h + 1) * QB, :] = jnp.where((lane >> 5) == slot, part, jnp.zeros_like(part))
    wt = iw_ref[...].T
    lane_half = lane >> 6
    for h in range(ATT_HEADS):
        g, r = divmod(h, ATT_GROUP)
        m, p = divmod(h, LANES // ATT_HEAD_DIM)
        qh = q_ref[:, m * LANES:(m + 1) * LANES].astype(F32)
        if p != g:
            qh = pltpu.roll(qh, ATT_HEAD_DIM, 1)
        qg_ref[g, r * QB:(r + 1) * QB, :] = jnp.where(lane_half == g, qh, 0.0).astype(BF16)

    qpos = t * QB + lax.broadcasted_iota(jnp.int32, (1, QB), 1)
    qchunk = qpos >> 6
    n_adm = jnp.minimum((qchunk + 1) * CHUNK, n_valid) - FRONT
    take_all = n_adm <= topk
    ltri = ltri_ref[...]

    def tile_body(n_blocks):
        ke = n_blocks * QB
        keys = slice(0, ke)
        kpos = lax.broadcasted_iota(jnp.int32, (ke, 1), 0)
        adm = ((kpos >> 6) <= qchunk) & (kpos >= FRONT) & (kpos < n_valid)

        ik = ik_ref[keys, :]
        acc = jnp.zeros((ke, QB), F32)
        for h in range(IDX_HEADS):
            rel = jnp.maximum(_dot_nt(ik, iq8_ref[h * QB:(h + 1) * QB, :]), 0.0)
            acc = acc + rel * wt[IDX_HEAD_DIM + h:IDX_HEAD_DIM + h + 1, :]
        acc = jnp.where(acc == 0.0, 0.0, acc)
        sc_ref[keys, :] = jnp.where(adm, acc, NEG_INF)

        def counts(cand):
            x = sc_ref[keys, :]
            return (_fold_rows(jnp.where(x >= cand, 1.0, 0.0), jnp.sum),
                    _fold_rows(jnp.where(x > cand, 1.0, 0.0), jnp.sum))

        def count16(ref, cand, strict=False):
            x = ref[keys, :]
            hit = jnp.where((x > cand) if strict else (x >= cand), jnp.ones((), BF16), jnp.zeros((), BF16))
            parts = [hit[i * FOLD:(i + 1) * FOLD, :] for i in range(ke // FOLD)]
            while len(parts) > 1:
                parts = [a + b for a, b in zip(parts[::2], parts[1::2])] + parts[len(parts) & ~1:]
            return jnp.sum(parts[0].astype(F32), axis=0, keepdims=True)

        def pattern16(v):
            return pltpu.bitcast(jnp.where(v < 0, v ^ 0x7FFF, v) << 16, F32).astype(BF16)

        def digit16(v):
            return v.astype(F32).astype(BF16)

        def bisect(ref, width, start, image, want):
            def step(i, t):
                cand = t + lax.shift_left(jnp.int32(1), width - 1 - i)
                return jnp.where(count16(ref, image(cand)) >= want, cand, t)

            t = lax.fori_loop(0, width, step, jnp.full((1, QB), start, jnp.int32))
            return t, want - count16(ref, image(t), strict=True)

        bits = pltpu.bitcast(sc_ref[keys, :], jnp.int32)
        key = jnp.where(bits < 0, bits ^ 0x7FFFFFFF, bits)
        key_ref[keys, :] = key
        hi_ref[keys, :] = pattern16(key >> 16)
        top, want = bisect(hi_ref, 16, -(2 ** 15), pattern16, kf)
        key = key_ref[keys, :]
        in_bucket = (key >> 16) == top
        hi_ref[keys, :] = digit16(jnp.where(in_bucket, (key >> 8) & 0xFF, -1))
        mid, want = bisect(hi_ref, 8, 0, digit16, want)
        key = key_ref[keys, :]
        in_bucket = (key >> 8) == ((top << 8) | mid)
        hi_ref[keys, :] = digit16(jnp.where(in_bucket, key & 0xFF, -1))
        low, _ = bisect(hi_ref, 8, 0, digit16, want)
        tkey = (top << 16) | (mid << 8) | low
        thr = _sortable_to_f32(jnp.where(take_all, NEG_FLT_MAX_KEY, tkey))
        c_ge, c_gt = counts(thr)

        def unsettled(c_ge, c_gt):
            live = jnp.logical_not(take_all)
            return (c_gt >= kf) & live, (c_ge < kf) & live

        def cond(carry):
            up, dn = unsettled(carry[1], carry[2])
            return jnp.max(jnp.where(up | dn, 1.0, 0.0)) > 0.0

        def fix(carry):
            thr, c_ge, c_gt = carry
            up, dn = unsettled(c_ge, c_gt)
            x = sc_ref[keys, :]
            above = _fold_rows(jnp.where(x > thr, x, POS_INF), jnp.min)
            below = _fold_rows(jnp.where(x < thr, x, NEG_INF), jnp.max)
            thr = jnp.where(up, above, jnp.where(dn, below, thr))
            return (thr,) + counts(thr)

        thr, c_ge, c_gt = lax.while_loop(cond, fix, (thr, c_ge, c_gt))

        need = kf - c_gt
        seen = jnp.zeros((1, QB), F32)
        for kb in range(n_blocks):
            blk = slice(kb * QB, (kb + 1) * QB)
            x = sc_ref[blk, :]
            eq = x == thr
            pre = _dot(ltri, jnp.where(eq, 1.0, 0.0).astype(BF16))
            take = (x > thr) | (eq & (pre + seen <= need))
            sc_ref[blk, :] = jnp.where(take, 0.0, NEG_INF)
            seen = seen + pre[QB - 1:QB, :]

        bias = sc_ref[keys, :]
        bias = jnp.concatenate([bias] * ATT_GROUP, axis=1)
        kk = k_ref[keys, :]
        for g in range(ATT_KV_HEADS):
            s = _dot_nt(kk, qg_ref[g]) + bias
            e = jnp.exp((s - _fold_rows(s, jnp.max)).astype(BF16))
            pv = _dot(vt_ref[g, :, keys], e)
            pv = pv[:ATT_HEAD_DIM, :] / pv[ATT_HEAD_DIM:ATT_HEAD_DIM + 1, :]
            for r in range(ATT_GROUP):
                h = g * ATT_GROUP + r
                ot_ref[h * ATT_HEAD_DIM:(h + 1) * ATT_HEAD_DIM, :] = pv[:, r * QB:(r + 1) * QB]
        o_ref[...] = ot_ref[...].T.astype(o_ref.dtype)

    lo = 0
    for n_blocks in extents:
        pl.when((t >= lo) & (t < n_blocks))(functools.partial(tile_body, n_blocks))
        lo = n_blocks


def _dsa(q, k, v, iq, ik, iw, ltri, batch, lp, topk, n_valid):
    n = q.shape[0]
    nq = lp // Q_BLOCK
    extents = tuple(sorted({nq - DSA_EXTENT_STEP * i for i in range(-(-nq // DSA_EXTENT_STEP))}))
    qblk = lambda b, i: (b * nq + i, 0)
    bat = lambda b, i: (b, 0)
    return pl.pallas_call(
        functools.partial(_dsa_kernel, topk=topk, n_valid=n_valid, extents=extents),
        grid=(batch, nq),
        in_specs=[
            pl.BlockSpec((Q_BLOCK, ATT_WIDTH), qblk),
            pl.BlockSpec((Q_BLOCK, IDX_WIDTH), qblk),
            pl.BlockSpec((Q_BLOCK, LANES), qblk),
            pl.BlockSpec((lp, LANES), bat),
            pl.BlockSpec((lp, LANES), bat),
            pl.BlockSpec((lp, LANES), bat),
            pl.BlockSpec((Q_BLOCK, Q_BLOCK), lambda b, i: (0, 0)),
        ],
        out_specs=pl.BlockSpec((Q_BLOCK, ATT_WIDTH), qblk),
        out_shape=jax.ShapeDtypeStruct((n, ATT_WIDTH), BF16),
        scratch_shapes=[
            pltpu.VMEM((lp, Q_BLOCK), F32),
            pltpu.VMEM((lp, Q_BLOCK), jnp.int32),
            pltpu.VMEM((lp, Q_BLOCK), BF16),
            pltpu.VMEM((ATT_KV_HEADS, ATT_HEAD_DIM + ONES_ROWS, lp), BF16),
            pltpu.VMEM((IDX_HEADS * Q_BLOCK, LANES), BF16),
            pltpu.VMEM((ATT_KV_HEADS, ATT_GROUP * Q_BLOCK, LANES), BF16),
            pltpu.VMEM((ATT_WIDTH, Q_BLOCK), F32),
        ],
        compiler_params=pltpu.CompilerParams(
            dimension_semantics=("arbitrary", "arbitrary"), vmem_limit_bytes=VMEM_LIMIT),
        name="dsa",
    )(q, iq, iw, k, v, ik, ltri)


def _merge_kernel(ya_ref, yb_ref, gate_ref, h_ref, wa_ref, wb_ref, wo_ref, fg_ref, wr_ref, br_ref, ltri_ref,
                  ho_ref, xc_ref, route_ref, cnt_ref):
    @pl.when(pl.program_id(0) == 0)
    def _():
        cnt_ref[...] = jnp.zeros_like(cnt_ref)

    gate = gate_ref[...]
    pa = _dot(ya_ref[...], wa_ref[...])
    pb = _dot(yb_ref[...], wb_ref[...])
    merged = jax.nn.sigmoid(gate[:, :D_MODEL]) * pa + jax.nn.sigmoid(gate[:, D_MODEL:]) * pb
    h = h_ref[...] + _dot(merged.astype(BF16), wo_ref[...])
    ho_ref[...] = h
    xn = h * lax.rsqrt(jnp.mean(h * h, axis=-1, keepdims=True) + NORM_EPS) * fg_ref[...]
    xh = xn.astype(BF16)
    xc_ref[:, :D_MODEL] = xh.astype(F32)

    xl = (xn - xh.astype(F32)).astype(BF16)
    r = _dot(xh, wr_ref[...])
    lg = r[:, :LANES] + r[:, LANES:] + _dot(xl, wr_ref[:, :LANES]) + br_ref[...]
    lane = lax.broadcasted_iota(jnp.int32, lg.shape, 1)
    lane_f = lane.astype(F32)
    big = float(LANES)
    is_g = (lane >= N_EXPERTS) & (lane < N_EXPERTS + N_GROUPS)
    gl = jnp.where(is_g, lg, NEG_INF)
    gmax = jnp.max(gl, axis=-1, keepdims=True)
    pg_top = 1.0 / jnp.sum(jnp.exp(gl - gmax), axis=-1, keepdims=True)
    g_lane = jnp.min(jnp.where(gl == gmax, lane_f, big), axis=-1, keepdims=True)
    e_lo = (g_lane - N_EXPERTS) * EXPERTS_PER_GROUP
    in_grp = (lane_f >= e_lo) & (lane_f < e_lo + EXPERTS_PER_GROUP)
    el = jnp.where(in_grp, lg, NEG_INF)
    ee = jnp.exp(el - jnp.max(el, axis=-1, keepdims=True))
    pe = ee / jnp.sum(ee, axis=-1, keepdims=True)
    pe = jnp.where(in_grp, pe, -1.0)
    p1 = jnp.max(pe, axis=-1, keepdims=True)
    i1 = jnp.min(jnp.where(pe == p1, lane_f, big), axis=-1, keepdims=True)
    pe2 = jnp.where(lane_f == i1, -1.0, pe)
    p2 = jnp.max(pe2, axis=-1, keepdims=True)
    i2 = jnp.min(jnp.where(pe2 == p2, lane_f, big), axis=-1, keepdims=True)
    tot = p1 + p2
    xc_ref[:, D_MODEL:] = jnp.where(lane_f == i1, p1 / tot * pg_top,
                                    jnp.where(lane_f == i2, p2 / tot * pg_top, 0.0))

    grp = g_lane - N_EXPERTS
    onehot = jnp.where(lane_f == grp, 1.0, 0.0)
    incl = _dot(ltri_ref[...], onehot.astype(BF16))
    seen = cnt_ref[0:1, :]
    rank = jnp.sum(onehot * (incl - 1.0 + seen), axis=-1, keepdims=True)
    route_ref[...] = jnp.where(lane == 0, grp, jnp.where(lane == 1, rank, 0.0))
    cnt_ref[...] = jnp.broadcast_to(seen + incl[incl.shape[0] - 1:, :], cnt_ref.shape)


def _merge(ya, yb, gates, h, wa, wb, wo, fg, wr, br):
    n = h.shape[0]
    tm = min(MERGE_TILE, n)
    ltri = jnp.asarray(np.tril(np.ones((tm, tm), np.float32)), BF16)
    row = lambda i: (i, 0)
    const = lambda i: (0, 0)
    return pl.pallas_call(
        _merge_kernel,
        grid=(n // tm,),
        in_specs=[
            pl.BlockSpec((tm, HGRN_WIDTH), row),
            pl.BlockSpec((tm, ATT_WIDTH), row),
            pl.BlockSpec((tm, 2 * D_MODEL), row),
            pl.BlockSpec((tm, D_MODEL), row),
            pl.BlockSpec((HGRN_WIDTH, D_MODEL), const),
            pl.BlockSpec((ATT_WIDTH, D_MODEL), const),
            pl.BlockSpec((D_MODEL, D_MODEL), const),
            pl.BlockSpec((1, D_MODEL), const),
            pl.BlockSpec((D_MODEL, 2 * LANES), const),
            pl.BlockSpec((1, LANES), const),
            pl.BlockSpec((tm, tm), const),
        ],
        out_specs=(pl.BlockSpec((tm, D_MODEL), row), pl.BlockSpec((tm, D_MODEL + LANES), row),
                   pl.BlockSpec((tm, LANES), row), pl.BlockSpec((SUBLANES, LANES), const)),
        out_shape=(jax.ShapeDtypeStruct((n, D_MODEL), F32), jax.ShapeDtypeStruct((n, D_MODEL + LANES), F32),
                   jax.ShapeDtypeStruct((n, LANES), F32), jax.ShapeDtypeStruct((SUBLANES, LANES), F32)),
        compiler_params=pltpu.CompilerParams(
            dimension_semantics=("arbitrary",), vmem_limit_bytes=VMEM_LIMIT),
        name="merge_router",
    )(ya, yb, gates, h, wa, wb, wo, fg, wr, br, ltri)


def _row_copies(n_rows, make_copy, whole_copy):
    def issue(r, c):
        make_copy(r).start()
        return c

    lax.fori_loop(0, n_rows, issue, 0, unroll=ROW_DMA_UNROLL)
    whole_copy.wait()


def _dispatch_kernel(dest_ref, x_ref, zeros_ref, xs_ref, sem):
    del zeros_ref
    tm = x_ref.shape[0]

    def row_copy(r):
        return pltpu.make_async_copy(x_ref.at[pl.ds(r, 1)], xs_ref.at[pl.ds(dest_ref[0, 0, r], 1)], sem)

    _row_copies(tm, row_copy, pltpu.make_async_copy(x_ref, xs_ref.at[pl.ds(0, tm)], sem))


def _experts_kernel(tile_group_ref, tile_on_ref, xs_ref, wg_ref, wu_ref, wd_ref, ys_ref):
    t = pl.program_id(0)
    tm = xs_ref.shape[0]

    @pl.when(tile_on_ref[t] == 1)
    def _():
        x = xs_ref[:, :D_MODEL].astype(BF16)
        c = xs_ref[:, D_MODEL:]
        lane = lax.broadcasted_iota(jnp.int32, c.shape, 1)
        first = tile_group_ref[t] * EXPERTS_PER_GROUP
        scale = jnp.concatenate(
            [jnp.broadcast_to(jnp.sum(jnp.where(lane == first + e, c, 0.0), axis=-1, keepdims=True), (tm, EXPERT_FF))
             for e in range(EXPERTS_PER_GROUP)], axis=1)
        a = _dot(x, wg_ref[0])
        u = _dot(x, wu_ref[0])
        act = (a * jax.nn.sigmoid(a)) * u * scale
        ys_ref[...] = _dot(act.astype(BF16), wd_ref[0])

    @pl.when(tile_on_ref[t] == 0)
    def _():
        ys_ref[...] = jnp.zeros_like(ys_ref)


def _combine_kernel(dest_ref, h_ref, ys_ref, o_ref, buf_ref, sem):
    tm = h_ref.shape[0]

    def row_copy(r):
        return pltpu.make_async_copy(ys_ref.at[pl.ds(dest_ref[0, 0, r], 1)], buf_ref.at[pl.ds(r, 1)], sem)

    _row_copies(tm, row_copy, pltpu.make_async_copy(ys_ref.at[pl.ds(0, tm)], buf_ref, sem))
    o_ref[...] = h_ref[...] + buf_ref[...]


def _expert_weights_kernel(g_ref, u_ref, d_ref, go_ref, uo_ref, do_ref):
    go_ref[0] = g_ref[0].astype(BF16)
    uo_ref[0] = u_ref[0].astype(BF16)
    do_ref[0] = d_ref[0].astype(BF16)


def _expert_weights(w_gate, w_up, w_down, layer):
    per = EXPERTS_PER_GROUP
    src = lambda shape: pl.BlockSpec((1,) + shape, lambda e: (layer * N_EXPERTS + e, 0, 0))
    side = pl.BlockSpec((1, D_MODEL, EXPERT_FF), lambda e: (e // per, 0, e % per))
    stack = pl.BlockSpec((1, EXPERT_FF, D_MODEL), lambda e: (e // per, e % per, 0))
    return pl.pallas_call(
        _expert_weights_kernel,
        grid=(N_EXPERTS,),
        in_specs=[src((D_MODEL, EXPERT_FF)), src((D_MODEL, EXPERT_FF)), src((EXPERT_FF, D_MODEL))],
        out_specs=(side, side, stack),
        out_shape=(jax.ShapeDtypeStruct((N_GROUPS, D_MODEL, per * EXPERT_FF), BF16),
                   jax.ShapeDtypeStruct((N_GROUPS, D_MODEL, per * EXPERT_FF), BF16),
                   jax.ShapeDtypeStruct((N_GROUPS, per * EXPERT_FF, D_MODEL), BF16)),
        compiler_params=pltpu.CompilerParams(
            dimension_semantics=("arbitrary",), vmem_limit_bytes=VMEM_LIMIT),
        name="expert_weights",
    )(w_gate, w_up, w_down)


def _moe(xc, route, counts, h, wg, wu, wd):
    n = h.shape[0]
    tm = min(MERGE_TILE, n)
    ts = min(MOE_SLOT_TILE, n)
    n_tiles = n // ts + N_GROUPS
    n_slots = n_tiles * ts

    grp = route[:, 0].astype(jnp.int32)
    rank = route[:, 1].astype(jnp.int32)
    cnt = counts[0, :N_GROUPS].astype(jnp.int32)
    padded = (cnt + ts - 1) // ts * ts
    ends = jnp.cumsum(padded)
    starts = ends - padded
    dest = (starts[grp] + rank).reshape(n // tm, 1, tm)
    tile_start = jnp.arange(n_tiles, dtype=jnp.int32) * ts
    tile_group = jnp.minimum(jnp.sum(tile_start[:, None] >= ends[None, :], axis=1), N_GROUPS - 1).astype(jnp.int32)
    tile_on = (tile_start < ends[N_GROUPS - 1]).astype(jnp.int32)

    tok = lambda i: (i, 0)
    smem_idx = pl.BlockSpec((1, 1, tm), lambda i: (i, 0, 0), memory_space=pltpu.SMEM)
    hbm = pl.BlockSpec(memory_space=pl.ANY)
    xs = pl.pallas_call(
        _dispatch_kernel,
        grid=(n // tm,),
        in_specs=[smem_idx, pl.BlockSpec((tm, D_MODEL + LANES), tok), hbm],
        out_specs=hbm,
        out_shape=jax.ShapeDtypeStruct((n_slots, D_MODEL + LANES), F32),
        scratch_shapes=[pltpu.SemaphoreType.DMA(())],
        input_output_aliases={2: 0},
        compiler_params=pltpu.CompilerParams(
            dimension_semantics=("arbitrary",), vmem_limit_bytes=VMEM_LIMIT),
        name="moe_dispatch",
    )(dest, xc, jnp.zeros((n_slots, D_MODEL + LANES), F32))

    wspec = lambda shape: pl.BlockSpec((1,) + shape, lambda t, tg, on: (tg[t], 0, 0))
    ys = pl.pallas_call(
        _experts_kernel,
        grid_spec=pltpu.PrefetchScalarGridSpec(
            num_scalar_prefetch=2,
            grid=(n_tiles,),
            in_specs=[pl.BlockSpec((ts, D_MODEL + LANES), lambda t, tg, on: (t, 0)),
                      wspec((D_MODEL, EXPERTS_PER_GROUP * EXPERT_FF)),
                      wspec((D_MODEL, EXPERTS_PER_GROUP * EXPERT_FF)),
                      wspec((EXPERTS_PER_GROUP * EXPERT_FF, D_MODEL))],
            out_specs=pl.BlockSpec((ts, D_MODEL), lambda t, tg, on: (t, 0)),
        ),
        out_shape=jax.ShapeDtypeStruct((n_slots, D_MODEL), F32),
        compiler_params=pltpu.CompilerParams(
            dimension_semantics=("arbitrary",), vmem_limit_bytes=VMEM_LIMIT),
        name="moe_experts",
    )(tile_group, tile_on, xs, wg, wu, wd)

    return pl.pallas_call(
        _combine_kernel,
        grid=(n // tm,),
        in_specs=[smem_idx, pl.BlockSpec((tm, D_MODEL), tok), hbm],
        out_specs=pl.BlockSpec((tm, D_MODEL), tok),
        out_shape=jax.ShapeDtypeStruct((n, D_MODEL), F32),
        scratch_shapes=[pltpu.VMEM((tm, D_MODEL), F32), pltpu.SemaphoreType.DMA(())],
        compiler_params=pltpu.CompilerParams(
            dimension_semantics=("arbitrary",), vmem_limit_bytes=VMEM_LIMIT),
        name="moe_combine",
    )(dest, h, ys)


def _final_kernel(h_ref, g_ref, o_ref, *, seq):
    step = 256 if seq % 256 == 0 else CHUNK
    for r in range(0, seq, step):
        x = h_ref[CHUNK + r:CHUNK + r + step, :]
        o_ref[0, r:r + step, :] = x * lax.rsqrt(jnp.mean(x * x, axis=-1, keepdims=True) + NORM_EPS) * g_ref[...]


def _final_norm(h, g, batch, lp, seq):
    return pl.pallas_call(
        functools.partial(_final_kernel, seq=seq),
        grid=(batch,),
        in_specs=[pl.BlockSpec((lp, D_MODEL), lambda b: (b, 0)), pl.BlockSpec((1, D_MODEL), lambda b: (0, 0))],
        out_specs=pl.BlockSpec((1, seq, D_MODEL), lambda b: (b, 0, 0)),
        out_shape=jax.ShapeDtypeStruct((batch, seq, D_MODEL), F32),
        compiler_params=pltpu.CompilerParams(
            dimension_semantics=("arbitrary",), vmem_limit_bytes=VMEM_LIMIT),
        name="final_norm",
    )(h, g)


def _rope_lane_tables(pos, head_dim):
    rot = head_dim // 4
    half = rot // 2
    inv = ROPE_THETA ** (-jnp.arange(0, rot, 2, dtype=F32) / rot)
    ang = pos[:, None] * inv[None, :]
    cos, sin = jnp.cos(ang), jnp.sin(ang)
    jj = np.arange(LANES) % head_dim
    first = jnp.asarray(jj < half)
    second = jnp.asarray((jj >= half) & (jj < rot))
    fidx = jnp.asarray(np.where(jj < half, jj, np.where(jj < rot, jj - half, 0)))
    cl, sl = cos[:, fidx], sin[:, fidx]
    c = jnp.where(first | second, cl, 1.0)
    s1 = jnp.where(second, sl, 0.0)
    s2 = jnp.where(first, -sl, 0.0)
    return c, s1, s2


def kernel(x, meta_tokens, mix_norm_g, w_in, hgrn_lb_logits, hgrn_norm_g, idx_k_norm_g, w_branch_hgrn,
           w_branch_dsa, w_out, ffn_norm_g, router_group_w, router_group_b, router_expert_w,
           router_expert_b, w_expert_gate, w_expert_up, w_expert_down, final_norm_g):
    batch, seq, _ = x.shape
    depth = w_in.shape[0]
    n_valid = CHUNK + seq
    assert seq % CHUNK == 0
    lp = -(-n_valid // Q_BLOCK) * Q_BLOCK
    n_chunks = n_valid // CHUNK
    topk = min(TOPK_MAX, (N_META + seq) // 4)

    meta = jnp.broadcast_to(meta_tokens[None].astype(x.dtype), (batch, N_META, D_MODEL))
    h = jnp.concatenate([jnp.zeros((batch, FRONT, D_MODEL), x.dtype), meta, x,
                         jnp.zeros((batch, lp - n_valid, D_MODEL), x.dtype)], axis=1)
    h = h.reshape(batch * lp, D_MODEL)

    pos = jnp.clip(jnp.arange(lp, dtype=jnp.int32) - FRONT, 0, N_META + seq - 1).astype(F32)
    tabs = jnp.stack(_rope_lane_tables(pos, ATT_HEAD_DIM) + _rope_lane_tables(pos, IDX_HEAD_DIM))

    p = jax.nn.softmax(hgrn_lb_logits.astype(F32), axis=0)
    cs = jnp.cumsum(p, axis=0)
    lower_bounds = cs - cs[0:1]

    rep = np.zeros((LANES, LANES), np.float32)
    for slot in range(IDX_HEADS_PER_LANE_GROUP):
        rep[np.arange(IDX_HEAD_DIM), slot * IDX_HEAD_DIM + np.arange(IDX_HEAD_DIM)] = 1.0
    rep = jnp.asarray(rep, BF16)
    ltri = jnp.asarray(np.tril(np.ones((Q_BLOCK, Q_BLOCK), np.float32)), BF16)

    pad_cols = jnp.zeros((D_MODEL, IDX_PAD - IDX_RAW), F32)
    for layer in range(depth):
        w = w_in[layer]
        w_p = jnp.concatenate([w[:, :COL_IDX + IDX_RAW], pad_cols, w[:, COL_IDX + IDX_RAW:]], axis=1).astype(BF16)
        ikg = jnp.pad(idx_k_norm_g[layer].astype(F32), (0, LANES - IDX_HEAD_DIM))[None]
        hg, gates, q, k, v, iq, ik, iw = _inproj(h, mix_norm_g[layer][None].astype(F32), w_p, tabs, ikg, rep, lp)
        ya = _hgrn(hg, lower_bounds[layer][None], hgrn_norm_g[layer][None].astype(F32), batch, lp, n_chunks)
        yb = _dsa(q, k, v, iq, ik, iw, ltri, batch, lp, topk, n_valid)
        wr = jnp.concatenate([router_expert_w[layer], router_group_w[layer],
                              jnp.zeros((D_MODEL, LANES - N_EXPERTS - N_GROUPS), F32)], axis=1)
        wr_hi = wr.astype(BF16)
        wr = jnp.concatenate([wr_hi, (wr - wr_hi.astype(F32)).astype(BF16)], axis=1)
        br = jnp.concatenate([router_expert_b[layer], router_group_b[layer],
                              jnp.zeros((LANES - N_EXPERTS - N_GROUPS,), F32)])[None]
        h, xc, route, counts = _merge(ya, yb, gates, h, w_branch_hgrn[layer].astype(BF16),
                                      w_branch_dsa[layer].astype(BF16), w_out[layer].astype(BF16),
                                      ffn_norm_g[layer][None].astype(F32), wr, br)
        wg, wu, wd = _expert_weights(w_expert_gate.reshape(-1, D_MODEL, EXPERT_FF),
                                     w_expert_up.reshape(-1, D_MODEL, EXPERT_FF),
                                     w_expert_down.reshape(-1, EXPERT_FF, D_MODEL), layer)
        h = _moe(xc, route, counts, h, wg, wu, wd)
    return _final_norm(h, final_norm_g[None].astype(F32), batch, lp, seq)
```

```python
import functools

import jax
import jax.numpy as jnp
import numpy as np
from jax import lax
from jax.experimental import pallas as pl
from jax.experimental.pallas import tpu as pltpu

F32 = jnp.float32
BF16 = jnp.bfloat16

D_MODEL = 1024
CHUNK = 64
N_META = 16
FRONT = CHUNK - N_META
ROPE_THETA = 500000.0
NORM_EPS = 1e-6
LANES = 128
SUBLANES = 8

HGRN_HEADS = 4
HGRN_HEAD_DIM = 128
HGRN_WIDTH = HGRN_HEADS * HGRN_HEAD_DIM
SUB = 4
HGRN_MAX_GROUP = 11

ATT_HEADS = 8
ATT_KV_HEADS = 2
ATT_HEAD_DIM = 64
ATT_GROUP = ATT_HEADS // ATT_KV_HEADS
ATT_WIDTH = ATT_HEADS * ATT_HEAD_DIM
ATT_KV_WIDTH = ATT_KV_HEADS * ATT_HEAD_DIM
ATT_ROPE_HALF = ATT_HEAD_DIM // 4 // 2
IDX_HEADS = 8
IDX_HEAD_DIM = 32
IDX_WIDTH = IDX_HEADS * IDX_HEAD_DIM
IDX_ROPE_HALF = IDX_HEAD_DIM // 4 // 2
IDX_HEADS_PER_LANE_GROUP = LANES // IDX_HEAD_DIM
TOPK_MAX = 256
Q_BLOCK = 128
DSA_EXTENT_STEP = 3
FOLD = 64
ONES_ROWS = 16

N_GROUPS = 4
EXPERTS_PER_GROUP = 8
N_EXPERTS = N_GROUPS * EXPERTS_PER_GROUP
EXPERT_FF = D_MODEL // 4
MERGE_TILE = 512
MOE_SLOT_TILE = 512
ROW_DMA_UNROLL = 8

COL_HG = 0
COL_ATT = 4 * HGRN_WIDTH
COL_IDX = COL_ATT + ATT_WIDTH + 2 * ATT_KV_WIDTH
IDX_RAW = IDX_WIDTH + IDX_HEAD_DIM + IDX_HEADS
IDX_PAD = IDX_WIDTH + LANES
COL_GATE = COL_IDX + IDX_PAD
IN_WIDTH_PAD = COL_GATE + 2 * D_MODEL

INT_MIN = np.int32(-2 ** 31)
NEG_FLT_MAX_KEY = np.int32(-2 ** 31 + 0x800000)
NEG_INF = float("-inf")
POS_INF = float("inf")

VMEM_LIMIT = 56 * 1024 * 1024


def _dot(a, b):
    return jnp.dot(a, b, preferred_element_type=F32)


def _dot_nt(a, b):
    return lax.dot_general(a, b, (((1,), (1,)), ((), ())), preferred_element_type=F32)


def _rope(x, c, s1, s2, half):
    return x * c + pltpu.roll(x, half, 1) * s1 + pltpu.roll(x, LANES - half, 1) * s2


def _inproj_kernel(h_ref, g_ref, w_ref, tab_ref, ikg_ref, rep_ref,
                   hg_ref, gate_ref, q_ref, k_ref, v_ref, iq_ref, ik_ref, iw_ref):
    x = h_ref[...]
    xn = (x * lax.rsqrt(jnp.mean(x * x, axis=-1, keepdims=True) + NORM_EPS) * g_ref[...]).astype(BF16)
    hg_ref[...] = _dot(xn, w_ref[:, COL_HG:COL_ATT])
    gate_ref[...] = _dot(xn, w_ref[:, COL_GATE:IN_WIDTH_PAD])
    att = _dot(xn, w_ref[:, COL_ATT:COL_IDX])
    idx = _dot(xn, w_ref[:, COL_IDX:COL_GATE])
    ca, s1a, s2a = tab_ref[0], tab_ref[1], tab_ref[2]
    ci, s1i, s2i = tab_ref[3], tab_ref[4], tab_ref[5]
    scale = ATT_HEAD_DIM ** -0.5
    for m in range(ATT_WIDTH // LANES):
        sl = slice(m * LANES, (m + 1) * LANES)
        q_ref[:, sl] = (_rope(att[:, sl], ca, s1a, s2a, ATT_ROPE_HALF) * scale).astype(BF16)
    k_ref[...] = _rope(att[:, ATT_WIDTH:ATT_WIDTH + LANES], ca, s1a, s2a, ATT_ROPE_HALF).astype(BF16)
    v_ref[...] = att[:, ATT_WIDTH + LANES:].astype(BF16)
    for m in range(IDX_WIDTH // LANES):
        sl = slice(m * LANES, (m + 1) * LANES)
        iq_ref[:, sl] = _rope(idx[:, sl], ci, s1i, s2i, IDX_ROPE_HALF).astype(BF16)
    xk = idx[:, IDX_WIDTH:]
    lane = lax.broadcasted_iota(jnp.int32, xk.shape, 1)
    ms = jnp.sum(jnp.where(lane < IDX_HEAD_DIM, xk * xk, 0.0), axis=-1, keepdims=True) / IDX_HEAD_DIM
    ikn = xk * lax.rsqrt(ms + NORM_EPS) * ikg_ref[...]
    ikr = _rope(ikn, ci, s1i, s2i, IDX_ROPE_HALF).astype(BF16)
    ik_ref[...] = _dot(ikr, rep_ref[...]).astype(BF16)
    iw_ref[...] = xk * (IDX_HEADS ** -0.5 * IDX_HEAD_DIM ** -0.5)


def _inproj(h, g, w, tabs, ikg, rep, lp):
    n = h.shape[0]
    tm = lp // 8
    per = lp // tm
    row = lambda i: (i, 0)
    const2 = lambda i: (0, 0)
    out_shape = (
        jax.ShapeDtypeStruct((n, 4 * HGRN_WIDTH), F32),
        jax.ShapeDtypeStruct((n, 2 * D_MODEL), F32),
        jax.ShapeDtypeStruct((n, ATT_WIDTH), BF16),
        jax.ShapeDtypeStruct((n, LANES), BF16),
        jax.ShapeDtypeStruct((n, LANES), BF16),
        jax.ShapeDtypeStruct((n, IDX_WIDTH), BF16),
        jax.ShapeDtypeStruct((n, LANES), BF16),
        jax.ShapeDtypeStruct((n, LANES), F32),
    )
    return pl.pallas_call(
        _inproj_kernel,
        grid=(n // tm,),
        in_specs=[
            pl.BlockSpec((tm, D_MODEL), row),
            pl.BlockSpec((1, D_MODEL), const2),
            pl.BlockSpec((D_MODEL, IN_WIDTH_PAD), const2),
            pl.BlockSpec((6, tm, LANES), lambda i: (0, i % per, 0)),
            pl.BlockSpec((1, LANES), const2),
            pl.BlockSpec((LANES, LANES), const2),
        ],
        out_specs=tuple(pl.BlockSpec((tm, s.shape[1]), row) for s in out_shape),
        out_shape=out_shape,
        compiler_params=pltpu.CompilerParams(
            dimension_semantics=("arbitrary",), vmem_limit_bytes=VMEM_LIMIT),
        name="inproj",
    )(h, g, w, tabs, ikg, rep)


def _hgrn_kernel(q_ref, f_ref, i_ref, g_ref, lb_ref, ng_ref, o_ref, st_ref, kb_ref, bb_ref, vb_ref,
                 *, n_chunks, group):
    C = CHUNK
    D = HGRN_HEAD_DIM
    st_ref[...] = jnp.zeros_like(st_ref)
    kb_ref[...] = jnp.zeros_like(kb_ref)
    bb_ref[...] = jnp.zeros_like(bb_ref)
    vb_ref[...] = jnp.zeros_like(vb_ref)
    lb = lb_ref[...]
    ng = ng_ref[...]
    R = group * C
    tri_row = lax.broadcasted_iota(jnp.int32, (C, C), 0)
    tri_col = lax.broadcasted_iota(jnp.int32, (C, C), 1)
    tri = (tri_row >= tri_col).astype(F32)
    local_row = lax.broadcasted_iota(jnp.int32, (R, 1), 0)
    sub_pos = local_row & (SUB - 1)

    def chunk(x, u):
        return x[u * C:(u + 1) * C, :]

    def body(it, carry):
        rows = pl.ds(pl.multiple_of(it * R, C), R)
        z = f_ref[rows, :]
        q = q_ref[rows, :]
        v = i_ref[rows, :]
        go = g_ref[rows, :]
        pad = (it * R + local_row) < FRONT
        f = lb + (1.0 - lb) * jax.nn.sigmoid(z)
        logf = jnp.where(pad, 0.0, jnp.log(f))
        k = jnp.where(pad, 0.0, (1.0 - lb) * jax.nn.sigmoid(-z))
        v = jnp.where(pad, 0.0, v)
        b = jnp.concatenate(
            [jnp.dot(tri, chunk(logf, u), preferred_element_type=F32, precision=lax.Precision.HIGHEST)
             for u in range(group)], axis=0)
        b3 = b.reshape(group, C, D)
        b_last = b3[:, C - 1:C, :]
        qd = (q * jnp.exp(b)).astype(BF16)
        kd = (k.reshape(group, C, D) * jnp.exp(b_last - b3)).reshape(R, D).astype(BF16)
        e_last = jnp.exp(b_last)
        upd =[_dot(chunk(v, u).T.astype(BF16), chunk(kd, u)) for u in range(group)]

        st = st_ref[...]
        before = []
        for u in range(group):
            before.append(st.astype(BF16))
            st = st * e_last[u] + upd[u]
        st_ref[...] = st
        o = jnp.concatenate([_dot_nt(chunk(qd, u), before[u]) for u in range(group)], axis=0)

        kb_ref[SUBLANES:, :] = k
        bb_ref[SUBLANES:, :] = b
        vb_ref[SUBLANES:, :] = v
        for j in range(SUB):
            ks = kb_ref[SUBLANES - j:SUBLANES - j + R, :]
            bs = bb_ref[SUBLANES - j:SUBLANES - j + R, :]
            vs = vb_ref[SUBLANES - j:SUBLANES - j + R, :]
            w = jnp.exp(jnp.minimum(b - bs, 0.0))
            a = jnp.sum(q * ks * w, axis=-1, keepdims=True)
            o = o + jnp.where(sub_pos >= j, a, 0.0) * vs

        att = [jnp.zeros((C, C), F32) for _ in range(group)]
        span = C
        while span > SUB:
            half = span // 2
            upper = (local_row & (span - 1)) >= half
            bs3 = b.reshape(R // span, span, D)
            mid = bs3[:, half - 1:half, :]
            qs = jnp.where(upper, (q.reshape(bs3.shape) * jnp.exp(jnp.minimum(bs3 - mid, 0.0))).reshape(R, D), 0.0)
            ks = jnp.where(upper, 0.0, (k.reshape(bs3.shape) * jnp.exp(jnp.minimum(mid - bs3, 0.0))).reshape(R, D))
            qs, ks = qs.astype(BF16), ks.astype(BF16)
            same_block = (tri_row & -span) == (tri_col & -span)
            att = [att[u] + jnp.where(same_block, _dot_nt(chunk(qs, u), chunk(ks, u)), 0.0) for u in range(group)]
            span = half
        o = o + jnp.concatenate([_dot(att[u].astype(BF16), chunk(v, u).astype(BF16)) for u in range(group)], axis=0)

        on = o * lax.rsqrt(jnp.mean(o * o, axis=-1, keepdims=True) + NORM_EPS) * ng
        o_ref[rows, :] = (on * (go * jax.nn.sigmoid(go))).astype(o_ref.dtype)
        return carry

    lax.fori_loop(0, n_chunks // group, body, 0)
    tail = o_ref.shape[0] - n_chunks * C
    if tail:
        o_ref[n_chunks * C:, :] = jnp.zeros((tail, D), o_ref.dtype)


def _hgrn(hg, lb, ng, batch, lp, n_chunks):
    n = hg.shape[0]
    H, D = HGRN_HEADS, HGRN_HEAD_DIM
    group = max(g for g in range(1, HGRN_MAX_GROUP + 1) if n_chunks % g == 0)

    def col(k):
        return pl.BlockSpec((lp, D), lambda b, h: (b, k * H + h))

    vec = pl.BlockSpec((1, D), lambda b, h: (0, h))
    return pl.pallas_call(
        functools.partial(_hgrn_kernel, n_chunks=n_chunks, group=group),
        grid=(batch, H),
        in_specs=[col(0), col(1), col(2), col(3), vec, vec],
        out_specs=pl.BlockSpec((lp, D), lambda b, h: (b, h)),
        out_shape=jax.ShapeDtypeStruct((n, HGRN_WIDTH), BF16),
        scratch_shapes=[
            pltpu.VMEM((D, D), F32),
            pltpu.VMEM((group * CHUNK + SUBLANES, D), F32),
            pltpu.VMEM((group * CHUNK + SUBLANES, D), F32),
            pltpu.VMEM((group * CHUNK + SUBLANES, D), F32),
        ],
        compiler_params=pltpu.CompilerParams(
            dimension_semantics=("arbitrary", "arbitrary"), vmem_limit_bytes=VMEM_LIMIT),
        name="hgrn2",
    )(hg, hg, hg, hg, lb, ng)


def _sortable_to_f32(key):
    bits = jnp.where(key < 0, key ^ jnp.int32(0x7FFFFFFF), key)
    return pltpu.bitcast(bits, F32)


def _fold_rows(x, op):
    rows = x.shape[0]
    if rows > FOLD and rows % FOLD == 0:
        x = op(x.reshape(rows // FOLD, FOLD, x.shape[1]), axis=0)
    return op(x, axis=0, keepdims=True)


def _dsa_kernel(q_ref, iq_ref, iw_ref, k_ref, v_ref, ik_ref, ltri_ref, o_ref,
                sc_ref, key_ref, hi_ref, vt_ref, iq8_ref, qg_ref, ot_ref, *, topk, n_valid, extents):
    QB = Q_BLOCK
    lp = k_ref.shape[0]
    t = pl.program_id(1)
    kf = float(topk)

    @pl.when(t == 0)
    def _():
        for kb in range(lp // QB):
            sl = slice(kb * QB, (kb + 1) * QB)
            vt = v_ref[sl, :].astype(F32).T.astype(BF16)
            for g in range(ATT_KV_HEADS):
                vt_ref[g, :ATT_HEAD_DIM, sl] = vt[g * ATT_HEAD_DIM:(g + 1) * ATT_HEAD_DIM, :]
        for g in range(ATT_KV_HEADS):
            vt_ref[g, ATT_HEAD_DIM:, :] = jnp.ones((ONES_ROWS, lp), BF16)

    lane = lax.broadcasted_iota(jnp.int32, (QB, LANES), 1)
    iq = iq_ref[...]
    for h in range(IDX_HEADS):
        grp, slot = divmod(h, IDX_HEADS_PER_LANE_GROUP)
        part = iq[:, grp * LANES:(grp + 1) * LANES]
        iq8_ref[h * QB:(h + 1) * QB, :] = jnp.where((lane >> 5) == slot, part, jnp.zeros_like(part))
    wt = iw_ref[...].T
    lane_half = lane >> 6
    for h in range(ATT_HEADS):
        g, r = divmod(h, ATT_GROUP)
        m, p = divmod(h, LANES // ATT_HEAD_DIM)
        qh = q_ref[:, m * LANES:(m + 1) * LANES].astype(F32)
        if p != g:
            qh = pltpu.roll(qh, ATT_HEAD_DIM, 1)
        qg_ref[g, r * QB:(r + 1) * QB, :] = jnp.where(lane_half == g, qh, 0.0).astype(BF16)

    qpos = t * QB + lax.broadcasted_iota(jnp.int32, (1, QB), 1)
    qchunk = qpos >> 6
    n_adm = jnp.minimum((qchunk + 1) * CHUNK, n_valid) - FRONT
    take_all = n_adm <= topk
    ltri = ltri_ref[...]

    def tile_body(n_blocks):
        ke = n_blocks * QB
        keys = slice(0, ke)
        kpos = lax.broadcasted_iota(jnp.int32, (ke, 1), 0)
        adm = ((kpos >> 6) <= qchunk) & (kpos >= FRONT) & (kpos < n_valid)

        ik = ik_ref[keys, :]
        acc = jnp.zeros((ke, QB), F32)
        for h in range(IDX_HEADS):
            rel = jnp.maximum(_dot_nt(ik, iq8_ref[h * QB:(h + 1) * QB, :]), 0.0)
            acc = acc + rel * wt[IDX_HEAD_DIM + h:IDX_HEAD_DIM + h + 1, :]
        acc = jnp.where(acc == 0.0, 0.0, acc)
        sc_ref[keys, :] = jnp.where(adm, acc, NEG_INF)

        def counts(cand):
            x = sc_ref[keys, :]
            return (_fold_rows(jnp.where(x >= cand, 1.0, 0.0), jnp.sum),
                    _fold_rows(jnp.where(x > cand, 1.0, 0.0), jnp.sum))

        def count16(ref, cand, strict=False):
            x = ref[keys, :]
            hit = jnp.where((x > cand) if strict else (x >= cand), jnp.ones((), BF16), jnp.zeros((), BF16))
            parts = [hit[i * FOLD:(i + 1) * FOLD, :] for i in range(ke // FOLD)]
            while len(parts) > 1:
                parts = [a + b for a, b in zip(parts[::2], parts[1::2])] + parts[len(parts) & ~1:]
            return jnp.sum(parts[0].astype(F32), axis=0, keepdims=True)

        def pattern16(v):
            return pltpu.bitcast(jnp.where(v < 0, v ^ 0x7FFF, v) << 16, F32).astype(BF16)

        def digit16(v):
            return v.astype(F32).astype(BF16)

        def bisect(ref, width, start, image, want):
            def step(i, t):
                cand = t + lax.shift_left(jnp.int32(1), width - 1 - i)
                return jnp.where(count16(ref, image(cand)) >= want, cand, t)

            t = lax.fori_loop(0, width, step, jnp.full((1, QB), start, jnp.int32))
            return t, want - count16(ref, image(t), strict=True)

        bits = pltpu.bitcast(sc_ref[keys, :], jnp.int32)
        key = jnp.where(bits < 0, bits ^ 0x7FFFFFFF, bits)
        key_ref[keys, :] = key
        hi_ref[keys, :] = pattern16(key >> 16)
        top, want = bisect(hi_ref, 16, -(2 ** 15), pattern16, kf)
        key = key_ref[keys, :]
        in_bucket = (key >> 16) == top
        hi_ref[keys, :] = digit16(jnp.where(in_bucket, (key >> 8) & 0xFF, -1))
        mid, want = bisect(hi_ref, 8, 0, digit16, want)
        key = key_ref[keys, :]
        in_bucket = (key >> 8) == ((top << 8) | mid)
        hi_ref[keys, :] = digit16(jnp.where(in_bucket, key & 0xFF, -1))
        low, _ = bisect(hi_ref, 8, 0, digit16, want)
        tkey = (top << 16) | (mid << 8) | low
        thr = _sortable_to_f32(jnp.where(take_all, NEG_FLT_MAX_KEY, tkey))
        c_ge, c_gt = counts(thr)

        def unsettled(c_ge, c_gt):
            live = jnp.logical_not(take_all)
            return (c_gt >= kf) & live, (c_ge < kf) & live

        def cond(carry):
            up, dn = unsettled(carry[1], carry[2])
            return jnp.max(jnp.where(up | dn, 1.0, 0.0)) > 0.0

        def fix(carry):
            thr, c_ge, c_gt = carry
            up, dn = unsettled(c_ge, c_gt)
            x = sc_ref[keys, :]
            above = _fold_rows(jnp.where(x > thr, x, POS_INF), jnp.min)
            below = _fold_rows(jnp.where(x < thr, x, NEG_INF), jnp.max)
            thr = jnp.where(up, above, jnp.where(dn, below, thr))
            return (thr,) + counts(thr)

        thr, c_ge, c_gt = lax.while_loop(cond, fix, (thr, c_ge, c_gt))

        need = kf - c_gt
        seen = jnp.zeros((1, QB), F32)
        for kb in range(n_blocks):
            blk = slice(kb * QB, (kb + 1) * QB)
            x = sc_ref[blk, :]
            eq = x == thr
            pre = _dot(ltri, jnp.where(eq, 1.0, 0.0).astype(BF16))
            take = (x > thr) | (eq & (pre + seen <= need))
            sc_ref[blk, :] = jnp.where(take, 0.0, NEG_INF)
            seen = seen + pre[QB - 1:QB, :]

        bias = sc_ref[keys, :]
        bias = jnp.concatenate([bias] * ATT_GROUP, axis=1)
        kk = k_ref[keys, :]
        for g in range(ATT_KV_HEADS):
            s = _dot_nt(kk, qg_ref[g]) + bias
            e = jnp.exp((s - _fold_rows(s, jnp.max)).astype(BF16))
            pv = _dot(vt_ref[g, :, keys], e)
            pv = pv[:ATT_HEAD_DIM, :] / pv[ATT_HEAD_DIM:ATT_HEAD_DIM + 1, :]
            for r in range(ATT_GROUP):
                h = g * ATT_GROUP + r
                ot_ref[h * ATT_HEAD_DIM:(h + 1) * ATT_HEAD_DIM, :] = pv[:, r * QB:(r + 1) * QB]
        o_ref[...] = ot_ref[...].T.astype(o_ref.dtype)

    lo = 0
    for n_blocks in extents:
        pl.when((t >= lo) & (t < n_blocks))(functools.partial(tile_body, n_blocks))
        lo = n_blocks


def _dsa(q, k, v, iq, ik, iw, ltri, batch, lp, topk, n_valid):
    n = q.shape[0]
    nq = lp // Q_BLOCK
    extents = tuple(sorted({nq - DSA_EXTENT_STEP * i for i in range(-(-nq // DSA_EXTENT_STEP))}))
    qblk = lambda b, i: (b * nq + i, 0)
    bat = lambda b, i: (b, 0)
    return pl.pallas_call(
        functools.partial(_dsa_kernel, topk=topk, n_valid=n_valid, extents=extents),
        grid=(batch, nq),
        in_specs=[
            pl.BlockSpec((Q_BLOCK, ATT_WIDTH), qblk),
            pl.BlockSpec((Q_BLOCK, IDX_WIDTH), qblk),
            pl.BlockSpec((Q_BLOCK, LANES), qblk),
            pl.BlockSpec((lp, LANES), bat),
            pl.BlockSpec((lp, LANES), bat),
            pl.BlockSpec((lp, LANES), bat),
            pl.BlockSpec((Q_BLOCK, Q_BLOCK), lambda b, i: (0, 0)),
        ],
        out_specs=pl.BlockSpec((Q_BLOCK, ATT_WIDTH), qblk),
        out_shape=jax.ShapeDtypeStruct((n, ATT_WIDTH), BF16),
        scratch_shapes=[
            pltpu.VMEM((lp, Q_BLOCK), F32),
            pltpu.VMEM((lp, Q_BLOCK), jnp.int32),
            pltpu.VMEM((lp, Q_BLOCK), BF16),
            pltpu.VMEM((ATT_KV_HEADS, ATT_HEAD_DIM + ONES_ROWS, lp), BF16),
            pltpu.VMEM((IDX_HEADS * Q_BLOCK, LANES), BF16),
            pltpu.VMEM((ATT_KV_HEADS, ATT_GROUP * Q_BLOCK, LANES), BF16),
            pltpu.VMEM((ATT_WIDTH, Q_BLOCK), F32),
        ],
        compiler_params=pltpu.CompilerParams(
            dimension_semantics=("arbitrary", "arbitrary"), vmem_limit_bytes=VMEM_LIMIT),
        name="dsa",
    )(q, iq, iw, k, v, ik, ltri)


def _merge_kernel(ya_ref, yb_ref, gate_ref, h_ref, wa_ref, wb_ref, wo_ref, fg_ref, wr_ref, br_ref, ltri_ref,
                  ho_ref, xc_ref, route_ref, cnt_ref):
    @pl.when(pl.program_id(0) == 0)
    def _():
        cnt_ref[...] = jnp.zeros_like(cnt_ref)

    gate = gate_ref[...]
    pa = _dot(ya_ref[...], wa_ref[...])
    pb = _dot(yb_ref[...], wb_ref[...])
    merged = jax.nn.sigmoid(gate[:, :D_MODEL]) * pa + jax.nn.sigmoid(gate[:, D_MODEL:]) * pb
    h = h_ref[...] + _dot(merged.astype(BF16), wo_ref[...])
    ho_ref[...] = h
    xn = h * lax.rsqrt(jnp.mean(h * h, axis=-1, keepdims=True) + NORM_EPS) * fg_ref[...]
    xh = xn.astype(BF16)
    xc_ref[:, :D_MODEL] = xh.astype(F32)

    xl = (xn - xh.astype(F32)).astype(BF16)
    r = _dot(xh, wr_ref[...])
    lg = r[:, :LANES] + r[:, LANES:] + _dot(xl, wr_ref[:, :LANES]) + br_ref[...]
    lane = lax.broadcasted_iota(jnp.int32, lg.shape, 1)
    lane_f = lane.astype(F32)
    big = float(LANES)
    is_g = (lane >= N_EXPERTS) & (lane < N_EXPERTS + N_GROUPS)
    gl = jnp.where(is_g, lg, NEG_INF)
    gmax = jnp.max(gl, axis=-1, keepdims=True)
    pg_top = 1.0 / jnp.sum(jnp.exp(gl - gmax), axis=-1, keepdims=True)
    g_lane = jnp.min(jnp.where(gl == gmax, lane_f, big), axis=-1, keepdims=True)
    e_lo = (g_lane - N_EXPERTS) * EXPERTS_PER_GROUP
    in_grp = (lane_f >= e_lo) & (lane_f < e_lo + EXPERTS_PER_GROUP)
    el = jnp.where(in_grp, lg, NEG_INF)
    ee = jnp.exp(el - jnp.max(el, axis=-1, keepdims=True))
    pe = ee / jnp.sum(ee, axis=-1, keepdims=True)
    pe = jnp.where(in_grp, pe, -1.0)
    p1 = jnp.max(pe, axis=-1, keepdims=True)
    i1 = jnp.min(jnp.where(pe == p1, lane_f, big), axis=-1, keepdims=True)
    pe2 = jnp.where(lane_f == i1, -1.0, pe)
    p2 = jnp.max(pe2, axis=-1, keepdims=True)
    i2 = jnp.min(jnp.where(pe2 == p2, lane_f, big), axis=-1, keepdims=True)
    tot = p1 + p2
    xc_ref[:, D_MODEL:] = jnp.where(lane_f == i1, p1 / tot * pg_top,
                                    jnp.where(lane_f == i2, p2 / tot * pg_top, 0.0))

    grp = g_lane - N_EXPERTS
    onehot = jnp.where(lane_f == grp, 1.0, 0.0)
    incl = _dot(ltri_ref[...], onehot.astype(BF16))
    seen = cnt_ref[0:1, :]
    rank = jnp.sum(onehot * (incl - 1.0 + seen), axis=-1, keepdims=True)
    route_ref[...] = jnp.where(lane == 0, grp, jnp.where(lane == 1, rank, 0.0))
    cnt_ref[...] = jnp.broadcast_to(seen + incl[incl.shape[0] - 1:, :], cnt_ref.shape)


def _merge(ya, yb, gates, h, wa, wb, wo, fg, wr, br):
    n = h.shape[0]
    tm = min(MERGE_TILE, n)
    ltri = jnp.asarray(np.tril(np.ones((tm, tm), np.float32)), BF16)
    row = lambda i: (i, 0)
    const = lambda i: (0, 0)
    return pl.pallas_call(
        _merge_kernel,
        grid=(n // tm,),
        in_specs=[
            pl.BlockSpec((tm, HGRN_WIDTH), row),
            pl.BlockSpec((tm, ATT_WIDTH), row),
            pl.BlockSpec((tm, 2 * D_MODEL), row),
            pl.BlockSpec((tm, D_MODEL), row),
            pl.BlockSpec((HGRN_WIDTH, D_MODEL), const),
            pl.BlockSpec((ATT_WIDTH, D_MODEL), const),
            pl.BlockSpec((D_MODEL, D_MODEL), const),
            pl.BlockSpec((1, D_MODEL), const),
            pl.BlockSpec((D_MODEL, 2 * LANES), const),
            pl.BlockSpec((1, LANES), const),
            pl.BlockSpec((tm, tm), const),
        ],
        out_specs=(pl.BlockSpec((tm, D_MODEL), row), pl.BlockSpec((tm, D_MODEL + LANES), row),
                   pl.BlockSpec((tm, LANES), row), pl.BlockSpec((SUBLANES, LANES), const)),
        out_shape=(jax.ShapeDtypeStruct((n, D_MODEL), F32), jax.ShapeDtypeStruct((n, D_MODEL + LANES), F32),
                   jax.ShapeDtypeStruct((n, LANES), F32), jax.ShapeDtypeStruct((SUBLANES, LANES), F32)),
        compiler_params=pltpu.CompilerParams(
            dimension_semantics=("arbitrary",), vmem_limit_bytes=VMEM_LIMIT),
        name="merge_router",
    )(ya, yb, gates, h, wa, wb, wo, fg, wr, br, ltri)


def _row_copies(n_rows, make_copy, whole_copy):
    def issue(r, c):
        make_copy(r).start()
        return c

    lax.fori_loop(0, n_rows, issue, 0, unroll=ROW_DMA_UNROLL)
    whole_copy.wait()


def _dispatch_kernel(dest_ref, x_ref, zeros_ref, xs_ref, sem):
    del zeros_ref
    tm = x_ref.shape[0]

    def row_copy(r):
        return pltpu.make_async_copy(x_ref.at[pl.ds(r, 1)], xs_ref.at[pl.ds(dest_ref[0, 0, r], 1)], sem)

    _row_copies(tm, row_copy, pltpu.make_async_copy(x_ref, xs_ref.at[pl.ds(0, tm)], sem))


def _experts_kernel(tile_group_ref, tile_on_ref, xs_ref, wg_ref, wu_ref, wd_ref, ys_ref):
    t = pl.program_id(0)
    tm = xs_ref.shape[0]

    @pl.when(tile_on_ref[t] == 1)
    def _():
        x = xs_ref[:, :D_MODEL].astype(BF16)
        c = xs_ref[:, D_MODEL:]
        lane = lax.broadcasted_iota(jnp.int32, c.shape, 1)
        first = tile_group_ref[t] * EXPERTS_PER_GROUP
        scale = jnp.concatenate(
            [jnp.broadcast_to(jnp.sum(jnp.where(lane == first + e, c, 0.0), axis=-1, keepdims=True), (tm, EXPERT_FF))
             for e in range(EXPERTS_PER_GROUP)], axis=1)
        a = _dot(x, wg_ref[0])
        u = _dot(x, wu_ref[0])
        act = (a * jax.nn.sigmoid(a)) * u * scale
        ys_ref[...] = _dot(act.astype(BF16), wd_ref[0])

    @pl.when(tile_on_ref[t] == 0)
    def _():
        ys_ref[...] = jnp.zeros_like(ys_ref)


def _combine_kernel(dest_ref, h_ref, ys_ref, o_ref, buf_ref, sem):
    tm = h_ref.shape[0]

    def row_copy(r):
        return pltpu.make_async_copy(ys_ref.at[pl.ds(dest_ref[0, 0, r], 1)], buf_ref.at[pl.ds(r, 1)], sem)

    _row_copies(tm, row_copy, pltpu.make_async_copy(ys_ref.at[pl.ds(0, tm)], buf_ref, sem))
    o_ref[...] = h_ref[...] + buf_ref[...]


def _expert_weights_kernel(g_ref, u_ref, d_ref, go_ref, uo_ref, do_ref):
    go_ref[0] = g_ref[0].astype(BF16)
    uo_ref[0] = u_ref[0].astype(BF16)
    do_ref[0] = d_ref[0].astype(BF16)


def _expert_weights(w_gate, w_up, w_down, layer):
    per = EXPERTS_PER_GROUP
    src = lambda shape: pl.BlockSpec((1,) + shape, lambda e: (layer * N_EXPERTS + e, 0, 0))
    side = pl.BlockSpec((1, D_MODEL, EXPERT_FF), lambda e: (e // per, 0, e % per))
    stack = pl.BlockSpec((1, EXPERT_FF, D_MODEL), lambda e: (e // per, e % per, 0))
    return pl.pallas_call(
        _expert_weights_kernel,
        grid=(N_EXPERTS,),
        in_specs=[src((D_MODEL, EXPERT_FF)), src((D_MODEL, EXPERT_FF)), src((EXPERT_FF, D_MODEL))],
        out_specs=(side, side, stack),
        out_shape=(jax.ShapeDtypeStruct((N_GROUPS, D_MODEL, per * EXPERT_FF), BF16),
                   jax.ShapeDtypeStruct((N_GROUPS, D_MODEL, per * EXPERT_FF), BF16),
                   jax.ShapeDtypeStruct((N_GROUPS, per * EXPERT_FF, D_MODEL), BF16)),
        compiler_params=pltpu.CompilerParams(
            dimension_semantics=("arbitrary",), vmem_limit_bytes=VMEM_LIMIT),
        name="expert_weights",
    )(w_gate, w_up, w_down)


def _moe(xc, route, counts, h, wg, wu, wd):
    n = h.shape[0]
    tm = min(MERGE_TILE, n)
    ts = min(MOE_SLOT_TILE, n)
    n_tiles = n // ts + N_GROUPS
    n_slots = n_tiles * ts

    grp = route[:, 0].astype(jnp.int32)
    rank = route[:, 1].astype(jnp.int32)
    cnt = counts[0, :N_GROUPS].astype(jnp.int32)
    padded = (cnt + ts - 1) // ts * ts
    ends = jnp.cumsum(padded)
    starts = ends - padded
    dest = (starts[grp] + rank).reshape(n // tm, 1, tm)
    tile_start = jnp.arange(n_tiles, dtype=jnp.int32) * ts
    tile_group = jnp.minimum(jnp.sum(tile_start[:, None] >= ends[None, :], axis=1), N_GROUPS - 1).astype(jnp.int32)
    tile_on = (tile_start < ends[N_GROUPS - 1]).astype(jnp.int32)

    tok = lambda i: (i, 0)
    smem_idx = pl.BlockSpec((1, 1, tm), lambda i: (i, 0, 0), memory_space=pltpu.SMEM)
    hbm = pl.BlockSpec(memory_space=pl.ANY)
    xs = pl.pallas_call(
        _dispatch_kernel,
        grid=(n // tm,),
        in_specs=[smem_idx, pl.BlockSpec((tm, D_MODEL + LANES), tok), hbm],
        out_specs=hbm,
        out_shape=jax.ShapeDtypeStruct((n_slots, D_MODEL + LANES), F32),
        scratch_shapes=[pltpu.SemaphoreType.DMA(())],
        input_output_aliases={2: 0},
        compiler_params=pltpu.CompilerParams(
            dimension_semantics=("arbitrary",), vmem_limit_bytes=VMEM_LIMIT),
        name="moe_dispatch",
    )(dest, xc, jnp.zeros((n_slots, D_MODEL + LANES), F32))

    wspec = lambda shape: pl.BlockSpec((1,) + shape, lambda t, tg, on: (tg[t], 0, 0))
    ys = pl.pallas_call(
        _experts_kernel,
        grid_spec=pltpu.PrefetchScalarGridSpec(
            num_scalar_prefetch=2,
            grid=(n_tiles,),
            in_specs=[pl.BlockSpec((ts, D_MODEL + LANES), lambda t, tg, on: (t, 0)),
                      wspec((D_MODEL, EXPERTS_PER_GROUP * EXPERT_FF)),
                      wspec((D_MODEL, EXPERTS_PER_GROUP * EXPERT_FF)),
                      wspec((EXPERTS_PER_GROUP * EXPERT_FF, D_MODEL))],
            out_specs=pl.BlockSpec((ts, D_MODEL), lambda t, tg, on: (t, 0)),
        ),
        out_shape=jax.ShapeDtypeStruct((n_slots, D_MODEL), F32),
        compiler_params=pltpu.CompilerParams(
            dimension_semantics=("arbitrary",), vmem_limit_bytes=VMEM_LIMIT),
        name="moe_experts",
    )(tile_group, tile_on, xs, wg, wu, wd)

    return pl.pallas_call(
        _combine_kernel,
        grid=(n // tm,),
        in_specs=[smem_idx, pl.BlockSpec((tm, D_MODEL), tok), hbm],
        out_specs=pl.BlockSpec((tm, D_MODEL), tok),
        out_shape=jax.ShapeDtypeStruct((n, D_MODEL), F32),
        scratch_shapes=[pltpu.VMEM((tm, D_MODEL), F32), pltpu.SemaphoreType.DMA(())],
        compiler_params=pltpu.CompilerParams(
            dimension_semantics=("arbitrary",), vmem_limit_bytes=VMEM_LIMIT),
        name="moe_combine",
    )(dest, h, ys)


def _final_kernel(h_ref, g_ref, o_ref, *, seq):
    step = 256 if seq % 256 == 0 else CHUNK
    for r in range(0, seq, step):
        x = h_ref[CHUNK + r:CHUNK + r + step, :]
        o_ref[0, r:r + step, :] = x * lax.rsqrt(jnp.mean(x * x, axis=-1, keepdims=True) + NORM_EPS) * g_ref[...]


def _final_norm(h, g, batch, lp, seq):
    return pl.pallas_call(
        functools.partial(_final_kernel, seq=seq),
        grid=(batch,),
        in_specs=[pl.BlockSpec((lp, D_MODEL), lambda b: (b, 0)), pl.BlockSpec((1, D_MODEL), lambda b: (0, 0))],
        out_specs=pl.BlockSpec((1, seq, D_MODEL), lambda b: (b, 0, 0)),
        out_shape=jax.ShapeDtypeStruct((batch, seq, D_MODEL), F32),
        compiler_params=pltpu.CompilerParams(
            dimension_semantics=("arbitrary",), vmem_limit_bytes=VMEM_LIMIT),
        name="final_norm",
    )(h, g)


def _rope_lane_tables(pos, head_dim):
    rot = head_dim // 4
    half = rot // 2
    inv = ROPE_THETA ** (-jnp.arange(0, rot, 2, dtype=F32) / rot)
    ang = pos[:, None] * inv[None, :]
    cos, sin = jnp.cos(ang), jnp.sin(ang)
    jj = np.arange(LANES) % head_dim
    first = jnp.asarray(jj < half)
    second = jnp.asarray((jj >= half) & (jj < rot))
    fidx = jnp.asarray(np.where(jj < half, jj, np.where(jj < rot, jj - half, 0)))
    cl, sl = cos[:, fidx], sin[:, fidx]
    c = jnp.where(first | second, cl, 1.0)
    s1 = jnp.where(second, sl, 0.0)
    s2 = jnp.where(first, -sl, 0.0)
    return c, s1, s2


def kernel(x, meta_tokens, mix_norm_g, w_in, hgrn_lb_logits, hgrn_norm_g, idx_k_norm_g, w_branch_hgrn,
           w_branch_dsa, w_out, ffn_norm_g, router_group_w, router_group_b, router_expert_w,
           router_expert_b, w_expert_gate, w_expert_up, w_expert_down, final_norm_g):
    batch, seq, _ = x.shape
    depth = w_in.shape[0]
    n_valid = CHUNK + seq
    assert seq % CHUNK == 0
    lp = -(-n_valid // Q_BLOCK) * Q_BLOCK
    n_chunks = n_valid // CHUNK
    topk = min(TOPK_MAX, (N_META + seq) // 4)

    meta = jnp.broadcast_to(meta_tokens[None].astype(x.dtype), (batch, N_META, D_MODEL))
    h = jnp.concatenate([jnp.zeros((batch, FRONT, D_MODEL), x.dtype), meta, x,
                         jnp.zeros((batch, lp - n_valid, D_MODEL), x.dtype)], axis=1)
    h = h.reshape(batch * lp, D_MODEL)

    pos = jnp.clip(jnp.arange(lp, dtype=jnp.int32) - FRONT, 0, N_META + seq - 1).astype(F32)
    tabs = jnp.stack(_rope_lane_tables(pos, ATT_HEAD_DIM) + _rope_lane_tables(pos, IDX_HEAD_DIM))

    p = jax.nn.softmax(hgrn_lb_logits.astype(F32), axis=0)
    cs = jnp.cumsum(p, axis=0)
    lower_bounds = cs - cs[0:1]

    rep = np.zeros((LANES, LANES), np.float32)
    for slot in range(IDX_HEADS_PER_LANE_GROUP):
        rep[np.arange(IDX_HEAD_DIM), slot * IDX_HEAD_DIM + np.arange(IDX_HEAD_DIM)] = 1.0
    rep = jnp.asarray(rep, BF16)
    ltri = jnp.asarray(np.tril(np.ones((Q_BLOCK, Q_BLOCK), np.float32)), BF16)

    pad_cols = jnp.zeros((D_MODEL, IDX_PAD - IDX_RAW), F32)
    for layer in range(depth):
        w = w_in[layer]
        w_p = jnp.concatenate([w[:, :COL_IDX + IDX_RAW], pad_cols, w[:, COL_IDX + IDX_RAW:]], axis=1).astype(BF16)
        ikg = jnp.pad(idx_k_norm_g[layer].astype(F32), (0, LANES - IDX_HEAD_DIM))[None]
        hg, gates, q, k, v, iq, ik, iw = _inproj(h, mix_norm_g[layer][None].astype(F32), w_p, tabs, ikg, rep, lp)
        ya = _hgrn(hg, lower_bounds[layer][None], hgrn_norm_g[layer][None].astype(F32), batch, lp, n_chunks)
        yb = _dsa(q, k, v, iq, ik, iw, ltri, batch, lp, topk, n_valid)
        wr = jnp.concatenate([router_expert_w[layer], router_group_w[layer],
                              jnp.zeros((D_MODEL, LANES - N_EXPERTS - N_GROUPS), F32)], axis=1)
        wr_hi = wr.astype(BF16)
        wr = jnp.concatenate([wr_hi, (wr - wr_hi.astype(F32)).astype(BF16)], axis=1)
        br = jnp.concatenate([router_expert_b[layer], router_group_b[layer],
                              jnp.zeros((LANES - N_EXPERTS - N_GROUPS,), F32)])[None]
        h, xc, route, counts = _merge(ya, yb, gates, h, w_branch_hgrn[layer].astype(BF16),
                                      w_branch_dsa[layer].astype(BF16), w_out[layer].astype(BF16),
                                      ffn_norm_g[layer][None].astype(F32), wr, br)
        wg, wu, wd = _expert_weights(w_expert_gate.reshape(-1, D_MODEL, EXPERT_FF),
                                     w_expert_up.reshape(-1, D_MODEL, EXPERT_FF),
                                     w_expert_down.reshape(-1, EXPERT_FF, D_MODEL), layer)
        h = _moe(xc, route, counts, h, wg, wu, wd)
    return _final_norm(h, final_norm_g[None].astype(F32), batch, lp, seq)
```

```python
import functools

import jax
import jax.numpy as jnp
import numpy as np
from jax import lax
from jax.experimental import pallas as pl
from jax.experimental.pallas import tpu as pltpu

F32 = jnp.float32
BF16 = jnp.bfloat16

D_MODEL = 1024
CHUNK = 64
N_META = 16
FRONT = CHUNK - N_META
ROPE_THETA = 500000.0
NORM_EPS = 1e-6
LANES = 128
SUBLANES = 8

HGRN_HEADS = 4
HGRN_HEAD_DIM = 128
HGRN_WIDTH = HGRN_HEADS * HGRN_HEAD_DIM
SUB = 4
HGRN_MAX_GROUP = 11

ATT_HEADS = 8
ATT_KV_HEADS = 2
ATT_HEAD_DIM = 64
ATT_GROUP = ATT_HEADS // ATT_KV_HEADS
ATT_WIDTH = ATT_HEADS * ATT_HEAD_DIM
ATT_KV_WIDTH = ATT_KV_HEADS * ATT_HEAD_DIM
ATT_ROPE_HALF = ATT_HEAD_DIM // 4 // 2
IDX_HEADS = 8
IDX_HEAD_DIM = 32
IDX_WIDTH = IDX_HEADS * IDX_HEAD_DIM
IDX_ROPE_HALF = IDX_HEAD_DIM // 4 // 2
IDX_HEADS_PER_LANE_GROUP = LANES // IDX_HEAD_DIM
TOPK_MAX = 256
Q_BLOCK = 128
DSA_EXTENT_STEP = 2
FOLD = 64
ONES_ROWS = 16

N_GROUPS = 4
EXPERTS_PER_GROUP = 8
N_EXPERTS = N_GROUPS * EXPERTS_PER_GROUP
EXPERT_FF = D_MODEL // 4
MERGE_TILE = 512
MOE_SLOT_TILE = 512
ROW_DMA_UNROLL = 8

COL_HG = 0
COL_ATT = 4 * HGRN_WIDTH
COL_IDX = COL_ATT + ATT_WIDTH + 2 * ATT_KV_WIDTH
IDX_RAW = IDX_WIDTH + IDX_HEAD_DIM + IDX_HEADS
IDX_PAD = IDX_WIDTH + LANES
COL_GATE = COL_IDX + IDX_PAD
IN_WIDTH_PAD = COL_GATE + 2 * D_MODEL

INT_MIN = np.int32(-2 ** 31)
NEG_FLT_MAX_KEY = np.int32(-2 ** 31 + 0x800000)
NEG_INF = float("-inf")
POS_INF = float("inf")

VMEM_LIMIT = 56 * 1024 * 1024


def _dot(a, b):
    return jnp.dot(a, b, preferred_element_type=F32)


def _dot_nt(a, b):
    return lax.dot_general(a, b, (((1,), (1,)), ((), ())), preferred_element_type=F32)


def _rope(x, c, s1, s2, half):
    return x * c + pltpu.roll(x, half, 1) * s1 + pltpu.roll(x, LANES - half, 1) * s2


def _inproj_kernel(h_ref, g_ref, w_ref, tab_ref, ikg_ref, rep_ref,
                   hg_ref, gate_ref, q_ref, k_ref, v_ref, iq_ref, ik_ref, iw_ref):
    x = h_ref[...]
    xn = (x * lax.rsqrt(jnp.mean(x * x, axis=-1, keepdims=True) + NORM_EPS) * g_ref[...]).astype(BF16)
    hg_ref[...] = _dot(xn, w_ref[:, COL_HG:COL_ATT])
    gate_ref[...] = _dot(xn, w_ref[:, COL_GATE:IN_WIDTH_PAD])
    att = _dot(xn, w_ref[:, COL_ATT:COL_IDX])
    idx = _dot(xn, w_ref[:, COL_IDX:COL_GATE])
    ca, s1a, s2a = tab_ref[0], tab_ref[1], tab_ref[2]
    ci, s1i, s2i = tab_ref[3], tab_ref[4], tab_ref[5]
    scale = ATT_HEAD_DIM ** -0.5
    for m in range(ATT_WIDTH // LANES):
        sl = slice(m * LANES, (m + 1) * LANES)
        q_ref[:, sl] = (_rope(att[:, sl], ca, s1a, s2a, ATT_ROPE_HALF) * scale).astype(BF16)
    k_ref[...] = _rope(att[:, ATT_WIDTH:ATT_WIDTH + LANES], ca, s1a, s2a, ATT_ROPE_HALF).astype(BF16)
    v_ref[...] = att[:, ATT_WIDTH + LANES:].astype(BF16)
    for m in range(IDX_WIDTH // LANES):
        sl = slice(m * LANES, (m + 1) * LANES)
        iq_ref[:, sl] = _rope(idx[:, sl], ci, s1i, s2i, IDX_ROPE_HALF).astype(BF16)
    xk = idx[:, IDX_WIDTH:]
    lane = lax.broadcasted_iota(jnp.int32, xk.shape, 1)
    ms = jnp.sum(jnp.where(lane < IDX_HEAD_DIM, xk * xk, 0.0), axis=-1, keepdims=True) / IDX_HEAD_DIM
    ikn = xk * lax.rsqrt(ms + NORM_EPS) * ikg_ref[...]
    ikr = _rope(ikn, ci, s1i, s2i, IDX_ROPE_HALF).astype(BF16)
    ik_ref[...] = _dot(ikr, rep_ref[...]).astype(BF16)
    iw_ref[...] = xk * (IDX_HEADS ** -0.5 * IDX_HEAD_DIM ** -0.5)


def _inproj(h, g, w, tabs, ikg, rep, lp):
    n = h.shape[0]
    tm = lp // 8
    per = lp // tm
    row = lambda i: (i, 0)
    const2 = lambda i: (0, 0)
    out_shape = (
        jax.ShapeDtypeStruct((n, 4 * HGRN_WIDTH), F32),
        jax.ShapeDtypeStruct((n, 2 * D_MODEL), F32),
        jax.ShapeDtypeStruct((n, ATT_WIDTH), BF16),
        jax.ShapeDtypeStruct((n, LANES), BF16),
        jax.ShapeDtypeStruct((n, LANES), BF16),
        jax.ShapeDtypeStruct((n, IDX_WIDTH), BF16),
        jax.ShapeDtypeStruct((n, LANES), BF16),
        jax.ShapeDtypeStruct((n, LANES), F32),
    )
    return pl.pallas_call(
        _inproj_kernel,
        grid=(n // tm,),
        in_specs=[
            pl.BlockSpec((tm, D_MODEL), row),
            pl.BlockSpec((1, D_MODEL), const2),
            pl.BlockSpec((D_MODEL, IN_WIDTH_PAD), const2),
            pl.BlockSpec((6, tm, LANES), lambda i: (0, i % per, 0)),
            pl.BlockSpec((1, LANES), const2),
            pl.BlockSpec((LANES, LANES), const2),
        ],
        out_specs=tuple(pl.BlockSpec((tm, s.shape[1]), row) for s in out_shape),
        out_shape=out_shape,
        compiler_params=pltpu.CompilerParams(
            dimension_semantics=("arbitrary",), vmem_limit_bytes=VMEM_LIMIT),
        name="inproj",
    )(h, g, w, tabs, ikg, rep)


def _hgrn_kernel(q_ref, f_ref, i_ref, g_ref, lb_ref, ng_ref, o_ref, st_ref, kb_ref, bb_ref, vb_ref,
                 *, n_chunks, group):
    C = CHUNK
    D = HGRN_HEAD_DIM
    st_ref[...] = jnp.zeros_like(st_ref)
    kb_ref[...] = jnp.zeros_like(kb_ref)
    bb_ref[...] = jnp.zeros_like(bb_ref)
    vb_ref[...] = jnp.zeros_like(vb_ref)
    lb = lb_ref[...]
    ng = ng_ref[...]
    R = group * C
    tri_row = lax.broadcasted_iota(jnp.int32, (C, C), 0)
    tri_col = lax.broadcasted_iota(jnp.int32, (C, C), 1)
    tri = (tri_row >= tri_col).astype(F32)
    local_row = lax.broadcasted_iota(jnp.int32, (R, 1), 0)
    sub_pos = local_row & (SUB - 1)

    def chunk(x, u):
        return x[u * C:(u + 1) * C, :]

    def body(it, carry):
        rows = pl.ds(pl.multiple_of(it * R, C), R)
        z = f_ref[rows, :]
        q = q_ref[rows, :]
        v = i_ref[rows, :]
        go = g_ref[rows, :]
        pad = (it * R + local_row) < FRONT
        f = lb + (1.0 - lb) * jax.nn.sigmoid(z)
        logf = jnp.where(pad, 0.0, jnp.log(f))
        k = jnp.where(pad, 0.0, (1.0 - lb) * jax.nn.sigmoid(-z))
        v = jnp.where(pad, 0.0, v)
        b = jnp.concatenate(
            [jnp.dot(tri, chunk(logf, u), preferred_element_type=F32, precision=lax.Precision.HIGHEST)
             for u in range(group)], axis=0)
        b3 = b.reshape(group, C, D)
        b_last = b3[:, C - 1:C, :]
        qd = (q * jnp.exp(b)).astype(BF16)
        kd = (k.reshape(group, C, D) * jnp.exp(b_last - b3)).reshape(R, D).astype(BF16)
        e_last = jnp.exp(b_last)
        upd =[_dot(chunk(v, u).T.astype(BF16), chunk(kd, u)) for u in range(group)]

        st = st_ref[...]
        before = []
        for u in range(group):
            before.append(st.astype(BF16))
            st = st * e_last[u] + upd[u]
        st_ref[...] = st
        o = jnp.concatenate([_dot_nt(chunk(qd, u), before[u]) for u in range(group)], axis=0)

        kb_ref[SUBLANES:, :] = k
        bb_ref[SUBLANES:, :] = b
        vb_ref[SUBLANES:, :] = v
        for j in range(SUB):
            ks = kb_ref[SUBLANES - j:SUBLANES - j + R, :]
            bs = bb_ref[SUBLANES - j:SUBLANES - j + R, :]
            vs = vb_ref[SUBLANES - j:SUBLANES - j + R, :]
            w = jnp.exp(jnp.minimum(b - bs, 0.0))
            a = jnp.sum(q * ks * w, axis=-1, keepdims=True)
            o = o + jnp.where(sub_pos >= j, a, 0.0) * vs

        att = [jnp.zeros((C, C), F32) for _ in range(group)]
        span = C
        while span > SUB:
            half = span // 2
            upper = (local_row & (span - 1)) >= half
            bs3 = b.reshape(R // span, span, D)
            mid = bs3[:, half - 1:half, :]
            qs = jnp.where(upper, (q.reshape(bs3.shape) * jnp.exp(jnp.minimum(bs3 - mid, 0.0))).reshape(R, D), 0.0)
            ks = jnp.where(upper, 0.0, (k.reshape(bs3.shape) * jnp.exp(jnp.minimum(mid - bs3, 0.0))).reshape(R, D))
            qs, ks = qs.astype(BF16), ks.astype(BF16)
            same_block = (tri_row & -span) == (tri_col & -span)
            att = [att[u] + jnp.where(same_block, _dot_nt(chunk(qs, u), chunk(ks, u)), 0.0) for u in range(group)]
            span = half
        o = o + jnp.concatenate([_dot(att[u].astype(BF16), chunk(v, u).astype(BF16)) for u in range(group)], axis=0)

        on = o * lax.rsqrt(jnp.mean(o * o, axis=-1, keepdims=True) + NORM_EPS) * ng
        o_ref[rows, :] = (on * (go * jax.nn.sigmoid(go))).astype(o_ref.dtype)
        return carry

    lax.fori_loop(0, n_chunks // group, body, 0)
    tail = o_ref.shape[0] - n_chunks * C
    if tail:
        o_ref[n_chunks * C:, :] = jnp.zeros((tail, D), o_ref.dtype)


def _hgrn(hg, lb, ng, batch, lp, n_chunks):
    n = hg.shape[0]
    H, D = HGRN_HEADS, HGRN_HEAD_DIM
    group = max(g for g in range(1, HGRN_MAX_GROUP + 1) if n_chunks % g == 0)

    def col(k):
        return pl.BlockSpec((lp, D), lambda b, h: (b, k * H + h))

    vec = pl.BlockSpec((1, D), lambda b, h: (0, h))
    return pl.pallas_call(
        functools.partial(_hgrn_kernel, n_chunks=n_chunks, group=group),
        grid=(batch, H),
        in_specs=[col(0), col(1), col(2), col(3), vec, vec],
        out_specs=pl.BlockSpec((lp, D), lambda b, h: (b, h)),
        out_shape=jax.ShapeDtypeStruct((n, HGRN_WIDTH), BF16),
        scratch_shapes=[
            pltpu.VMEM((D, D), F32),
            pltpu.VMEM((group * CHUNK + SUBLANES, D), F32),
            pltpu.VMEM((group * CHUNK + SUBLANES, D), F32),
            pltpu.VMEM((group * CHUNK + SUBLANES, D), F32),
        ],
        compiler_params=pltpu.CompilerParams(
            dimension_semantics=("arbitrary", "arbitrary"), vmem_limit_bytes=VMEM_LIMIT),
        name="hgrn2",
    )(hg, hg, hg, hg, lb, ng)


def _sortable_to_f32(key):
    bits = jnp.where(key < 0, key ^ jnp.int32(0x7FFFFFFF), key)
    return pltpu.bitcast(bits, F32)


def _fold_rows(x, op):
    rows = x.shape[0]
    if rows > FOLD and rows % FOLD == 0:
        x = op(x.reshape(rows // FOLD, FOLD, x.shape[1]), axis=0)
    return op(x, axis=0, keepdims=True)


def _dsa_kernel(q_ref, iq_ref, iw_ref, k_ref, v_ref, ik_ref, ltri_ref, o_ref,
                sc_ref, vt_ref, iq8_ref, qg_ref, ot_ref, *, topk, n_valid, extents):
    QB = Q_BLOCK
    lp = k_ref.shape[0]
    t = pl.program_id(1)
    kf = float(topk)

    @pl.when(t == 0)
    def _():
        for kb in range(lp // QB):
            sl = slice(kb * QB, (kb + 1) * QB)
            vt = v_ref[sl, :].astype(F32).T.astype(BF16)
            for g in range(ATT_KV_HEADS):
                vt_ref[g, :ATT_HEAD_DIM, sl] = vt[g * ATT_HEAD_DIM:(g + 1) * ATT_HEAD_DIM, :]
        for g in range(ATT_KV_HEADS):
            vt_ref[g, ATT_HEAD_DIM:, :] = jnp.ones((ONES_ROWS, lp), BF16)

    lane = lax.broadcasted_iota(jnp.int32, (QB, LANES), 1)
    iq = iq_ref[...]
    for h in range(IDX_HEADS):
        grp, slot = divmod(h, IDX_HEADS_PER_LANE_GROUP)
        part = iq[:, grp * LANES:(grp + 1) * LANES]
        iq8_ref[h * QB:(h + 1) * QB, :] = jnp.where((lane >> 5) == slot, part, jnp.zeros_like(part))
    wt = iw_ref[...].T
    lane_half = lane >> 6
    for h in range(ATT_HEADS):
        g, r = divmod(h, ATT_GROUP)
        m, p = divmod(h, LANES // ATT_HEAD_DIM)
        qh = q_ref[:, m * LANES:(m + 1) * LANES].astype(F32)
        if p != g:
            qh = pltpu.roll(qh, ATT_HEAD_DIM, 1)
        qg_ref[g, r * QB:(r + 1) * QB, :] = jnp.where(lane_half == g, qh, 0.0).astype(BF16)

    qpos = t * QB + lax.broadcasted_iota(jnp.int32, (1, QB), 1)
    qchunk = qpos >> 6
    n_adm = jnp.minimum((qchunk + 1) * CHUNK, n_valid) - FRONT
    take_all = n_adm <= topk
    ltri = ltri_ref[...]

    def tile_body(n_blocks):
        ke = n_blocks * QB
        keys = slice(0, ke)
        kpos = lax.broadcasted_iota(jnp.int32, (ke, 1), 0)
        adm = ((kpos >> 6) <= qchunk) & (kpos >= FRONT) & (kpos < n_valid)

        ik = ik_ref[keys, :]
        acc = jnp.zeros((ke, QB), F32)
        for h in range(IDX_HEADS):
            rel = jnp.maximum(_dot_nt(ik, iq8_ref[h * QB:(h + 1) * QB, :]), 0.0)
            acc = acc + rel * wt[IDX_HEAD_DIM + h:IDX_HEAD_DIM + h + 1, :]
        acc = jnp.where(acc == 0.0, 0.0, acc)
        sc_ref[keys, :] = jnp.where(adm, acc, NEG_INF)

        def count_ge(cand):
            return _fold_rows(jnp.where(sc_ref[keys, :] >= cand, 1.0, 0.0), jnp.sum)

        def counts(cand):
            x = sc_ref[keys, :]
            return (_fold_rows(jnp.where(x >= cand, 1.0, 0.0), jnp.sum),
                    _fold_rows(jnp.where(x > cand, 1.0, 0.0), jnp.sum))

        def search(i, tkey):
            cand = tkey + lax.shift_left(jnp.int32(1), 31 - i)
            return jnp.where(count_ge(_sortable_to_f32(cand)) >= kf, cand, tkey)

        tkey = lax.fori_loop(0, 32, search, jnp.full((1, QB), INT_MIN, jnp.int32))
        thr = _sortable_to_f32(jnp.where(take_all, NEG_FLT_MAX_KEY, tkey))
        c_ge, c_gt = counts(thr)

        def unsettled(c_ge, c_gt):
            live = jnp.logical_not(take_all)
            return (c_gt >= kf) & live, (c_ge < kf) & live

        def cond(carry):
            up, dn = unsettled(carry[1], carry[2])
            return jnp.max(jnp.where(up | dn, 1.0, 0.0)) > 0.0

        def fix(carry):
            thr, c_ge, c_gt = carry
            up, dn = unsettled(c_ge, c_gt)
            x = sc_ref[keys, :]
            above = _fold_rows(jnp.where(x > thr, x, POS_INF), jnp.min)
            below = _fold_rows(jnp.where(x < thr, x, NEG_INF), jnp.max)
            thr = jnp.where(up, above, jnp.where(dn, below, thr))
            return (thr,) + counts(thr)

        thr, c_ge, c_gt = lax.while_loop(cond, fix, (thr, c_ge, c_gt))

        need = kf - c_gt
        seen = jnp.zeros((1, QB), F32)
        for kb in range(n_blocks):
            blk = slice(kb * QB, (kb + 1) * QB)
            x = sc_ref[blk, :]
            eq = x == thr
            pre = _dot(ltri, jnp.where(eq, 1.0, 0.0).astype(BF16))
            take = (x > thr) | (eq & (pre + seen <= need))
            sc_ref[blk, :] = jnp.where(take, 0.0, NEG_INF)
            seen = seen + pre[QB - 1:QB, :]

        bias = sc_ref[keys, :]
        bias = jnp.concatenate([bias] * ATT_GROUP, axis=1)
        kk = k_ref[keys, :]
        for g in range(ATT_KV_HEADS):
            s = _dot_nt(kk, qg_ref[g]) + bias
            e = jnp.exp((s - _fold_rows(s, jnp.max)).astype(BF16))
            pv = _dot(vt_ref[g, :, keys], e)
            pv = pv[:ATT_HEAD_DIM, :] / pv[ATT_HEAD_DIM:ATT_HEAD_DIM + 1, :]
            for r in range(ATT_GROUP):
                h = g * ATT_GROUP + r
                ot_ref[h * ATT_HEAD_DIM:(h + 1) * ATT_HEAD_DIM, :] = pv[:, r * QB:(r + 1) * QB]
        o_ref[...] = ot_ref[...].T.astype(o_ref.dtype)

    lo = 0
    for n_blocks in extents:
        pl.when((t >= lo) & (t < n_blocks))(functools.partial(tile_body, n_blocks))
        lo = n_blocks


def _dsa(q, k, v, iq, ik, iw, ltri, batch, lp, topk, n_valid):
    n = q.shape[0]
    nq = lp // Q_BLOCK
    extents = tuple(sorted({nq - DSA_EXTENT_STEP * i for i in range(-(-nq // DSA_EXTENT_STEP))}))
    qblk = lambda b, i: (b * nq + i, 0)
    bat = lambda b, i: (b, 0)
    return pl.pallas_call(
        functools.partial(_dsa_kernel, topk=topk, n_valid=n_valid, extents=extents),
        grid=(batch, nq),
        in_specs=[
            pl.BlockSpec((Q_BLOCK, ATT_WIDTH), qblk),
            pl.BlockSpec((Q_BLOCK, IDX_WIDTH), qblk),
            pl.BlockSpec((Q_BLOCK, LANES), qblk),
            pl.BlockSpec((lp, LANES), bat),
            pl.BlockSpec((lp, LANES), bat),
            pl.BlockSpec((lp, LANES), bat),
            pl.BlockSpec((Q_BLOCK, Q_BLOCK), lambda b, i: (0, 0)),
        ],
        out_specs=pl.BlockSpec((Q_BLOCK, ATT_WIDTH), qblk),
        out_shape=jax.ShapeDtypeStruct((n, ATT_WIDTH), BF16),
        scratch_shapes=[
            pltpu.VMEM((lp, Q_BLOCK), F32),
            pltpu.VMEM((ATT_KV_HEADS, ATT_HEAD_DIM + ONES_ROWS, lp), BF16),
            pltpu.VMEM((IDX_HEADS * Q_BLOCK, LANES), BF16),
            pltpu.VMEM((ATT_KV_HEADS, ATT_GROUP * Q_BLOCK, LANES), BF16),
            pltpu.VMEM((ATT_WIDTH, Q_BLOCK), F32),
        ],
        compiler_params=pltpu.CompilerParams(
            dimension_semantics=("arbitrary", "arbitrary"), vmem_limit_bytes=VMEM_LIMIT),
        name="dsa",
    )(q, iq, iw, k, v, ik, ltri)


def _merge_kernel(ya_ref, yb_ref, gate_ref, h_ref, wa_ref, wb_ref, wo_ref, fg_ref, wr_ref, br_ref, ltri_ref,
                  ho_ref, xc_ref, route_ref, cnt_ref):
    @pl.when(pl.program_id(0) == 0)
    def _():
        cnt_ref[...] = jnp.zeros_like(cnt_ref)

    gate = gate_ref[...]
    pa = _dot(ya_ref[...], wa_ref[...])
    pb = _dot(yb_ref[...], wb_ref[...])
    merged = jax.nn.sigmoid(gate[:, :D_MODEL]) * pa + jax.nn.sigmoid(gate[:, D_MODEL:]) * pb
    h = h_ref[...] + _dot(merged.astype(BF16), wo_ref[...])
    ho_ref[...] = h
    xn = h * lax.rsqrt(jnp.mean(h * h, axis=-1, keepdims=True) + NORM_EPS) * fg_ref[...]
    xh = xn.astype(BF16)
    xc_ref[:, :D_MODEL] = xh.astype(F32)

    xl = (xn - xh.astype(F32)).astype(BF16)
    r = _dot(xh, wr_ref[...])
    lg = r[:, :LANES] + r[:, LANES:] + _dot(xl, wr_ref[:, :LANES]) + br_ref[...]
    lane = lax.broadcasted_iota(jnp.int32, lg.shape, 1)
    lane_f = lane.astype(F32)
    big = float(LANES)
    is_g = (lane >= N_EXPERTS) & (lane < N_EXPERTS + N_GROUPS)
    gl = jnp.where(is_g, lg, NEG_INF)
    gmax = jnp.max(gl, axis=-1, keepdims=True)
    pg_top = 1.0 / jnp.sum(jnp.exp(gl - gmax), axis=-1, keepdims=True)
    g_lane = jnp.min(jnp.where(gl == gmax, lane_f, big), axis=-1, keepdims=True)
    e_lo = (g_lane - N_EXPERTS) * EXPERTS_PER_GROUP
    in_grp = (lane_f >= e_lo) & (lane_f < e_lo + EXPERTS_PER_GROUP)
    el = jnp.where(in_grp, lg, NEG_INF)
    ee = jnp.exp(el - jnp.max(el, axis=-1, keepdims=True))
    pe = ee / jnp.sum(ee, axis=-1, keepdims=True)
    pe = jnp.where(in_grp, pe, -1.0)
    p1 = jnp.max(pe, axis=-1, keepdims=True)
    i1 = jnp.min(jnp.where(pe == p1, lane_f, big), axis=-1, keepdims=True)
    pe2 = jnp.where(lane_f == i1, -1.0, pe)
    p2 = jnp.max(pe2, axis=-1, keepdims=True)
    i2 = jnp.min(jnp.where(pe2 == p2, lane_f, big), axis=-1, keepdims=True)
    tot = p1 + p2
    xc_ref[:, D_MODEL:] = jnp.where(lane_f == i1, p1 / tot * pg_top,
                                    jnp.where(lane_f == i2, p2 / tot * pg_top, 0.0))

    grp = g_lane - N_EXPERTS
    onehot = jnp.where(lane_f == grp, 1.0, 0.0)
    incl = _dot(ltri_ref[...], onehot.astype(BF16))
    seen = cnt_ref[0:1, :]
    rank = jnp.sum(onehot * (incl - 1.0 + seen), axis=-1, keepdims=True)
    route_ref[...] = jnp.where(lane == 0, grp, jnp.where(lane == 1, rank, 0.0))
    cnt_ref[...] = jnp.broadcast_to(seen + incl[incl.shape[0] - 1:, :], cnt_ref.shape)


def _merge(ya, yb, gates, h, wa, wb, wo, fg, wr, br):
    n = h.shape[0]
    tm = min(MERGE_TILE, n)
    ltri = jnp.asarray(np.tril(np.ones((tm, tm), np.float32)), BF16)
    row = lambda i: (i, 0)
    const = lambda i: (0, 0)
    return pl.pallas_call(
        _merge_kernel,
        grid=(n // tm,),
        in_specs=[
            pl.BlockSpec((tm, HGRN_WIDTH), row),
            pl.BlockSpec((tm, ATT_WIDTH), row),
            pl.BlockSpec((tm, 2 * D_MODEL), row),
            pl.BlockSpec((tm, D_MODEL), row),
            pl.BlockSpec((HGRN_WIDTH, D_MODEL), const),
            pl.BlockSpec((ATT_WIDTH, D_MODEL), const),
            pl.BlockSpec((D_MODEL, D_MODEL), const),
            pl.BlockSpec((1, D_MODEL), const),
            pl.BlockSpec((D_MODEL, 2 * LANES), const),
            pl.BlockSpec((1, LANES), const),
            pl.BlockSpec((tm, tm), const),
        ],
        out_specs=(pl.BlockSpec((tm, D_MODEL), row), pl.BlockSpec((tm, D_MODEL + LANES), row),
                   pl.BlockSpec((tm, LANES), row), pl.BlockSpec((SUBLANES, LANES), const)),
        out_shape=(jax.ShapeDtypeStruct((n, D_MODEL), F32), jax.ShapeDtypeStruct((n, D_MODEL + LANES), F32),
                   jax.ShapeDtypeStruct((n, LANES), F32), jax.ShapeDtypeStruct((SUBLANES, LANES), F32)),
        compiler_params=pltpu.CompilerParams(
            dimension_semantics=("arbitrary",), vmem_limit_bytes=VMEM_LIMIT),
        name="merge_router",
    )(ya, yb, gates, h, wa, wb, wo, fg, wr, br, ltri)


def _row_copies(n_rows, make_copy, whole_copy):
    def issue(i, c):
        for u in range(ROW_DMA_UNROLL):
            make_copy(i * ROW_DMA_UNROLL + u).start(priority=u % 2)
        return c

    lax.fori_loop(0, n_rows // ROW_DMA_UNROLL, issue, 0)
    whole_copy.wait()


def _dispatch_kernel(dest_ref, x_ref, zeros_ref, xs_ref, sem):
    del zeros_ref
    tm = x_ref.shape[0]

    def row_copy(r):
        return pltpu.make_async_copy(x_ref.at[pl.ds(r, 1)], xs_ref.at[pl.ds(dest_ref[0, 0, r], 1)], sem)

    _row_copies(tm, row_copy, pltpu.make_async_copy(x_ref, xs_ref.at[pl.ds(0, tm)], sem))


def _experts_kernel(tile_group_ref, tile_on_ref, xs_ref, wg_ref, wu_ref, wd_ref, ys_ref):
    t = pl.program_id(0)
    tm = xs_ref.shape[0]

    @pl.when(tile_on_ref[t] == 1)
    def _():
        x = xs_ref[:, :D_MODEL].astype(BF16)
        c = xs_ref[:, D_MODEL:]
        lane = lax.broadcasted_iota(jnp.int32, c.shape, 1)
        first = tile_group_ref[t] * EXPERTS_PER_GROUP
        scale = jnp.concatenate(
            [jnp.broadcast_to(jnp.sum(jnp.where(lane == first + e, c, 0.0), axis=-1, keepdims=True), (tm, EXPERT_FF))
             for e in range(EXPERTS_PER_GROUP)], axis=1)
        a = _dot(x, wg_ref[0])
        u = _dot(x, wu_ref[0])
        act = (a * jax.nn.sigmoid(a)) * u * scale
        ys_ref[...] = _dot(act.astype(BF16), wd_ref[0])

    @pl.when(tile_on_ref[t] == 0)
    def _():
        ys_ref[...] = jnp.zeros_like(ys_ref)


def _combine_kernel(dest_ref, h_ref, ys_ref, o_ref, buf_ref, sem):
    tm = h_ref.shape[0]

    def row_copy(r):
        return pltpu.make_async_copy(ys_ref.at[pl.ds(dest_ref[0, 0, r], 1)], buf_ref.at[pl.ds(r, 1)], sem)

    _row_copies(tm, row_copy, pltpu.make_async_copy(ys_ref.at[pl.ds(0, tm)], buf_ref, sem))
    o_ref[...] = h_ref[...] + buf_ref[...]


def _expert_weights_kernel(g_ref, u_ref, d_ref, go_ref, uo_ref, do_ref):
    go_ref[0] = g_ref[0].astype(BF16)
    uo_ref[0] = u_ref[0].astype(BF16)
    do_ref[0] = d_ref[0].astype(BF16)


def _expert_weights(w_gate, w_up, w_down, layer):
    per = EXPERTS_PER_GROUP
    src = lambda shape: pl.BlockSpec((1,) + shape, lambda e: (layer * N_EXPERTS + e, 0, 0))
    side = pl.BlockSpec((1, D_MODEL, EXPERT_FF), lambda e: (e // per, 0, e % per))
    stack = pl.BlockSpec((1, EXPERT_FF, D_MODEL), lambda e: (e // per, e % per, 0))
    return pl.pallas_call(
        _expert_weights_kernel,
        grid=(N_EXPERTS,),
        in_specs=[src((D_MODEL, EXPERT_FF)), src((D_MODEL, EXPERT_FF)), src((EXPERT_FF, D_MODEL))],
        out_specs=(side, side, stack),
        out_shape=(jax.ShapeDtypeStruct((N_GROUPS, D_MODEL, per * EXPERT_FF), BF16),
                   jax.ShapeDtypeStruct((N_GROUPS, D_MODEL, per * EXPERT_FF), BF16),
                   jax.ShapeDtypeStruct((N_GROUPS, per * EXPERT_FF, D_MODEL), BF16)),
        compiler_params=pltpu.CompilerParams(
            dimension_semantics=("arbitrary",), vmem_limit_bytes=VMEM_LIMIT),
        name="expert_weights",
    )(w_gate, w_up, w_down)


def _moe(xc, route, counts, h, wg, wu, wd):
    n = h.shape[0]
    tm = min(MERGE_TILE, n)
    ts = min(MOE_SLOT_TILE, n)
    n_tiles = n // ts + N_GROUPS
    n_slots = n_tiles * ts

    grp = route[:, 0].astype(jnp.int32)
    rank = route[:, 1].astype(jnp.int32)
    cnt = counts[0, :N_GROUPS].astype(jnp.int32)
    padded = (cnt + ts - 1) // ts * ts
    ends = jnp.cumsum(padded)
    starts = ends - padded
    dest = (starts[grp] + rank).reshape(n // tm, 1, tm)
    tile_start = jnp.arange(n_tiles, dtype=jnp.int32) * ts
    tile_group = jnp.minimum(jnp.sum(tile_start[:, None] >= ends[None, :], axis=1), N_GROUPS - 1).astype(jnp.int32)
    tile_on = (tile_start < ends[N_GROUPS - 1]).astype(jnp.int32)

    tok = lambda i: (i, 0)
    smem_idx = pl.BlockSpec((1, 1, tm), lambda i: (i, 0, 0), memory_space=pltpu.SMEM)
    hbm = pl.BlockSpec(memory_space=pl.ANY)
    xs = pl.pallas_call(
        _dispatch_kernel,
        grid=(n // tm,),
        in_specs=[smem_idx, pl.BlockSpec((tm, D_MODEL + LANES), tok), hbm],
        out_specs=hbm,
        out_shape=jax.ShapeDtypeStruct((n_slots, D_MODEL + LANES), F32),
        scratch_shapes=[pltpu.SemaphoreType.DMA(())],
        input_output_aliases={2: 0},
        compiler_params=pltpu.CompilerParams(
            dimension_semantics=("arbitrary",), vmem_limit_bytes=VMEM_LIMIT),
        name="moe_dispatch",
    )(dest, xc, jnp.zeros((n_slots, D_MODEL + LANES), F32))

    wspec = lambda shape: pl.BlockSpec((1,) + shape, lambda t, tg, on: (tg[t], 0, 0))
    ys = pl.pallas_call(
        _experts_kernel,
        grid_spec=pltpu.PrefetchScalarGridSpec(
            num_scalar_prefetch=2,
            grid=(n_tiles,),
            in_specs=[pl.BlockSpec((ts, D_MODEL + LANES), lambda t, tg, on: (t, 0)),
                      wspec((D_MODEL, EXPERTS_PER_GROUP * EXPERT_FF)),
                      wspec((D_MODEL, EXPERTS_PER_GROUP * EXPERT_FF)),
                      wspec((EXPERTS_PER_GROUP * EXPERT_FF, D_MODEL))],
            out_specs=pl.BlockSpec((ts, D_MODEL), lambda t, tg, on: (t, 0)),
        ),
        out_shape=jax.ShapeDtypeStruct((n_slots, D_MODEL), F32),
        compiler_params=pltpu.CompilerParams(
            dimension_semantics=("arbitrary",), vmem_limit_bytes=VMEM_LIMIT),
        name="moe_experts",
    )(tile_group, tile_on, xs, wg, wu, wd)

    return pl.pallas_call(
        _combine_kernel,
        grid=(n // tm,),
        in_specs=[smem_idx, pl.BlockSpec((tm, D_MODEL), tok), hbm],
        out_specs=pl.BlockSpec((tm, D_MODEL), tok),
        out_shape=jax.ShapeDtypeStruct((n, D_MODEL), F32),
        scratch_shapes=[pltpu.VMEM((tm, D_MODEL), F32), pltpu.SemaphoreType.DMA(())],
        compiler_params=pltpu.CompilerParams(
            dimension_semantics=("arbitrary",), vmem_limit_bytes=VMEM_LIMIT),
        name="moe_combine",
    )(dest, h, ys)


def _final_kernel(h_ref, g_ref, o_ref, *, seq):
    step = 256 if seq % 256 == 0 else CHUNK
    for r in range(0, seq, step):
        x = h_ref[CHUNK + r:CHUNK + r + step, :]
        o_ref[0, r:r + step, :] = x * lax.rsqrt(jnp.mean(x * x, axis=-1, keepdims=True) + NORM_EPS) * g_ref[...]


def _final_norm(h, g, batch, lp, seq):
    return pl.pallas_call(
        functools.partial(_final_kernel, seq=seq),
        grid=(batch,),
        in_specs=[pl.BlockSpec((lp, D_MODEL), lambda b: (b, 0)), pl.BlockSpec((1, D_MODEL), lambda b: (0, 0))],
        out_specs=pl.BlockSpec((1, seq, D_MODEL), lambda b: (b, 0, 0)),
        out_shape=jax.ShapeDtypeStruct((batch, seq, D_MODEL), F32),
        compiler_params=pltpu.CompilerParams(
            dimension_semantics=("arbitrary",), vmem_limit_bytes=VMEM_LIMIT),
        name="final_norm",
    )(h, g)


def _rope_lane_tables(pos, head_dim):
    rot = head_dim // 4
    half = rot // 2
    inv = ROPE_THETA ** (-jnp.arange(0, rot, 2, dtype=F32) / rot)
    ang = pos[:, None] * inv[None, :]
    cos, sin = jnp.cos(ang), jnp.sin(ang)
    jj = np.arange(LANES) % head_dim
    first = jnp.asarray(jj < half)
    second = jnp.asarray((jj >= half) & (jj < rot))
    fidx = jnp.asarray(np.where(jj < half, jj, np.where(jj < rot, jj - half, 0)))
    cl, sl = cos[:, fidx], sin[:, fidx]
    c = jnp.where(first | second, cl, 1.0)
    s1 = jnp.where(second, sl, 0.0)
    s2 = jnp.where(first, -sl, 0.0)
    return c, s1, s2


def kernel(x, meta_tokens, mix_norm_g, w_in, hgrn_lb_logits, hgrn_norm_g, idx_k_norm_g, w_branch_hgrn,
           w_branch_dsa, w_out, ffn_norm_g, router_group_w, router_group_b, router_expert_w,
           router_expert_b, w_expert_gate, w_expert_up, w_expert_down, final_norm_g):
    batch, seq, _ = x.shape
    depth = w_in.shape[0]
    n_valid = CHUNK + seq
    assert seq % CHUNK == 0
    lp = -(-n_valid // Q_BLOCK) * Q_BLOCK
    n_chunks = n_valid // CHUNK
    topk = min(TOPK_MAX, (N_META + seq) // 4)

    meta = jnp.broadcast_to(meta_tokens[None].astype(x.dtype), (batch, N_META, D_MODEL))
    h = jnp.concatenate([jnp.zeros((batch, FRONT, D_MODEL), x.dtype), meta, x,
                         jnp.zeros((batch, lp - n_valid, D_MODEL), x.dtype)], axis=1)
    h = h.reshape(batch * lp, D_MODEL)

    pos = jnp.clip(jnp.arange(lp, dtype=jnp.int32) - FRONT, 0, N_META + seq - 1).astype(F32)
    tabs = jnp.stack(_rope_lane_tables(pos, ATT_HEAD_DIM) + _rope_lane_tables(pos, IDX_HEAD_DIM))

    p = jax.nn.softmax(hgrn_lb_logits.astype(F32), axis=0)
    cs = jnp.cumsum(p, axis=0)
    lower_bounds = cs - cs[0:1]

    rep = np.zeros((LANES, LANES), np.float32)
    for slot in range(IDX_HEADS_PER_LANE_GROUP):
        rep[np.arange(IDX_HEAD_DIM), slot * IDX_HEAD_DIM + np.arange(IDX_HEAD_DIM)] = 1.0
    rep = jnp.asarray(rep, BF16)
    ltri = jnp.asarray(np.tril(np.ones((Q_BLOCK, Q_BLOCK), np.float32)), BF16)

    pad_cols = jnp.zeros((D_MODEL, IDX_PAD - IDX_RAW), F32)
    for layer in range(depth):
        w = w_in[layer]
        w_p = jnp.concatenate([w[:, :COL_IDX + IDX_RAW], pad_cols, w[:, COL_IDX + IDX_RAW:]], axis=1).astype(BF16)
        ikg = jnp.pad(idx_k_norm_g[layer].astype(F32), (0, LANES - IDX_HEAD_DIM))[None]
        hg, gates, q, k, v, iq, ik, iw = _inproj(h, mix_norm_g[layer][None].astype(F32), w_p, tabs, ikg, rep, lp)
        ya = _hgrn(hg, lower_bounds[layer][None], hgrn_norm_g[layer][None].astype(F32), batch, lp, n_chunks)
        yb = _dsa(q, k, v, iq, ik, iw, ltri, batch, lp, topk, n_valid)
        wr = jnp.concatenate([router_expert_w[layer], router_group_w[layer],
                              jnp.zeros((D_MODEL, LANES - N_EXPERTS - N_GROUPS), F32)], axis=1)
        wr_hi = wr.astype(BF16)
        wr = jnp.concatenate([wr_hi, (wr - wr_hi.astype(F32)).astype(BF16)], axis=1)
        br = jnp.concatenate([router_expert_b[layer], router_group_b[layer],
                              jnp.zeros((LANES - N_EXPERTS - N_GROUPS,), F32)])[None]
        h, xc, route, counts = _merge(ya, yb, gates, h, w_branch_hgrn[layer].astype(BF16),
                                      w_branch_dsa[layer].astype(BF16), w_out[layer].astype(BF16),
                                      ffn_norm_g[layer][None].astype(F32), wr, br)
        wg, wu, wd = _expert_weights(w_expert_gate.reshape(-1, D_MODEL, EXPERT_FF),
                                     w_expert_up.reshape(-1, D_MODEL, EXPERT_FF),
                                     w_expert_down.reshape(-1, EXPERT_FF, D_MODEL), layer)
        h = _moe(xc, route, counts, h, wg, wu, wd)
    return _final_norm(h, final_norm_g[None].astype(F32), batch, lp, seq)
```

```python
import functools

import jax
import jax.numpy as jnp
import numpy as np
from jax import lax
from jax.experimental import pallas as pl
from jax.experimental.pallas import tpu as pltpu

F32 = jnp.float32
BF16 = jnp.bfloat16

D_MODEL = 1024
CHUNK = 64
N_META = 16
FRONT = CHUNK - N_META
ROPE_THETA = 500000.0
NORM_EPS = 1e-6
LANES = 128
SUBLANES = 8

HGRN_HEADS = 4
HGRN_HEAD_DIM = 128
HGRN_WIDTH = HGRN_HEADS * HGRN_HEAD_DIM
SUB = 4
HGRN_MAX_GROUP = 11

ATT_HEADS = 8
ATT_KV_HEADS = 2
ATT_HEAD_DIM = 64
ATT_GROUP = ATT_HEADS // ATT_KV_HEADS
ATT_WIDTH = ATT_HEADS * ATT_HEAD_DIM
ATT_KV_WIDTH = ATT_KV_HEADS * ATT_HEAD_DIM
ATT_ROPE_HALF = ATT_HEAD_DIM // 4 // 2
IDX_HEADS = 8
IDX_HEAD_DIM = 32
IDX_WIDTH = IDX_HEADS * IDX_HEAD_DIM
IDX_ROPE_HALF = IDX_HEAD_DIM // 4 // 2
IDX_HEADS_PER_LANE_GROUP = LANES // IDX_HEAD_DIM
TOPK_MAX = 256
Q_BLOCK = 128
DSA_EXTENT_STEP = 2
FOLD = 64
ONES_ROWS = 16

N_GROUPS = 4
EXPERTS_PER_GROUP = 8
N_EXPERTS = N_GROUPS * EXPERTS_PER_GROUP
EXPERT_FF = D_MODEL // 4
MERGE_TILE = 512
MOE_SLOT_TILE = 512
ROW_DMA_UNROLL = 8

COL_HG = 0
COL_ATT = 4 * HGRN_WIDTH
COL_IDX = COL_ATT + ATT_WIDTH + 2 * ATT_KV_WIDTH
IDX_RAW = IDX_WIDTH + IDX_HEAD_DIM + IDX_HEADS
IDX_PAD = IDX_WIDTH + LANES
COL_GATE = COL_IDX + IDX_PAD
IN_WIDTH_PAD = COL_GATE + 2 * D_MODEL

INT_MIN = np.int32(-2 ** 31)
NEG_FLT_MAX_KEY = np.int32(-2 ** 31 + 0x800000)
NEG_INF = float("-inf")
POS_INF = float("inf")

VMEM_LIMIT = 56 * 1024 * 1024


def _dot(a, b):
    return jnp.dot(a, b, preferred_element_type=F32)


def _dot_nt(a, b):
    return lax.dot_general(a, b, (((1,), (1,)), ((), ())), preferred_element_type=F32)


def _rope(x, c, s1, s2, half):
    return x * c + pltpu.roll(x, half, 1) * s1 + pltpu.roll(x, LANES - half, 1) * s2


def _inproj_kernel(h_ref, g_ref, w_ref, tab_ref, ikg_ref, rep_ref,
                   hg_ref, gate_ref, q_ref, k_ref, v_ref, iq_ref, ik_ref, iw_ref):
    x = h_ref[...]
    xn = (x * lax.rsqrt(jnp.mean(x * x, axis=-1, keepdims=True) + NORM_EPS) * g_ref[...]).astype(BF16)
    hg_ref[...] = _dot(xn, w_ref[:, COL_HG:COL_ATT])
    gate_ref[...] = _dot(xn, w_ref[:, COL_GATE:IN_WIDTH_PAD])
    att = _dot(xn, w_ref[:, COL_ATT:COL_IDX])
    idx = _dot(xn, w_ref[:, COL_IDX:COL_GATE])
    ca, s1a, s2a = tab_ref[0], tab_ref[1], tab_ref[2]
    ci, s1i, s2i = tab_ref[3], tab_ref[4], tab_ref[5]
    scale = ATT_HEAD_DIM ** -0.5
    for m in range(ATT_WIDTH // LANES):
        sl = slice(m * LANES, (m + 1) * LANES)
        q_ref[:, sl] = (_rope(att[:, sl], ca, s1a, s2a, ATT_ROPE_HALF) * scale).astype(BF16)
    k_ref[...] = _rope(att[:, ATT_WIDTH:ATT_WIDTH + LANES], ca, s1a, s2a, ATT_ROPE_HALF).astype(BF16)
    v_ref[...] = att[:, ATT_WIDTH + LANES:].astype(BF16)
    for m in range(IDX_WIDTH // LANES):
        sl = slice(m * LANES, (m + 1) * LANES)
        iq_ref[:, sl] = _rope(idx[:, sl], ci, s1i, s2i, IDX_ROPE_HALF).astype(BF16)
    xk = idx[:, IDX_WIDTH:]
    lane = lax.broadcasted_iota(jnp.int32, xk.shape, 1)
    ms = jnp.sum(jnp.where(lane < IDX_HEAD_DIM, xk * xk, 0.0), axis=-1, keepdims=True) / IDX_HEAD_DIM
    ikn = xk * lax.rsqrt(ms + NORM_EPS) * ikg_ref[...]
    ikr = _rope(ikn, ci, s1i, s2i, IDX_ROPE_HALF).astype(BF16)
    ik_ref[...] = _dot(ikr, rep_ref[...]).astype(BF16)
    iw_ref[...] = xk * (IDX_HEADS ** -0.5 * IDX_HEAD_DIM ** -0.5)


def _inproj(h, g, w, tabs, ikg, rep, lp):
    n = h.shape[0]
    tm = lp // 8
    per = lp // tm
    row = lambda i: (i, 0)
    const2 = lambda i: (0, 0)
    out_shape = (
        jax.ShapeDtypeStruct((n, 4 * HGRN_WIDTH), F32),
        jax.ShapeDtypeStruct((n, 2 * D_MODEL), F32),
        jax.ShapeDtypeStruct((n, ATT_WIDTH), BF16),
        jax.ShapeDtypeStruct((n, LANES), BF16),
        jax.ShapeDtypeStruct((n, LANES), BF16),
        jax.ShapeDtypeStruct((n, IDX_WIDTH), BF16),
        jax.ShapeDtypeStruct((n, LANES), BF16),
        jax.ShapeDtypeStruct((n, LANES), F32),
    )
    return pl.pallas_call(
        _inproj_kernel,
        grid=(n // tm,),
        in_specs=[
            pl.BlockSpec((tm, D_MODEL), row),
            pl.BlockSpec((1, D_MODEL), const2),
            pl.BlockSpec((D_MODEL, IN_WIDTH_PAD), const2),
            pl.BlockSpec((6, tm, LANES), lambda i: (0, i % per, 0)),
            pl.BlockSpec((1, LANES), const2),
            pl.BlockSpec((LANES, LANES), const2),
        ],
        out_specs=tuple(pl.BlockSpec((tm, s.shape[1]), row) for s in out_shape),
        out_shape=out_shape,
        compiler_params=pltpu.CompilerParams(
            dimension_semantics=("arbitrary",), vmem_limit_bytes=VMEM_LIMIT),
        name="inproj",
    )(h, g, w, tabs, ikg, rep)


def _hgrn_kernel(q_ref, f_ref, i_ref, g_ref, lb_ref, ng_ref, o_ref, st_ref, kb_ref, bb_ref, vb_ref,
                 *, n_chunks, group):
    C = CHUNK
    D = HGRN_HEAD_DIM
    st_ref[...] = jnp.zeros_like(st_ref)
    kb_ref[...] = jnp.zeros_like(kb_ref)
    bb_ref[...] = jnp.zeros_like(bb_ref)
    vb_ref[...] = jnp.zeros_like(vb_ref)
    lb = lb_ref[...]
    ng = ng_ref[...]
    R = group * C
    tri_row = lax.broadcasted_iota(jnp.int32, (C, C), 0)
    tri_col = lax.broadcasted_iota(jnp.int32, (C, C), 1)
    tri = (tri_row >= tri_col).astype(F32)
    local_row = lax.broadcasted_iota(jnp.int32, (R, 1), 0)
    sub_pos = local_row & (SUB - 1)

    def chunk(x, u):
        return x[u * C:(u + 1) * C, :]

    def body(it, carry):
        rows = pl.ds(pl.multiple_of(it * R, C), R)
        z = f_ref[rows, :]
        q = q_ref[rows, :]
        v = i_ref[rows, :]
        go = g_ref[rows, :]
        pad = (it * R + local_row) < FRONT
        f = lb + (1.0 - lb) * jax.nn.sigmoid(z)
        logf = jnp.where(pad, 0.0, jnp.log(f))
        k = jnp.where(pad, 0.0, (1.0 - lb) * jax.nn.sigmoid(-z))
        v = jnp.where(pad, 0.0, v)
        b = jnp.concatenate(
            [jnp.dot(tri, chunk(logf, u), preferred_element_type=F32, precision=lax.Precision.HIGHEST)
             for u in range(group)], axis=0)
        b3 = b.reshape(group, C, D)
        b_last = b3[:, C - 1:C, :]
        qd = (q * jnp.exp(b)).astype(BF16)
        kd = (k.reshape(group, C, D) * jnp.exp(b_last - b3)).reshape(R, D).astype(BF16)
        e_last = jnp.exp(b_last)
        upd =[_dot(chunk(v, u).T.astype(BF16), chunk(kd, u)) for u in range(group)]

        st = st_ref[...]
        before = []
        for u in range(group):
            before.append(st.astype(BF16))
            st = st * e_last[u] + upd[u]
        st_ref[...] = st
        o = jnp.concatenate([_dot_nt(chunk(qd, u), before[u]) for u in range(group)], axis=0)

        kb_ref[SUBLANES:, :] = k
        bb_ref[SUBLANES:, :] = b
        vb_ref[SUBLANES:, :] = v
        for j in range(SUB):
            ks = kb_ref[SUBLANES - j:SUBLANES - j + R, :]
            bs = bb_ref[SUBLANES - j:SUBLANES - j + R, :]
            vs = vb_ref[SUBLANES - j:SUBLANES - j + R, :]
            w = jnp.exp(b - bs)
            a = jnp.sum(q * ks * w, axis=-1, keepdims=True)
            o = o + jnp.where(sub_pos >= j, a, 0.0) * vs

        att = [jnp.zeros((C, C), F32) for _ in range(group)]
        span = C
        while span > SUB:
            half = span // 2
            upper = (local_row & (span - 1)) >= half
            bs3 = b.reshape(R // span, span, D)
            mid = bs3[:, half - 1:half, :]
            qs = jnp.where(upper, (q.reshape(bs3.shape) * jnp.exp(bs3 - mid)).reshape(R, D), 0.0)
            ks = jnp.where(upper, 0.0, (k.reshape(bs3.shape) * jnp.exp(mid - bs3)).reshape(R, D))
            qs, ks = qs.astype(BF16), ks.astype(BF16)
            same_block = (tri_row & -span) == (tri_col & -span)
            att = [att[u] + jnp.where(same_block, _dot_nt(chunk(qs, u), chunk(ks, u)), 0.0) for u in range(group)]
            span = half
        o = o + jnp.concatenate([_dot(att[u].astype(BF16), chunk(v, u).astype(BF16)) for u in range(group)], axis=0)

        on = o * lax.rsqrt(jnp.mean(o * o, axis=-1, keepdims=True) + NORM_EPS) * ng
        o_ref[rows, :] = (on * (go * jax.nn.sigmoid(go))).astype(o_ref.dtype)
        return carry

    lax.fori_loop(0, n_chunks // group, body, 0)
    tail = o_ref.shape[0] - n_chunks * C
    if tail:
        o_ref[n_chunks * C:, :] = jnp.zeros((tail, D), o_ref.dtype)


def _hgrn(hg, lb, ng, batch, lp, n_chunks):
    n = hg.shape[0]
    H, D = HGRN_HEADS, HGRN_HEAD_DIM
    group = max(g for g in range(1, HGRN_MAX_GROUP + 1) if n_chunks % g == 0)

    def col(k):
        return pl.BlockSpec((lp, D), lambda b, h: (b, k * H + h))

    vec = pl.BlockSpec((1, D), lambda b, h: (0, h))
    return pl.pallas_call(
        functools.partial(_hgrn_kernel, n_chunks=n_chunks, group=group),
        grid=(batch, H),
        in_specs=[col(0), col(1), col(2), col(3), vec, vec],
        out_specs=pl.BlockSpec((lp, D), lambda b, h: (b, h)),
        out_shape=jax.ShapeDtypeStruct((n, HGRN_WIDTH), BF16),
        scratch_shapes=[
            pltpu.VMEM((D, D), F32),
            pltpu.VMEM((group * CHUNK + SUBLANES, D), F32),
            pltpu.VMEM((group * CHUNK + SUBLANES, D), F32),
            pltpu.VMEM((group * CHUNK + SUBLANES, D), F32),
        ],
        compiler_params=pltpu.CompilerParams(
            dimension_semantics=("arbitrary", "arbitrary"), vmem_limit_bytes=VMEM_LIMIT),
        name="hgrn2",
    )(hg, hg, hg, hg, lb, ng)


def _sortable_to_f32(key):
    bits = jnp.where(key < 0, key ^ jnp.int32(0x7FFFFFFF), key)
    return pltpu.bitcast(bits, F32)


def _fold_rows(x, op):
    rows = x.shape[0]
    if rows > FOLD and rows % FOLD == 0:
        x = op(x.reshape(rows // FOLD, FOLD, x.shape[1]), axis=0)
    return op(x, axis=0, keepdims=True)


def _dsa_kernel(q_ref, iq_ref, iw_ref, k_ref, v_ref, ik_ref, ltri_ref, o_ref,
                sc_ref, vt_ref, iq8_ref, qg_ref, ot_ref, *, topk, n_valid, extents):
    QB = Q_BLOCK
    lp = k_ref.shape[0]
    t = pl.program_id(1)
    kf = float(topk)

    @pl.when(t == 0)
    def _():
        for kb in range(lp // QB):
            sl = slice(kb * QB, (kb + 1) * QB)
            vt = v_ref[sl, :].astype(F32).T.astype(BF16)
            for g in range(ATT_KV_HEADS):
                vt_ref[g, :ATT_HEAD_DIM, sl] = vt[g * ATT_HEAD_DIM:(g + 1) * ATT_HEAD_DIM, :]
        for g in range(ATT_KV_HEADS):
            vt_ref[g, ATT_HEAD_DIM:, :] = jnp.ones((ONES_ROWS, lp), BF16)

    lane = lax.broadcasted_iota(jnp.int32, (QB, LANES), 1)
    iq = iq_ref[...]
    for h in range(IDX_HEADS):
        grp, slot = divmod(h, IDX_HEADS_PER_LANE_GROUP)
        part = iq[:, grp * LANES:(grp + 1) * LANES]
        iq8_ref[h * QB:(h + 1) * QB, :] = jnp.where((lane >> 5) == slot, part, jnp.zeros_like(part))
    wt = iw_ref[...].T
    lane_half = lane >> 6
    for h in range(ATT_HEADS):
        g, r = divmod(h, ATT_GROUP)
        m, p = divmod(h, LANES // ATT_HEAD_DIM)
        qh = q_ref[:, m * LANES:(m + 1) * LANES].astype(F32)
        if p != g:
            qh = pltpu.roll(qh, ATT_HEAD_DIM, 1)
        qg_ref[g, r * QB:(r + 1) * QB, :] = jnp.where(lane_half == g, qh, 0.0).astype(BF16)

    qpos = t * QB + lax.broadcasted_iota(jnp.int32, (1, QB), 1)
    qchunk = qpos >> 6
    n_adm = jnp.minimum((qchunk + 1) * CHUNK, n_valid) - FRONT
    take_all = n_adm <= topk
    ltri = ltri_ref[...]

    def tile_body(n_blocks):
        ke = n_blocks * QB
        keys = slice(0, ke)
        kpos = lax.broadcasted_iota(jnp.int32, (ke, 1), 0)
        adm = ((kpos >> 6) <= qchunk) & (kpos >= FRONT) & (kpos < n_valid)

        ik = ik_ref[keys, :]
        acc = jnp.zeros((ke, QB), F32)
        for h in range(IDX_HEADS):
            rel = jnp.maximum(_dot_nt(ik, iq8_ref[h * QB:(h + 1) * QB, :]), 0.0)
            acc = acc + rel * wt[IDX_HEAD_DIM + h:IDX_HEAD_DIM + h + 1, :]
        acc = jnp.where(acc == 0.0, 0.0, acc)
        sc_ref[keys, :] = jnp.where(adm, acc, NEG_INF)

        def count_ge(cand):
            return _fold_rows(jnp.where(sc_ref[keys, :] >= cand, 1.0, 0.0), jnp.sum)

        def counts(cand):
            x = sc_ref[keys, :]
            return (_fold_rows(jnp.where(x >= cand, 1.0, 0.0), jnp.sum),
                    _fold_rows(jnp.where(x > cand, 1.0, 0.0), jnp.sum))

        def search(i, tkey):
            cand = tkey + lax.shift_left(jnp.int32(1), 31 - i)
            return jnp.where(count_ge(_sortable_to_f32(cand)) >= kf, cand, tkey)

        tkey = lax.fori_loop(0, 32, search, jnp.full((1, QB), INT_MIN, jnp.int32))
        thr = _sortable_to_f32(jnp.where(take_all, NEG_FLT_MAX_KEY, tkey))
        c_ge, c_gt = counts(thr)

        def unsettled(c_ge, c_gt):
            live = jnp.logical_not(take_all)
            return (c_gt >= kf) & live, (c_ge < kf) & live

        def cond(carry):
            up, dn = unsettled(carry[1], carry[2])
            return jnp.max(jnp.where(up | dn, 1.0, 0.0)) > 0.0

        def fix(carry):
            thr, c_ge, c_gt = carry
            up, dn = unsettled(c_ge, c_gt)
            x = sc_ref[keys, :]
            above = _fold_rows(jnp.where(x > thr, x, POS_INF), jnp.min)
            below = _fold_rows(jnp.where(x < thr, x, NEG_INF), jnp.max)
            thr = jnp.where(up, above, jnp.where(dn, below, thr))
            return (thr,) + counts(thr)

        thr, c_ge, c_gt = lax.while_loop(cond, fix, (thr, c_ge, c_gt))

        need = kf - c_gt
        seen = jnp.zeros((1, QB), F32)
        for kb in range(n_blocks):
            blk = slice(kb * QB, (kb + 1) * QB)
            x = sc_ref[blk, :]
            eq = x == thr
            pre = _dot(ltri, jnp.where(eq, 1.0, 0.0).astype(BF16))
            take = (x > thr) | (eq & (pre + seen <= need))
            sc_ref[blk, :] = jnp.where(take, 0.0, NEG_INF)
            seen = seen + pre[QB - 1:QB, :]

        bias = sc_ref[keys, :]
        bias = jnp.concatenate([bias] * ATT_GROUP, axis=1)
        kk = k_ref[keys, :]
        for g in range(ATT_KV_HEADS):
            s = _dot_nt(kk, qg_ref[g]) + bias
            e = jnp.exp((s - _fold_rows(s, jnp.max)).astype(BF16))
            pv = _dot(vt_ref[g, :, keys], e)
            pv = pv[:ATT_HEAD_DIM, :] / pv[ATT_HEAD_DIM:ATT_HEAD_DIM + 1, :]
            for r in range(ATT_GROUP):
                h = g * ATT_GROUP + r
                ot_ref[h * ATT_HEAD_DIM:(h + 1) * ATT_HEAD_DIM, :] = pv[:, r * QB:(r + 1) * QB]
        o_ref[...] = ot_ref[...].T.astype(o_ref.dtype)

    lo = 0
    for n_blocks in extents:
        pl.when((t >= lo) & (t < n_blocks))(functools.partial(tile_body, n_blocks))
        lo = n_blocks


def _dsa(q, k, v, iq, ik, iw, ltri, batch, lp, topk, n_valid):
    n = q.shape[0]
    nq = lp // Q_BLOCK
    extents = tuple(sorted({nq - DSA_EXTENT_STEP * i for i in range(-(-nq // DSA_EXTENT_STEP))}))
    qblk = lambda b, i: (b * nq + i, 0)
    bat = lambda b, i: (b, 0)
    return pl.pallas_call(
        functools.partial(_dsa_kernel, topk=topk, n_valid=n_valid, extents=extents),
        grid=(batch, nq),
        in_specs=[
            pl.BlockSpec((Q_BLOCK, ATT_WIDTH), qblk),
            pl.BlockSpec((Q_BLOCK, IDX_WIDTH), qblk),
            pl.BlockSpec((Q_BLOCK, LANES), qblk),
            pl.BlockSpec((lp, LANES), bat),
            pl.BlockSpec((lp, LANES), bat),
            pl.BlockSpec((lp, LANES), bat),
            pl.BlockSpec((Q_BLOCK, Q_BLOCK), lambda b, i: (0, 0)),
        ],
        out_specs=pl.BlockSpec((Q_BLOCK, ATT_WIDTH), qblk),
        out_shape=jax.ShapeDtypeStruct((n, ATT_WIDTH), BF16),
        scratch_shapes=[
            pltpu.VMEM((lp, Q_BLOCK), F32),
            pltpu.VMEM((ATT_KV_HEADS, ATT_HEAD_DIM + ONES_ROWS, lp), BF16),
            pltpu.VMEM((IDX_HEADS * Q_BLOCK, LANES), BF16),
            pltpu.VMEM((ATT_KV_HEADS, ATT_GROUP * Q_BLOCK, LANES), BF16),
            pltpu.VMEM((ATT_WIDTH, Q_BLOCK), F32),
        ],
        compiler_params=pltpu.CompilerParams(
            dimension_semantics=("arbitrary", "arbitrary"), vmem_limit_bytes=VMEM_LIMIT),
        name="dsa",
    )(q, iq, iw, k, v, ik, ltri)


def _merge_kernel(ya_ref, yb_ref, gate_ref, h_ref, wa_ref, wb_ref, wo_ref, fg_ref, wr_ref, br_ref, ltri_ref,
                  ho_ref, xc_ref, route_ref, cnt_ref):
    @pl.when(pl.program_id(0) == 0)
    def _():
        cnt_ref[...] = jnp.zeros_like(cnt_ref)

    gate = gate_ref[...]
    pa = _dot(ya_ref[...], wa_ref[...])
    pb = _dot(yb_ref[...], wb_ref[...])
    merged = jax.nn.sigmoid(gate[:, :D_MODEL]) * pa + jax.nn.sigmoid(gate[:, D_MODEL:]) * pb
    h = h_ref[...] + _dot(merged.astype(BF16), wo_ref[...])
    ho_ref[...] = h
    xn = h * lax.rsqrt(jnp.mean(h * h, axis=-1, keepdims=True) + NORM_EPS) * fg_ref[...]
    xh = xn.astype(BF16)
    xc_ref[:, :D_MODEL] = xh.astype(F32)

    xl = (xn - xh.astype(F32)).astype(BF16)
    r = _dot(xh, wr_ref[...])
    lg = r[:, :LANES] + r[:, LANES:] + _dot(xl, wr_ref[:, :LANES]) + br_ref[...]
    lane = lax.broadcasted_iota(jnp.int32, lg.shape, 1)
    lane_f = lane.astype(F32)
    big = float(LANES)
    is_g = (lane >= N_EXPERTS) & (lane < N_EXPERTS + N_GROUPS)
    gl = jnp.where(is_g, lg, NEG_INF)
    gmax = jnp.max(gl, axis=-1, keepdims=True)
    pg_top = 1.0 / jnp.sum(jnp.exp(gl - gmax), axis=-1, keepdims=True)
    g_lane = jnp.min(jnp.where(gl == gmax, lane_f, big), axis=-1, keepdims=True)
    e_lo = (g_lane - N_EXPERTS) * EXPERTS_PER_GROUP
    in_grp = (lane_f >= e_lo) & (lane_f < e_lo + EXPERTS_PER_GROUP)
    el = jnp.where(in_grp, lg, NEG_INF)
    ee = jnp.exp(el - jnp.max(el, axis=-1, keepdims=True))
    pe = ee / jnp.sum(ee, axis=-1, keepdims=True)
    pe = jnp.where(in_grp, pe, -1.0)
    p1 = jnp.max(pe, axis=-1, keepdims=True)
    i1 = jnp.min(jnp.where(pe == p1, lane_f, big), axis=-1, keepdims=True)
    pe2 = jnp.where(lane_f == i1, -1.0, pe)
    p2 = jnp.max(pe2, axis=-1, keepdims=True)
    i2 = jnp.min(jnp.where(pe2 == p2, lane_f, big), axis=-1, keepdims=True)
    tot = p1 + p2
    xc_ref[:, D_MODEL:] = jnp.where(lane_f == i1, p1 / tot * pg_top,
                                    jnp.where(lane_f == i2, p2 / tot * pg_top, 0.0))

    grp = g_lane - N_EXPERTS
    onehot = jnp.where(lane_f == grp, 1.0, 0.0)
    incl = _dot(ltri_ref[...], onehot.astype(BF16))
    seen = cnt_ref[0:1, :]
    rank = jnp.sum(onehot * (incl - 1.0 + seen), axis=-1, keepdims=True)
    route_ref[...] = jnp.where(lane == 0, grp, jnp.where(lane == 1, rank, 0.0))
    cnt_ref[...] = jnp.broadcast_to(seen + incl[incl.shape[0] - 1:, :], cnt_ref.shape)


def _merge(ya, yb, gates, h, wa, wb, wo, fg, wr, br):
    n = h.shape[0]
    tm = min(MERGE_TILE, n)
    ltri = jnp.asarray(np.tril(np.ones((tm, tm), np.float32)), BF16)
    row = lambda i: (i, 0)
    const = lambda i: (0, 0)
    return pl.pallas_call(
        _merge_kernel,
        grid=(n // tm,),
        in_specs=[
            pl.BlockSpec((tm, HGRN_WIDTH), row),
            pl.BlockSpec((tm, ATT_WIDTH), row),
            pl.BlockSpec((tm, 2 * D_MODEL), row),
            pl.BlockSpec((tm, D_MODEL), row),
            pl.BlockSpec((HGRN_WIDTH, D_MODEL), const),
            pl.BlockSpec((ATT_WIDTH, D_MODEL), const),
            pl.BlockSpec((D_MODEL, D_MODEL), const),
            pl.BlockSpec((1, D_MODEL), const),
            pl.BlockSpec((D_MODEL, 2 * LANES), const),
            pl.BlockSpec((1, LANES), const),
            pl.BlockSpec((tm, tm), const),
        ],
        out_specs=(pl.BlockSpec((tm, D_MODEL), row), pl.BlockSpec((tm, D_MODEL + LANES), row),
                   pl.BlockSpec((tm, LANES), row), pl.BlockSpec((SUBLANES, LANES), const)),
        out_shape=(jax.ShapeDtypeStruct((n, D_MODEL), F32), jax.ShapeDtypeStruct((n, D_MODEL + LANES), F32),
                   jax.ShapeDtypeStruct((n, LANES), F32), jax.ShapeDtypeStruct((SUBLANES, LANES), F32)),
        compiler_params=pltpu.CompilerParams(
            dimension_semantics=("arbitrary",), vmem_limit_bytes=VMEM_LIMIT),
        name="merge_router",
    )(ya, yb, gates, h, wa, wb, wo, fg, wr, br, ltri)


def _row_copies(n_rows, make_copy, whole_copy):
    def issue(r, c):
        make_copy(r).start()
        return c

    lax.fori_loop(0, n_rows, issue, 0, unroll=ROW_DMA_UNROLL)
    whole_copy.wait()


def _dispatch_kernel(dest_ref, x_ref, zeros_ref, xs_ref, sem):
    del zeros_ref
    tm = x_ref.shape[0]

    def row_copy(r):
        return pltpu.make_async_copy(x_ref.at[pl.ds(r, 1)], xs_ref.at[pl.ds(dest_ref[0, 0, r], 1)], sem)

    _row_copies(tm, row_copy, pltpu.make_async_copy(x_ref, xs_ref.at[pl.ds(0, tm)], sem))


def _experts_kernel(tile_group_ref, tile_on_ref, xs_ref, wg_ref, wu_ref, wd_ref, ys_ref):
    t = pl.program_id(0)
    tm = xs_ref.shape[0]

    @pl.when(tile_on_ref[t] == 1)
    def _():
        x = xs_ref[:, :D_MODEL].astype(BF16)
        c = xs_ref[:, D_MODEL:]
        lane = lax.broadcasted_iota(jnp.int32, c.shape, 1)
        first = tile_group_ref[t] * EXPERTS_PER_GROUP
        scale = jnp.concatenate(
            [jnp.broadcast_to(jnp.sum(jnp.where(lane == first + e, c, 0.0), axis=-1, keepdims=True), (tm, EXPERT_FF))
             for e in range(EXPERTS_PER_GROUP)], axis=1)
        a = _dot(x, wg_ref[0])
        u = _dot(x, wu_ref[0])
        act = (a * jax.nn.sigmoid(a)) * u * scale
        ys_ref[...] = _dot(act.astype(BF16), wd_ref[0])

    @pl.when(tile_on_ref[t] == 0)
    def _():
        ys_ref[...] = jnp.zeros_like(ys_ref)


def _combine_kernel(dest_ref, h_ref, ys_ref, o_ref, buf_ref, sem):
    tm = h_ref.shape[0]

    def row_copy(r):
        return pltpu.make_async_copy(ys_ref.at[pl.ds(dest_ref[0, 0, r], 1)], buf_ref.at[pl.ds(r, 1)], sem)

    _row_copies(tm, row_copy, pltpu.make_async_copy(ys_ref.at[pl.ds(0, tm)], buf_ref, sem))
    o_ref[...] = h_ref[...] + buf_ref[...]


def _expert_weights_kernel(g_ref, u_ref, d_ref, go_ref, uo_ref, do_ref):
    go_ref[0] = g_ref[0].astype(BF16)
    uo_ref[0] = u_ref[0].astype(BF16)
    do_ref[0] = d_ref[0].astype(BF16)


def _expert_weights(w_gate, w_up, w_down, layer):
    per = EXPERTS_PER_GROUP
    src = lambda shape: pl.BlockSpec((1,) + shape, lambda e: (layer * N_EXPERTS + e, 0, 0))
    side = pl.BlockSpec((1, D_MODEL, EXPERT_FF), lambda e: (e // per, 0, e % per))
    stack = pl.BlockSpec((1, EXPERT_FF, D_MODEL), lambda e: (e // per, e % per, 0))
    return pl.pallas_call(
        _expert_weights_kernel,
        grid=(N_EXPERTS,),
        in_specs=[src((D_MODEL, EXPERT_FF)), src((D_MODEL, EXPERT_FF)), src((EXPERT_FF, D_MODEL))],
        out_specs=(side, side, stack),
        out_shape=(jax.ShapeDtypeStruct((N_GROUPS, D_MODEL, per * EXPERT_FF), BF16),
                   jax.ShapeDtypeStruct((N_GROUPS, D_MODEL, per * EXPERT_FF), BF16),
                   jax.ShapeDtypeStruct((N_GROUPS, per * EXPERT_FF, D_MODEL), BF16)),
        compiler_params=pltpu.CompilerParams(
            dimension_semantics=("arbitrary",), vmem_limit_bytes=VMEM_LIMIT),
        name="expert_weights",
    )(w_gate, w_up, w_down)


def _moe(xc, route, counts, h, wg, wu, wd):
    n = h.shape[0]
    tm = min(MERGE_TILE, n)
    ts = min(MOE_SLOT_TILE, n)
    n_tiles = n // ts + N_GROUPS
    n_slots = n_tiles * ts

    grp = route[:, 0].astype(jnp.int32)
    rank = route[:, 1].astype(jnp.int32)
    cnt = counts[0, :N_GROUPS].astype(jnp.int32)
    padded = (cnt + ts - 1) // ts * ts
    ends = jnp.cumsum(padded)
    starts = ends - padded
    dest = (starts[grp] + rank).reshape(n // tm, 1, tm)
    tile_start = jnp.arange(n_tiles, dtype=jnp.int32) * ts
    tile_group = jnp.minimum(jnp.sum(tile_start[:, None] >= ends[None, :], axis=1), N_GROUPS - 1).astype(jnp.int32)
    tile_on = (tile_start < ends[N_GROUPS - 1]).astype(jnp.int32)

    tok = lambda i: (i, 0)
    smem_idx = pl.BlockSpec((1, 1, tm), lambda i: (i, 0, 0), memory_space=pltpu.SMEM)
    hbm = pl.BlockSpec(memory_space=pl.ANY)
    xs = pl.pallas_call(
        _dispatch_kernel,
        grid=(n // tm,),
        in_specs=[smem_idx, pl.BlockSpec((tm, D_MODEL + LANES), tok), hbm],
        out_specs=hbm,
        out_shape=jax.ShapeDtypeStruct((n_slots, D_MODEL + LANES), F32),
        scratch_shapes=[pltpu.SemaphoreType.DMA(())],
        input_output_aliases={2: 0},
        compiler_params=pltpu.CompilerParams(
            dimension_semantics=("arbitrary",), vmem_limit_bytes=VMEM_LIMIT),
        name="moe_dispatch",
    )(dest, xc, jnp.zeros((n_slots, D_MODEL + LANES), F32))

    wspec = lambda shape: pl.BlockSpec((1,) + shape, lambda t, tg, on: (tg[t], 0, 0))
    ys = pl.pallas_call(
        _experts_kernel,
        grid_spec=pltpu.PrefetchScalarGridSpec(
            num_scalar_prefetch=2,
            grid=(n_tiles,),
            in_specs=[pl.BlockSpec((ts, D_MODEL + LANES), lambda t, tg, on: (t, 0)),
                      wspec((D_MODEL, EXPERTS_PER_GROUP * EXPERT_FF)),
                      wspec((D_MODEL, EXPERTS_PER_GROUP * EXPERT_FF)),
                      wspec((EXPERTS_PER_GROUP * EXPERT_FF, D_MODEL))],
            out_specs=pl.BlockSpec((ts, D_MODEL), lambda t, tg, on: (t, 0)),
        ),
        out_shape=jax.ShapeDtypeStruct((n_slots, D_MODEL), F32),
        compiler_params=pltpu.CompilerParams(
            dimension_semantics=("arbitrary",), vmem_limit_bytes=VMEM_LIMIT),
        name="moe_experts",
    )(tile_group, tile_on, xs, wg, wu, wd)

    return pl.pallas_call(
        _combine_kernel,
        grid=(n // tm,),
        in_specs=[smem_idx, pl.BlockSpec((tm, D_MODEL), tok), hbm],
        out_specs=pl.BlockSpec((tm, D_MODEL), tok),
        out_shape=jax.ShapeDtypeStruct((n, D_MODEL), F32),
        scratch_shapes=[pltpu.VMEM((tm, D_MODEL), F32), pltpu.SemaphoreType.DMA(())],
        compiler_params=pltpu.CompilerParams(
            dimension_semantics=("arbitrary",), vmem_limit_bytes=VMEM_LIMIT),
        name="moe_combine",
    )(dest, h, ys)


def _final_kernel(h_ref, g_ref, o_ref, *, seq):
    step = 256 if seq % 256 == 0 else CHUNK
    for r in range(0, seq, step):
        x = h_ref[CHUNK + r:CHUNK + r + step, :]
        o_ref[0, r:r + step, :] = x * lax.rsqrt(jnp.mean(x * x, axis=-1, keepdims=True) + NORM_EPS) * g_ref[...]


def _final_norm(h, g, batch, lp, seq):
    return pl.pallas_call(
        functools.partial(_final_kernel, seq=seq),
        grid=(batch,),
        in_specs=[pl.BlockSpec((lp, D_MODEL), lambda b: (b, 0)), pl.BlockSpec((1, D_MODEL), lambda b: (0, 0))],
        out_specs=pl.BlockSpec((1, seq, D_MODEL), lambda b: (b, 0, 0)),
        out_shape=jax.ShapeDtypeStruct((batch, seq, D_MODEL), F32),
        compiler_params=pltpu.CompilerParams(
            dimension_semantics=("arbitrary",), vmem_limit_bytes=VMEM_LIMIT),
        name="final_norm",
    )(h, g)


def _rope_lane_tables(pos, head_dim):
    rot = head_dim // 4
    half = rot // 2
    inv = ROPE_THETA ** (-jnp.arange(0, rot, 2, dtype=F32) / rot)
    ang = pos[:, None] * inv[None, :]
    cos, sin = jnp.cos(ang), jnp.sin(ang)
    jj = np.arange(LANES) % head_dim
    first = jnp.asarray(jj < half)
    second = jnp.asarray((jj >= half) & (jj < rot))
    fidx = jnp.asarray(np.where(jj < half, jj, np.where(jj < rot, jj - half, 0)))
    cl, sl = cos[:, fidx], sin[:, fidx]
    c = jnp.where(first | second, cl, 1.0)
    s1 = jnp.where(second, sl, 0.0)
    s2 = jnp.where(first, -sl, 0.0)
    return c, s1, s2


def kernel(x, meta_tokens, mix_norm_g, w_in, hgrn_lb_logits, hgrn_norm_g, idx_k_norm_g, w_branch_hgrn,
           w_branch_dsa, w_out, ffn_norm_g, router_group_w, router_group_b, router_expert_w,
           router_expert_b, w_expert_gate, w_expert_up, w_expert_down, final_norm_g):
    batch, seq, _ = x.shape
    depth = w_in.shape[0]
    n_valid = CHUNK + seq
    assert seq % CHUNK == 0
    lp = -(-n_valid // Q_BLOCK) * Q_BLOCK
    n_chunks = n_valid // CHUNK
    topk = min(TOPK_MAX, (N_META + seq) // 4)

    meta = jnp.broadcast_to(meta_tokens[None].astype(x.dtype), (batch, N_META, D_MODEL))
    h = jnp.concatenate([jnp.zeros((batch, FRONT, D_MODEL), x.dtype), meta, x,
                         jnp.zeros((batch, lp - n_valid, D_MODEL), x.dtype)], axis=1)
    h = h.reshape(batch * lp, D_MODEL)

    pos = jnp.clip(jnp.arange(lp, dtype=jnp.int32) - FRONT, 0, N_META + seq - 1).astype(F32)
    tabs = jnp.stack(_rope_lane_tables(pos, ATT_HEAD_DIM) + _rope_lane_tables(pos, IDX_HEAD_DIM))

    p = jax.nn.softmax(hgrn_lb_logits.astype(F32), axis=0)
    cs = jnp.cumsum(p, axis=0)
    lower_bounds = cs - cs[0:1]

    rep = np.zeros((LANES, LANES), np.float32)
    for slot in range(IDX_HEADS_PER_LANE_GROUP):
        rep[np.arange(IDX_HEAD_DIM), slot * IDX_HEAD_DIM + np.arange(IDX_HEAD_DIM)] = 1.0
    rep = jnp.asarray(rep, BF16)
    ltri = jnp.asarray(np.tril(np.ones((Q_BLOCK, Q_BLOCK), np.float32)), BF16)

    pad_cols = jnp.zeros((D_MODEL, IDX_PAD - IDX_RAW), F32)
    for layer in range(depth):
        w = w_in[layer]
        w_p = jnp.concatenate([w[:, :COL_IDX + IDX_RAW], pad_cols, w[:, COL_IDX + IDX_RAW:]], axis=1).astype(BF16)
        ikg = jnp.pad(idx_k_norm_g[layer].astype(F32), (0, LANES - IDX_HEAD_DIM))[None]
        hg, gates, q, k, v, iq, ik, iw = _inproj(h, mix_norm_g[layer][None].astype(F32), w_p, tabs, ikg, rep, lp)
        ya = _hgrn(hg, lower_bounds[layer][None], hgrn_norm_g[layer][None].astype(F32), batch, lp, n_chunks)
        yb = _dsa(q, k, v, iq, ik, iw, ltri, batch, lp, topk, n_valid)
        wr = jnp.concatenate([router_expert_w[layer], router_group_w[layer],
                              jnp.zeros((D_MODEL, LANES - N_EXPERTS - N_GROUPS), F32)], axis=1)
        wr_hi = wr.astype(BF16)
        wr = jnp.concatenate([wr_hi, (wr - wr_hi.astype(F32)).astype(BF16)], axis=1)
        br = jnp.concatenate([router_expert_b[layer], router_group_b[layer],
                              jnp.zeros((LANES - N_EXPERTS - N_GROUPS,), F32)])[None]
        h, xc, route, counts = _merge(ya, yb, gates, h, w_branch_hgrn[layer].astype(BF16),
                                      w_branch_dsa[layer].astype(BF16), w_out[layer].astype(BF16),
                                      ffn_norm_g[layer][None].astype(F32), wr, br)
        wg, wu, wd = _expert_weights(w_expert_gate.reshape(-1, D_MODEL, EXPERT_FF),
                                     w_expert_up.reshape(-1, D_MODEL, EXPERT_FF),
                                     w_expert_down.reshape(-1, EXPERT_FF, D_MODEL), layer)
        h = _moe(xc, route, counts, h, wg, wu, wd)
    return _final_norm(h, final_norm_g[None].astype(F32), batch, lp, seq)
```
